```python
import math
import jax
import jax.numpy as jnp
from jax import lax
import numpy as np

D_MODEL = 2048
BATCH = 4
SEQ = 2048
DEPTH = 2

GRID_W = 64
CTX_LEN = 256
EPS = 1e-6
S5_WIDTH = 1024
S5_GROUP_CH = 16
S5_GROUPS = S5_WIDTH // S5_GROUP_CH
S5_STATE = 64
DN_HEADS = 8
DN_HEAD_DIM = 128
DN_WIDTH = DN_HEADS * DN_HEAD_DIM
DN_CONV = 3
DN_CHUNK = 64
IN_U_END = S5_WIDTH
IN_QKV_END = IN_U_END + 3 * DN_WIDTH
IN_Z_END = IN_QKV_END + DN_WIDTH
IN_BA_END = IN_Z_END + 4 * DN_HEADS
IN_TOTAL = IN_BA_END + 2 * D_MODEL
N_EXPERTS = 64
TOP_K = 8
N_GROUPS = 8
TOPK_GROUPS = 4
EXPERT_DIM = 512
SHARED_DIM = 512
ROUTE_SCALE = 2.5
EXPERT_BLOCK = 128

kernel_name = 'hybrid_s5_gdn_moe_prefix_dit'

F32 = jnp.float32


def rms_norm(x, w):
    xf = x.astype(F32)
    y = xf * lax.rsqrt(jnp.mean(xf * xf, axis=-1, keepdims=True) + EPS)
    return (y * w.astype(F32)).astype(x.dtype)


def modulate(h, shift, scale):
    return h * (1.0 + scale) + shift


def l2norm(x):
    return x * lax.rsqrt(jnp.sum(x * x, axis=-1, keepdims=True) + EPS)


def _flip(t):
    return jnp.flip(t, axis=1)


def _keep(t):
    return t


def raster_to_colmajor(t):
    b, l, ch = t.shape
    rows = l // GRID_W
    return t.reshape(b, rows, GRID_W, ch).transpose(0, 2, 1, 3).reshape(b, l, ch)


def colmajor_to_raster(t):
    b, l, ch = t.shape
    rows = l // GRID_W
    return t.reshape(b, GRID_W, rows, ch).transpose(0, 2, 1, 3).reshape(b, l, ch)


def centred_dwconv(x, w):
    k, ch = w.shape
    pad = (k - 1) // 2
    return lax.conv_general_dilated(x, w[:, None, :].astype(x.dtype), window_strides=(1,),
                                    padding=[(pad, pad)], dimension_numbers=('NWC', 'WIO', 'NWC'),
                                    feature_group_count=ch)


def s5_discretise(lam_re, lam_im, log_step, b_re, b_im):
    lr, li = lam_re.astype(F32), lam_im.astype(F32)
    step = jnp.exp(log_step.astype(F32))[:, None]
    mag = jnp.exp(lr * step)
    ab_re, ab_im = mag * jnp.cos(li * step), mag * jnp.sin(li * step)
    den = lr * lr + li * li
    nr = ab_re - 1.0
    cr = (nr * lr + ab_im * li) / den
    ci = (ab_im * lr - nr * li) / den
    br, bi = b_re.astype(F32), b_im.astype(F32)
    bb_re = cr[..., None] * br - ci[..., None] * bi
    bb_im = cr[..., None] * bi + ci[..., None] * br
    return ab_re, ab_im, bb_re, bb_im


def _complex_affine_combine(e1, e2):
    a1r, a1i, b1r, b1i = e1
    a2r, a2i, b2r, b2i = e2
    return (a2r * a1r - a2i * a1i, a2r * a1i + a2i * a1r,
            a2r * b1r - a2i * b1i + b2r, a2r * b1i + a2i * b1r + b2i)


def s5_scan(u, ab_re, ab_im, bb_re, bb_im, h0_re, h0_im):
    bu_re = jnp.einsum('gpj,blgj->blgp', bb_re, u)
    bu_im = jnp.einsum('gpj,blgj->blgp', bb_im, u)
    bu_re = bu_re.at[:, 0].add(ab_re * h0_re - ab_im * h0_im)
    bu_im = bu_im.at[:, 0].add(ab_re * h0_im + ab_im * h0_re)
    a_re = jnp.broadcast_to(ab_re, bu_re.shape)
    a_im = jnp.broadcast_to(ab_im, bu_im.shape)
    _, _, h_re, h_im = lax.associative_scan(_complex_affine_combine, (a_re, a_im, bu_re, bu_im), axis=1)
    return h_re, h_im


def s5_readout(h_re, h_im, c_re, c_im):
    return (jnp.einsum('gjp,blgp->blgj', c_re.astype(F32), h_re)
            - jnp.einsum('gjp,blgp->blgj', c_im.astype(F32), h_im))


def s5_glu(y, w_glu):
    g = jax.nn.gelu(y)
    return g * jax.nn.sigmoid(g @ w_glu.astype(F32))


def s5_branch(u_c, u_l, lam_re, lam_im, log_step, b_re, b_im, c_re, c_im, d_skip, w_glu, need_ctx):
    bsz = u_l.shape[0]
    uc = u_c.astype(F32).reshape(bsz, -1, S5_GROUPS, S5_GROUP_CH)
    ul = raster_to_colmajor(u_l).astype(F32).reshape(bsz, -1, S5_GROUPS, S5_GROUP_CH)
    dsk = d_skip.astype(F32).reshape(S5_GROUPS, S5_GROUP_CH)
    yl_parts, yc_parts = [ul * dsk], [uc * dsk]
    for d in range(2):
        fl = _flip if d == 1 else _keep
        ab_re, ab_im, bb_re, bb_im = s5_discretise(lam_re[d], lam_im[d], log_step[d], b_re[d], b_im[d])
        zero = jnp.zeros((bsz, S5_GROUPS, S5_STATE), F32)
        hc_re, hc_im = s5_scan(fl(uc), ab_re, ab_im, bb_re, bb_im, zero, zero)
        hl_re, hl_im = s5_scan(fl(ul), ab_re, ab_im, bb_re, bb_im, hc_re[:, -1], hc_im[:, -1])
        yl_parts.append(fl(s5_readout(hl_re, hl_im, c_re[d], c_im[d])))
        if need_ctx:
            yc_parts.append(fl(s5_readout(hc_re, hc_im, c_re[d], c_im[d])))
    y_l = (yl_parts[0] + yl_parts[1] + yl_parts[2]).reshape(bsz, -1, S5_WIDTH)
    out_l = colmajor_to_raster(s5_glu(y_l, w_glu)).astype(u_l.dtype)
    out_c = None
    if need_ctx:
        y_c = (yc_parts[0] + yc_parts[1] + yc_parts[2]).reshape(bsz, -1, S5_WIDTH)
        out_c = s5_glu(y_c, w_glu).astype(u_c.dtype)
    return out_c, out_l


def gated_delta_chunked(q, k, v, g, beta, s0):
    bsz, t, nh, dk = q.shape
    dv = v.shape[-1]
    n = t // DN_CHUNK
    c = DN_CHUNK

    def chunks(a):
        return a.reshape(bsz, n, c, nh, -1).transpose(1, 0, 3, 2, 4)

    qc, kc, vc = chunks(q), chunks(k), chunks(v)
    gc = g.reshape(bsz, n, c, nh).transpose(1, 0, 3, 2)
    bc = beta.reshape(bsz, n, c, nh).transpose(1, 0, 3, 2)
    gcum = jnp.cumsum(gc, axis=-1)
    causal = jnp.tril(jnp.ones((c, c), bool))
    strict = jnp.tril(jnp.ones((c, c), bool), -1)
    ldiff = gcum[..., :, None] - gcum[..., None, :]
    decay = jnp.where(causal, jnp.exp(jnp.where(causal, ldiff, 0.0)), 0.0)
    kbeta = kc * bc[..., None]
    a_mat = jnp.where(strict, jnp.einsum('nbhid,nbhjd->nbhij', kbeta, kc) * decay, 0.0)
    rhs = jnp.concatenate([vc * bc[..., None], kbeta * jnp.exp(gcum)[..., None]], axis=-1)
    sol = lax.linalg.triangular_solve(jnp.eye(c, dtype=F32) + a_mat, rhs, left_side=True, lower=True)
    u_c, w_c = sol[..., :dv], sol[..., dv:]
    qk = jnp.where(causal, jnp.einsum('nbhid,nbhjd->nbhij', qc, kc) * decay, 0.0)

    def step(s, inp):
        q_i, k_i, u_i, w_i, g_i, qk_i = inp
        v_new = u_i - jnp.einsum('bhck,bhkv->bhcv', w_i, s)
        o_i = (jnp.einsum('bhck,bhkv->bhcv', q_i * jnp.exp(g_i)[..., None], s)
               + jnp.einsum('bhij,bhjv->bhiv', qk_i, v_new))
        g_last = g_i[..., -1:]
        s = s * jnp.exp(g_last)[..., None] + jnp.einsum(
            'bhck,bhcv->bhkv', k_i * jnp.exp(g_last - g_i)[..., None], v_new)
        return s, o_i

    s_fin, o = lax.scan(step, s0, (qc, kc, u_c, w_c, gcum, qk))
    o = o.transpose(1, 0, 3, 2, 4).reshape(bsz, t, nh, dv)
    return o, s_fin


def dn_prepare(qkv, ba, conv_w, a_log, dt_bias):
    bsz, t, _ = qkv.shape
    qkv = jax.nn.silu(centred_dwconv(qkv, conv_w)).astype(F32).reshape(bsz, t, 3, DN_HEADS, DN_HEAD_DIM)
    q = l2norm(qkv[:, :, 0]) * (DN_HEAD_DIM ** -0.5)
    k = l2norm(qkv[:, :, 1])
    v = qkv[:, :, 2]
    ba = ba.astype(F32).reshape(bsz, t, 2, 2, DN_HEADS)
    beta = jax.nn.sigmoid(ba[:, :, 0])
    g = -jnp.exp(a_log.astype(F32)) * jax.nn.softplus(ba[:, :, 1] + dt_bias.astype(F32))
    return q, k, v, beta, g


def dn_gated_out(o, z, norm_w):
    bsz, t = o.shape[0], o.shape[1]
    zz = z.astype(F32).reshape(o.shape)
    return (rms_norm(o, norm_w) * jax.nn.silu(zz)).reshape(bsz, t, DN_WIDTH)


def deltanet_branch(p_c, p_l, conv_w, a_log, dt_bias, norm_w, need_ctx):
    qc, kc, vc, bc, gc = dn_prepare(p_c[..., IN_U_END:IN_QKV_END], p_c[..., IN_Z_END:IN_BA_END],
                                    conv_w, a_log, dt_bias)
    ql, kl, vl, bl, gl = dn_prepare(p_l[..., IN_U_END:IN_QKV_END], p_l[..., IN_Z_END:IN_BA_END],
                                    conv_w, a_log, dt_bias)
    bsz = p_l.shape[0]
    oc_parts, ol_parts = [], []
    for d in range(2):
        fl = _flip if d == 1 else _keep
        s0 = jnp.zeros((bsz, DN_HEADS, DN_HEAD_DIM, DN_HEAD_DIM), F32)
        oc, s_ctx = gated_delta_chunked(fl(qc), fl(kc), fl(vc), fl(gc[:, :, d]), fl(bc[:, :, d]), s0)
        ol, _ = gated_delta_chunked(fl(ql), fl(kl), fl(vl), fl(gl[:, :, d]), fl(bl[:, :, d]), s_ctx)
        oc_parts.append(fl(oc))
        ol_parts.append(fl(ol))
    out_l = dn_gated_out(ol_parts[0] + ol_parts[1], p_l[..., IN_QKV_END:IN_Z_END], norm_w).astype(p_l.dtype)
    out_c = None
    if need_ctx:
        out_c = dn_gated_out(oc_parts[0] + oc_parts[1], p_c[..., IN_QKV_END:IN_Z_END], norm_w).astype(p_c.dtype)
    return out_c, out_l


def branch_merge(p, y_s5, y_dn, w_br_s5, w_br_dn, w_out):
    g_s5 = p[..., IN_BA_END:IN_BA_END + D_MODEL]
    g_dn = p[..., IN_BA_END + D_MODEL:IN_TOTAL]
    m = jax.nn.sigmoid(g_s5) * (y_s5 @ w_br_s5) + jax.nn.sigmoid(g_dn) * (y_dn @ w_br_dn)
    return m @ w_out


def swiglu(x, wg, wu, wd):
    return (jax.nn.silu(x @ wg) * (x @ wu)) @ wd


def routed_experts(h, eidx, wts, w_gate, w_up, w_down):
    t, d = h.shape
    n_assign = t * TOP_K
    flat_e = eidx.reshape(n_assign).astype(jnp.int32)
    flat_tok = jnp.arange(n_assign, dtype=jnp.int32) // TOP_K
    flat_w = wts.reshape(n_assign)
    order = jnp.argsort(flat_e)
    se, stok, sw = flat_e[order], flat_tok[order], flat_w[order]
    counts = jnp.zeros((N_EXPERTS,), jnp.int32).at[flat_e].add(1)
    padded = (counts + EXPERT_BLOCK - 1) // EXPERT_BLOCK * EXPERT_BLOCK
    pad_end = jnp.cumsum(padded)
    pad_start = pad_end - padded
    start = jnp.cumsum(counts) - counts
    dest = pad_start[se] + jnp.arange(n_assign, dtype=jnp.int32) - start[se]
    n_blocks = (n_assign + N_EXPERTS * (EXPERT_BLOCK - 1) + EXPERT_BLOCK - 1) // EXPERT_BLOCK
    rows = n_blocks * EXPERT_BLOCK
    tok_buf = jnp.zeros((rows,), jnp.int32).at[dest].set(stok)
    w_buf = jnp.zeros((rows,), h.dtype).at[dest].set(sw)
    block_e = jnp.minimum(jnp.searchsorted(pad_end, jnp.arange(n_blocks, dtype=jnp.int32) * EXPERT_BLOCK,
                                           side='right'), N_EXPERTS - 1)

    def block_fn(args):
        e, toks, w = args
        return swiglu(h[toks], w_gate[e], w_up[e], w_down[e]) * w[:, None]

    y = lax.map(block_fn, (block_e, tok_buf.reshape(n_blocks, EXPERT_BLOCK), w_buf.reshape(n_blocks, EXPERT_BLOCK)))
    return jnp.zeros_like(h).at[tok_buf].add(y.reshape(rows, d))


def moe_ffn(h, w_router, e_bias, w_gate, w_up, w_down, ws_gate, ws_up, ws_down):
    t = h.shape[0]
    scores = jax.nn.sigmoid((h @ w_router).astype(F32))
    sel = (scores + e_bias.astype(F32)).reshape(t, N_GROUPS, N_EXPERTS // N_GROUPS)
    group_score = lax.top_k(sel, 2)[0].sum(-1)
    _, gidx = lax.top_k(group_score, TOPK_GROUPS)
    gmask = jnp.any(gidx[:, :, None] == jnp.arange(N_GROUPS)[None, None, :], axis=1)
    sel = jnp.where(gmask[:, :, None], sel, -jnp.inf).reshape(t, N_EXPERTS)
    _, eidx = lax.top_k(sel, TOP_K)
    wts = jnp.take_along_axis(scores, eidx, axis=-1)
    wts = wts / jnp.sum(wts, axis=-1, keepdims=True) * ROUTE_SCALE
    routed = routed_experts(h, eidx, wts.astype(h.dtype), w_gate, w_up, w_down)
    return routed + swiglu(h, ws_gate, ws_up, ws_down)


def setup_inputs(seed: int = 0) -> dict:
    key = jax.random.key(seed)
    ks = iter(list(jax.random.split(key, 40)))
    D, L, G, P, J, H, E = D_MODEL, DEPTH, S5_GROUPS, S5_STATE, S5_GROUP_CH, DN_HEADS, N_EXPERTS

    def nrm(shape, scale):
        return jax.random.normal(next(ks), shape, F32) * scale

    def uni(shape, lo, hi):
        return jax.random.uniform(next(ks), shape, F32, lo, hi)

    dt = jnp.exp(uni((L, 2, H), math.log(1e-3), math.log(1e-1)))
    return {
        'x': nrm((BATCH, SEQ, D), 1.0),
        'c': nrm((BATCH, D), 1.0),
        'ctx': nrm((BATCH, CTX_LEN, D), 1.0),
        'c_ctx': nrm((D,), 1.0),
        'w_mod': nrm((L, D, 6 * D), 0.5 * D ** -0.5),
        'b_mod': nrm((L, 6 * D), 0.02),
        'norm_mix_pre': 1.0 + nrm((L, D), 0.02),
        'norm_mix_post': 1.0 + nrm((L, D), 0.02),
        'norm_ffn_pre': 1.0 + nrm((L, D), 0.02),
        'norm_ffn_post': 1.0 + nrm((L, D), 0.02),
        'w_in': nrm((L, D, IN_TOTAL), D ** -0.5),
        's5_lam_re': -0.5 + nrm((L, 2, G, P), 0.01),
        's5_lam_im': jnp.pi * jnp.arange(P, dtype=F32) + nrm((L, 2, G, P), 0.01),
        's5_log_step': uni((L, 2, G), math.log(1e-3), math.log(1e-1)),
        's5_b_re': nrm((L, 2, G, P, J), (2 * J) ** -0.5),
        's5_b_im': nrm((L, 2, G, P, J), (2 * J) ** -0.5),
        's5_c_re': nrm((L, 2, G, J, P), (2 * P) ** -0.5),
        's5_c_im': nrm((L, 2, G, J, P), (2 * P) ** -0.5),
        's5_d': nrm((L, S5_WIDTH), 1.0),
        's5_w_glu': nrm((L, S5_WIDTH, S5_WIDTH), S5_WIDTH ** -0.5),
        'dn_conv': nrm((L, DN_CONV, 3 * DN_WIDTH), DN_CONV ** -0.5),
        'dn_a_log': jnp.log(uni((L, 2, H), 1.0, 16.0)),
        'dn_dt_bias': dt + jnp.log(-jnp.expm1(-dt)),
        'dn_norm': 1.0 + nrm((L, DN_HEAD_DIM), 0.02),
        'w_br_s5': nrm((L, S5_WIDTH, D), S5_WIDTH ** -0.5),
        'w_br_dn': nrm((L, DN_WIDTH, D), DN_WIDTH ** -0.5),
        'w_out': nrm((L, D, D), D ** -0.5),
        'moe_router': nrm((L, D, E), D ** -0.5),
        'moe_bias': nrm((L, E), 0.01),
        'moe_w_gate': nrm((L, E, D, EXPERT_DIM), D ** -0.5),
        'moe_w_up': nrm((L, E, D, EXPERT_DIM), D ** -0.5),
        'moe_w_down': nrm((L, E, EXPERT_DIM, D), EXPERT_DIM ** -0.5),
        'sh_w_gate': nrm((L, D, SHARED_DIM), D ** -0.5),
        'sh_w_up': nrm((L, D, SHARED_DIM), D ** -0.5),
        'sh_w_down': nrm((L, SHARED_DIM, D), SHARED_DIM ** -0.5),
    }


def reference(x, c, ctx, c_ctx, w_mod, b_mod, norm_mix_pre, norm_mix_post, norm_ffn_pre, norm_ffn_post,
              w_in, s5_lam_re, s5_lam_im, s5_log_step, s5_b_re, s5_b_im, s5_c_re, s5_c_im, s5_d, s5_w_glu,
              dn_conv, dn_a_log, dn_dt_bias, dn_norm, w_br_s5, w_br_dn, w_out,
              moe_router, moe_bias, moe_w_gate, moe_w_up, moe_w_down, sh_w_gate, sh_w_up, sh_w_down):
    x_lat, x_ctx = x, ctx
    for i in range(DEPTH):
        need_ctx = i < DEPTH - 1
        mods_l = jnp.split((jax.nn.silu(c) @ w_mod[i] + b_mod[i])[:, None, :], 6, axis=-1)
        mods_c = jnp.split(jax.nn.silu(c_ctx) @ w_mod[i] + b_mod[i], 6, axis=-1)
        p_l = modulate(rms_norm(x_lat, norm_mix_pre[i]), mods_l[0], mods_l[1]) @ w_in[i]
        p_c = modulate(rms_norm(x_ctx, norm_mix_pre[i]), mods_c[0], mods_c[1]) @ w_in[i]
        s5_c, s5_l = s5_branch(p_c[..., :IN_U_END], p_l[..., :IN_U_END], s5_lam_re[i], s5_lam_im[i],
                               s5_log_step[i], s5_b_re[i], s5_b_im[i], s5_c_re[i], s5_c_im[i], s5_d[i],
                               s5_w_glu[i], need_ctx)
        dn_c, dn_l = deltanet_branch(p_c, p_l, dn_conv[i], dn_a_log[i], dn_dt_bias[i], dn_norm[i], need_ctx)
        mix_l = branch_merge(p_l, s5_l, dn_l, w_br_s5[i], w_br_dn[i], w_out[i])
        x_lat = x_lat + mods_l[2] * rms_norm(mix_l, norm_mix_post[i])
        if need_ctx:
            mix_c = branch_merge(p_c, s5_c, dn_c, w_br_s5[i], w_br_dn[i], w_out[i])
            x_ctx = x_ctx + mods_c[2] * rms_norm(mix_c, norm_mix_post[i])
        h_l = modulate(rms_norm(x_lat, norm_ffn_pre[i]), mods_l[3], mods_l[4]).reshape(-1, D_MODEL)
        moe_args = (moe_router[i], moe_bias[i], moe_w_gate[i], moe_w_up[i], moe_w_down[i],
                    sh_w_gate[i], sh_w_up[i], sh_w_down[i])
        if need_ctx:
            h_c = modulate(rms_norm(x_ctx, norm_ffn_pre[i]), mods_c[3], mods_c[4]).reshape(-1, D_MODEL)
            n_c = h_c.shape[0]
            f = moe_ffn(jnp.concatenate([h_c, h_l], axis=0), *moe_args)
            f_c, f_l = f[:n_c], f[n_c:]
            x_ctx = x_ctx + mods_c[5] * rms_norm(f_c.reshape(x_ctx.shape), norm_ffn_post[i])
        else:
            f_l = moe_ffn(h_l, *moe_args)
        x_lat = x_lat + mods_l[5] * rms_norm(f_l.reshape(x_lat.shape), norm_ffn_post[i])
    return x_lat
```

```python
import functools

import jax
import jax.numpy as jnp
from jax import lax
from jax.experimental import pallas as pl
from jax.experimental.pallas import tpu as pltpu

F32 = jnp.float32
BF16 = jnp.bfloat16

EPS = 1e-6
GRID_W = 64
S5_CHUNK = 16
DN_CHUNK = 64
TOP_K = 8
N_GROUPS = 8
TOPK_GROUPS = 4
ROUTE_SCALE = 2.5
MOE_BLOCK = 256

V7X_VMEM_LIMIT = 56 * 1024 * 1024
LANES = 128
ROW_TILE = 1024


def _cparams(n_axes, vmem=V7X_VMEM_LIMIT):
    return pltpu.CompilerParams(dimension_semantics=("arbitrary",) * n_axes, vmem_limit_bytes=vmem)


def _silu(x):
    return x * jax.nn.sigmoid(x)


def _row_tile(rows, tm=None):
    tm = min(tm or ROW_TILE, ROW_TILE, rows)
    assert rows % tm == 0, (rows, tm)
    return tm


def _col_tile(n, tn, col0=0):
    tn = min(tn, n)
    while n % tn or col0 % tn:
        tn -= LANES
    return tn


def _wspec(w, layer, block, index_map, **kw):
    if w.ndim == len(block):
        return pl.BlockSpec(block, index_map, **kw)
    return pl.BlockSpec((None,) + tuple(block), lambda *a: (layer,) + tuple(index_map(*a)), **kw)


def _mm_kernel(x_ref, w_ref, b_ref, o_ref, wbf_ref):
    @pl.when(pl.program_id(1) == 0)
    def _():
        wbf_ref[...] = w_ref[...].astype(BF16)

    acc = jnp.dot(x_ref[...].astype(BF16), wbf_ref[...], preferred_element_type=F32)
    o_ref[...] = (acc + b_ref[...]).astype(o_ref.dtype)


def matmul(x, w, bias=None, *, layer=0, n_cols=None, col0=0, out_dtype=F32, tm=None, tn=1024, rows=None,
           name="matmul"):
    m, k = x.shape
    rows = m if rows is None else rows
    n_cols = w.shape[-1] - col0 if n_cols is None else n_cols
    tm = _row_tile(rows, tm)
    tn = _col_tile(n_cols, tn, col0)
    assert rows % tm == 0 and n_cols % tn == 0 and col0 % tn == 0, (rows, tm, n_cols, tn, col0)
    if bias is None:
        bias = jnp.zeros((1, n_cols), F32)
    cb0 = col0 // tn
    return pl.pallas_call(
        _mm_kernel,
        out_shape=jax.ShapeDtypeStruct((rows, n_cols), out_dtype),
        grid=(n_cols // tn, rows // tm),
        in_specs=[
            pl.BlockSpec((tm, k), lambda j, i: (i, 0)),
            _wspec(w, layer, (k, tn), lambda j, i: (0, j + cb0)),
            pl.BlockSpec((1, tn), lambda j, i: (0, j)),
        ],
        out_specs=pl.BlockSpec((tm, tn), lambda j, i: (i, j)),
        scratch_shapes=[pltpu.VMEM((k, tn), BF16)],
        compiler_params=_cparams(2),
        name=name,
    )(x, w, bias)


def _prenorm_kernel(x_ref, w_ref, sh_ref, sc_ref, o_ref):
    x = x_ref[...]
    y = x * lax.rsqrt(jnp.mean(x * x, axis=-1, keepdims=True) + EPS) * w_ref[...]
    o_ref[...] = (y * (1.0 + sc_ref[0]) + sh_ref[0]).astype(o_ref.dtype)


def prenorm(x, w, shift, scale, set_of_tile, *, rows=None, tm=None):
    m, d = x.shape
    rows = m if rows is None else rows
    tm = _row_tile(rows, tm)
    return pl.pallas_call(
        _prenorm_kernel,
        out_shape=jax.ShapeDtypeStruct((rows, d), BF16),
        grid=(rows // tm,),
        in_specs=[
            pl.BlockSpec((tm, d), lambda i: (i, 0)),
            pl.BlockSpec((1, d), lambda i: (0, 0)),
            pl.BlockSpec((1, 1, d), lambda i: (set_of_tile(i, tm), 0, 0)),
            pl.BlockSpec((1, 1, d), lambda i: (set_of_tile(i, tm), 0, 0)),
        ],
        out_specs=pl.BlockSpec((tm, d), lambda i: (i, 0)),
        compiler_params=_cparams(1),
        name="prenorm",
    )(x, w.reshape(1, d), shift, scale)


def _glu_kernel(y_ref, w_ref, o_ref, wbf_ref, *, tn):
    @pl.when(pl.program_id(1) == 0)
    def _():
        wbf_ref[...] = w_ref[...].astype(BF16)

    j = pl.program_id(0)
    g = jax.nn.gelu(y_ref[...].astype(F32))
    acc = jnp.dot(g.astype(BF16), wbf_ref[...], preferred_element_type=F32)
    gj = jax.nn.gelu(y_ref[:, pl.ds(pl.multiple_of(j * tn, tn), tn)].astype(F32))
    o_ref[...] = (gj * jax.nn.sigmoid(acc)).astype(o_ref.dtype)


def s5_glu(y, w, *, layer, rows, tm=None, tn=512):
    m, k = y.shape
    tm = _row_tile(rows, tm)
    tn = _col_tile(k, tn)
    return pl.pallas_call(
        functools.partial(_glu_kernel, tn=tn),
        out_shape=jax.ShapeDtypeStruct((rows, k), BF16),
        grid=(k // tn, rows // tm),
        in_specs=[
            pl.BlockSpec((tm, k), lambda j, i: (i, 0)),
            _wspec(w, layer, (k, tn), lambda j, i: (0, j)),
        ],
        out_specs=pl.BlockSpec((tm, tn), lambda j, i: (i, j)),
        scratch_shapes=[pltpu.VMEM((k, tn), BF16)],
        compiler_params=_cparams(2),
        name="s5_glu",
    )(y, w)


def _merge_kernel(a_ref, b_ref, ga_ref, gb_ref, wa_ref, wb_ref, o_ref, wabf_ref, wbbf_ref):
    @pl.when(pl.program_id(1) == 0)
    def _():
        wabf_ref[...] = wa_ref[...].astype(BF16)
        wbbf_ref[...] = wb_ref[...].astype(BF16)

    ya = jnp.dot(a_ref[...], wabf_ref[...], preferred_element_type=F32)
    yb = jnp.dot(b_ref[...], wbbf_ref[...], preferred_element_type=F32)
    m = jax.nn.sigmoid(ga_ref[...].astype(F32)) * ya + jax.nn.sigmoid(gb_ref[...].astype(F32)) * yb
    o_ref[...] = m.astype(o_ref.dtype)


def branch_merge(ya, yb, gates, wa, wb, *, layer, rows, tm=None, tn=512):
    ka, kb = ya.shape[1], yb.shape[1]
    d = wa.shape[-1]
    tm = _row_tile(rows, tm)
    tn = _col_tile(d, tn)
    nb = d // tn
    return pl.pallas_call(
        _merge_kernel,
        out_shape=jax.ShapeDtypeStruct((rows, d), BF16),
        grid=(nb, rows // tm),
        in_specs=[
            pl.BlockSpec((tm, ka), lambda j, i: (i, 0)),
            pl.BlockSpec((tm, kb), lambda j, i: (i, 0)),
            pl.BlockSpec((tm, tn), lambda j, i: (i, j)),
            pl.BlockSpec((tm, tn), lambda j, i: (i, j + nb)),
            _wspec(wa, layer, (ka, tn), lambda j, i: (0, j)),
            _wspec(wb, layer, (kb, tn), lambda j, i: (0, j)),
        ],
        out_specs=pl.BlockSpec((tm, tn), lambda j, i: (i, j)),
        scratch_shapes=[pltpu.VMEM((ka, tn), BF16), pltpu.VMEM((kb, tn), BF16)],
        compiler_params=_cparams(2),
        name="branch_merge",
    )(ya, yb, gates, gates, wa, wb)


def _outproj_kernel(m_ref, w_ref, x_ref, nw_ref, g_ref, o_ref, wbf_ref):
    @pl.when(pl.program_id(0) == 0)
    def _():
        wbf_ref[...] = w_ref[...].astype(BF16)

    y = jnp.dot(m_ref[...], wbf_ref[...], preferred_element_type=F32)
    yn = y * lax.rsqrt(jnp.mean(y * y, axis=-1, keepdims=True) + EPS) * nw_ref[...]
    o_ref[...] = x_ref[...] + g_ref[0] * yn


def outproj_residual(m, w, x, nw, gate, set_of_tile, *, layer, rows, tm=512):
    d = w.shape[-1]
    k = w.shape[-2]
    tm = _row_tile(rows, tm)
    return pl.pallas_call(
        _outproj_kernel,
        out_shape=jax.ShapeDtypeStruct((rows, d), F32),
        grid=(rows // tm,),
        in_specs=[
            pl.BlockSpec((tm, k), lambda i: (i, 0)),
            _wspec(w, layer, (k, d), lambda i: (0, 0), pipeline_mode=pl.Buffered(1)),
            pl.BlockSpec((tm, d), lambda i: (i, 0)),
            pl.BlockSpec((1, d), lambda i: (0, 0)),
            pl.BlockSpec((1, 1, d), lambda i: (set_of_tile(i, tm), 0, 0)),
        ],
        out_specs=pl.BlockSpec((tm, d), lambda i: (i, 0)),
        scratch_shapes=[pltpu.VMEM((k, d), BF16)],
        compiler_params=_cparams(1),
        name="outproj_residual",
    )(m, w, x, nw.reshape(1, d), gate)


def s5_operators(lam_re, lam_im, log_step, b_re, b_im, c_re, c_im, n_levels):
    tc = S5_CHUNK
    hp = lax.Precision.HIGHEST
    lr, li = lam_re.astype(F32), lam_im.astype(F32)
    step = jnp.exp(log_step.astype(F32))[..., None]

    def apow(l):
        mag = jnp.exp(lr * step * l)
        return mag * jnp.cos(li * step * l), mag * jnp.sin(li * step * l)

    ab_re, ab_im = apow(1.0)
    den = lr * lr + li * li
    nr = ab_re - 1.0
    cr = (nr * lr + ab_im * li) / den
    ci = (ab_im * lr - nr * li) / den
    br, bi = b_re.astype(F32), b_im.astype(F32)
    bb_re = cr[..., None] * br - ci[..., None] * bi
    bb_im = cr[..., None] * bi + ci[..., None] * br
    lags = jnp.arange(tc + 1, dtype=F32)[:, None, None, None]
    pw_re, pw_im = apow(lags)
    pw_re, pw_im = jnp.moveaxis(pw_re, 0, 2), jnp.moveaxis(pw_im, 0, 2)
    cre, cim = c_re.astype(F32), c_im.astype(F32)
    cp_re = cre[:, :, None] * pw_re[:, :, :, None] - cim[:, :, None] * pw_im[:, :, :, None]
    cp_im = cre[:, :, None] * pw_im[:, :, :, None] + cim[:, :, None] * pw_re[:, :, :, None]
    kl = (jnp.einsum('dgljp,dgpi->dglji', cp_re[:, :, :tc], bb_re, precision=hp)
          - jnp.einsum('dgljp,dgpi->dglji', cp_im[:, :, :tc], bb_im, precision=hp))
    s_idx = jnp.arange(tc)[:, None]
    t_idx = jnp.arange(tc)[None, :]
    lag = t_idx - s_idx
    tmat = jnp.where((lag >= 0)[None, None, :, :, None, None], kl[:, :, jnp.clip(lag, 0, tc - 1)], 0.0)
    tmat = tmat.transpose(0, 1, 2, 5, 3, 4)
    pr, pi = pw_re[:, :, tc - 1 - jnp.arange(tc)], pw_im[:, :, tc - 1 - jnp.arange(tc)]
    win_re = pr[:, :, :, None, :] * bb_re.transpose(0, 1, 3, 2)[:, :, None] - pi[:, :, :, None, :] * bb_im.transpose(0, 1, 3, 2)[:, :, None]
    win_im = pr[:, :, :, None, :] * bb_im.transpose(0, 1, 3, 2)[:, :, None] + pi[:, :, :, None, :] * bb_re.transpose(0, 1, 3, 2)[:, :, None]
    win = jnp.concatenate([win_re, win_im], axis=-1)
    wo_re = cp_re[:, :, 1:].transpose(0, 1, 4, 2, 3)
    wo_im = -cp_im[:, :, 1:].transpose(0, 1, 4, 2, 3)
    wout = jnp.concatenate([wo_re, wo_im], axis=2)
    tmat = jnp.stack([tmat[0], tmat[1, :, ::-1, :, ::-1, :]])
    win = jnp.stack([win[0], win[1, :, ::-1]])
    wout = jnp.stack([wout[0], wout[1, :, :, ::-1]])
    g = lr.shape[1]
    j = br.shape[-1]
    p = lr.shape[-1]
    rows = []
    for k in range(n_levels):
        ar, ai = apow(float(tc * 2 ** k))
        rows.append(jnp.concatenate([ar, ar], axis=-1))
        rows.append(jnp.concatenate([-ai, ai], axis=-1))
    apw = jnp.stack(rows, axis=2)
    return (tmat.reshape(2, g, tc * j, tc * j).transpose(1, 0, 2, 3).astype(BF16),
            win.reshape(2, g, tc * j, 2 * p).transpose(1, 0, 2, 3).astype(BF16),
            wout.reshape(2, g, 2 * p, tc * j).transpose(1, 0, 2, 3).astype(BF16),
            apw.transpose(1, 0, 2, 3))


def _s5_kernel(u_ref, tm_ref, win_ref, wout_ref, apw_ref, dsk_ref, y_ref, *, nb, ctx_rows, n_levels, p):
    u = u_ref[0]
    n = u.shape[0]
    row = lax.broadcasted_iota(jnp.int32, (n, 1), 0)
    y = u.astype(F32) * dsk_ref[0]
    for d in range(2):
        y = y + jnp.dot(u, tm_ref[0, d], preferred_element_type=F32)
        x = jnp.dot(u, win_ref[0, d], preferred_element_type=F32)
        if d == 0:
            def shift(a, s):
                return jnp.where(row >= s, pltpu.roll(a, s, 0), 0.0)
        else:
            if ctx_rows:
                x = pltpu.roll(x, n - ctx_rows, 0)

            def shift(a, s):
                return jnp.where(row < n - s, pltpu.roll(a, n - s, 0), 0.0)
        x = shift(x, nb)
        for k in range(n_levels):
            sh = shift(x, nb * 2 ** k)
            a1 = apw_ref[0, d, 2 * k:2 * k + 1, :]
            a2 = apw_ref[0, d, 2 * k + 1:2 * k + 2, :]
            x = x + a1 * sh + a2 * pltpu.roll(sh, p, 1)
        if d == 1 and ctx_rows:
            x = pltpu.roll(x, ctx_rows, 0)
        y = y + jnp.dot(x.astype(BF16), wout_ref[0, d], preferred_element_type=F32)
    y_ref[0] = y.astype(y_ref.dtype)


def s5_scan(uc, tmat, win, wout, apw, dsk, *, nb, ctx_rows, n_levels):
    g, n, lanes = uc.shape
    p2 = win.shape[-1]
    return pl.pallas_call(
        functools.partial(_s5_kernel, nb=nb, ctx_rows=ctx_rows, n_levels=n_levels, p=p2 // 2),
        out_shape=jax.ShapeDtypeStruct((g, n, lanes), BF16),
        grid=(g,),
        in_specs=[
            pl.BlockSpec((1, n, lanes), lambda i: (i, 0, 0)),
            pl.BlockSpec((1, 2, lanes, lanes), lambda i: (i, 0, 0, 0)),
            pl.BlockSpec((1, 2, lanes, p2), lambda i: (i, 0, 0, 0)),
            pl.BlockSpec((1, 2, p2, lanes), lambda i: (i, 0, 0, 0)),
            pl.BlockSpec((1, 2, 2 * n_levels, p2), lambda i: (i, 0, 0, 0)),
            pl.BlockSpec((1, 1, lanes), lambda i: (i, 0, 0)),
        ],
        out_specs=pl.BlockSpec((1, n, lanes), lambda i: (i, 0, 0)),
        compiler_params=_cparams(1),
        name="s5_scan",
    )(uc, tmat, win, wout, apw, dsk)


def s5_mix(u_rows, ops, d_skip, *, bsz, seq, ctx):
    tmat, win, wout, apw = ops
    g = tmat.shape[0]
    lanes = tmat.shape[-1]
    tc = S5_CHUNK
    j = lanes // tc
    hgt = seq // GRID_W
    ul = u_rows[:bsz * seq].reshape(bsz, hgt, GRID_W, g, j).transpose(0, 2, 1, 3, 4).reshape(bsz, seq, g, j)
    uc = u_rows[bsz * seq:].reshape(bsz, ctx, g, j)
    useq = jnp.concatenate([uc, ul], axis=1)
    nc = (seq + ctx) // tc
    uch = useq.reshape(bsz, nc, tc, g, j).transpose(3, 1, 0, 2, 4).reshape(g, nc * bsz, lanes)
    n_levels = apw.shape[2] // 2
    dsk = jnp.tile(d_skip.astype(F32).reshape(g, 1, j), (1, tc, 1)).reshape(g, 1, lanes)
    ych = s5_scan(uch.astype(BF16), tmat, win, wout, apw, dsk, nb=bsz, ctx_rows=(ctx // tc) * bsz, n_levels=n_levels)
    yseq = ych.reshape(g, nc, bsz, tc, j).transpose(2, 1, 3, 0, 4).reshape(bsz, seq + ctx, g * j)
    yc = yseq[:, :ctx].reshape(bsz * ctx, g * j)
    yl = yseq[:, ctx:].reshape(bsz, GRID_W, hgt, g * j).transpose(0, 2, 1, 3).reshape(bsz * seq, g * j)
    return jnp.concatenate([yl, yc], axis=0)


def _softplus(x):
    return jnp.maximum(x, 0.0) + jnp.log1p(jnp.exp(-jnp.abs(x)))


def _dn_kernel(ql_ref, kl_ref, vl_ref, qc_ref, kc_ref, vc_ref, bal_ref, bac_ref, zl_ref, zc_ref,
               cw_ref, lp_ref, nw_ref, ol_ref, oc_ref,
               u_s, w_s, qe_s, ket_s, qk_s, el_s, o_s, *, n_heads, ctx):
    c = DN_CHUNK
    seq = ql_ref.shape[0]
    t = seq + ctx
    dk = ql_ref.shape[1]
    nch = t // c
    ncc = ctx // c
    h = pl.program_id(1)
    row = lax.broadcasted_iota(jnp.int32, (t, 1), 0)
    rowc = row % c
    lane = lax.broadcasted_iota(jnp.int32, (1, dk), 1)

    first = (row == 0) | (row == ctx)
    last = (row == ctx - 1) | (row == t - 1)

    def conv_silu(xc_ref, xl_ref, kind):
        x = jnp.concatenate([xc_ref[...], xl_ref[...]], axis=0).astype(F32)
        xp = jnp.where(first, 0.0, pltpu.roll(x, 1, 0))
        xn = jnp.where(last, 0.0, pltpu.roll(x, t - 1, 0))
        w = cw_ref[0, kind]
        return _silu(xp * w[0:1] + x * w[1:2] + xn * w[2:3])

    def l2n(x):
        return x * lax.rsqrt(jnp.sum(x * x, axis=-1, keepdims=True) + EPS)

    q = l2n(conv_silu(qc_ref, ql_ref, 0)) * (dk ** -0.5)
    k = l2n(conv_silu(kc_ref, kl_ref, 1))
    v = conv_silu(vc_ref, vl_ref, 2)

    ba = jnp.concatenate([bac_ref[...], bal_ref[...]], axis=0)
    beta_all = jax.nn.sigmoid(ba)
    g_all = -lp_ref[0:1, :] * _softplus(ba + lp_ref[1:2, :])
    pf, sf = g_all, g_all
    s = 1
    while s < c:
        pf = pf + jnp.where(rowc >= s, pltpu.roll(pf, s, 0), 0.0)
        sf = sf + jnp.where(rowc < c - s, pltpu.roll(sf, t - s, 0), 0.0)
        s *= 2

    def col(a, idx):
        return jnp.sum(jnp.where(lane == idx, a, 0.0), axis=1, keepdims=True)

    causal_f = (lax.broadcasted_iota(jnp.int32, (c, c), 0) >= lax.broadcasted_iota(jnp.int32, (c, c), 1))[None]
    strict_f = (lax.broadcasted_iota(jnp.int32, (c, c), 0) > lax.broadcasted_iota(jnp.int32, (c, c), 1))[None]
    causal_b = (lax.broadcasted_iota(jnp.int32, (c, c), 0) <= lax.broadcasted_iota(jnp.int32, (c, c), 1))[None]
    strict_b = (lax.broadcasted_iota(jnp.int32, (c, c), 0) < lax.broadcasted_iota(jnp.int32, (c, c), 1))[None]
    eye = (lax.broadcasted_iota(jnp.int32, (c, c), 0) == lax.broadcasted_iota(jnp.int32, (c, c), 1))[None].astype(F32)

    q3 = q.reshape(nch, c, dk)
    k3 = k.reshape(nch, c, dk)
    v3 = v.reshape(nch, c, dk)
    k3b = k3.astype(BF16)
    q3b = q3.astype(BF16)
    hp = lax.Precision.HIGHEST
    for d in range(2):
        causal, strict = (causal_f, strict_f) if d == 0 else (causal_b, strict_b)
        beta = col(beta_all, d * n_heads + h)
        gc = col(pf if d == 0 else sf, (2 + d) * n_heads + h)
        gcb = jnp.broadcast_to(gc, (t, dk))
        g1 = jnp.where(lane == 0, gcb, jnp.where(lane == 1, 1.0, 0.0)).reshape(nch, c, dk)
        g2 = jnp.where(lane == 0, 1.0, jnp.where(lane == 1, -gcb, 0.0)).reshape(nch, c, dk)
        ldiff = jnp.einsum('cid,cjd->cij', g1, g2, precision=hp, preferred_element_type=F32)
        decay = jnp.where(causal, jnp.exp(jnp.where(causal, ldiff, 0.0)), 0.0)
        beta3 = beta.reshape(nch, c, 1)
        gc3 = gc.reshape(nch, c, 1)
        glast3 = gc3[:, c - 1:c, :] if d == 0 else gc3[:, 0:1, :]
        kb = k3 * beta3
        a = jnp.einsum('cid,cjd->cij', kb.astype(BF16), k3b, preferred_element_type=F32)
        a = jnp.where(strict, a * decay, 0.0)
        qk = jnp.einsum('cid,cjd->cij', q3b, k3b, preferred_element_type=F32)
        qk = jnp.where(causal, qk * decay, 0.0)
        x = -a
        pinv = eye + x
        m = 2
        while m < c:
            xb = x.astype(BF16)
            x = jnp.einsum('cij,cjk->cik', xb, xb, preferred_element_type=F32)
            pinv = pinv + jnp.einsum('cij,cjk->cik', pinv.astype(BF16), x.astype(BF16), preferred_element_type=F32)
            m *= 2
        rhs = jnp.concatenate([v3 * beta3, kb * jnp.exp(gc3)], axis=-1)
        sol = jnp.einsum('cij,cjd->cid', pinv.astype(BF16), rhs.astype(BF16), preferred_element_type=F32)
        u_s[d] = sol[:, :, :dk].reshape(t, dk)
        w_s[d] = sol[:, :, dk:].reshape(t, dk).astype(BF16)
        qe_s[d] = (q3 * jnp.exp(gc3)).reshape(t, dk).astype(BF16)
        ke = k3 * jnp.exp(glast3 - gc3)
        ket_s[d] = jnp.swapaxes(ke, 1, 2).astype(BF16)
        qk_s[d] = qk.astype(BF16)
        el_s[d] = jnp.broadcast_to(jnp.exp(glast3), (nch, 8, dk))

    def chunk_step(d, ci, s_mat):
        r0 = pl.multiple_of(ci * c, c)
        sb = s_mat.astype(BF16)
        v_new = u_s[d, pl.ds(r0, c), :] - jnp.dot(w_s[d, pl.ds(r0, c), :], sb, preferred_element_type=F32)
        vb = v_new.astype(BF16)
        o = (jnp.dot(qe_s[d, pl.ds(r0, c), :], sb, preferred_element_type=F32)
             + jnp.dot(qk_s[d, ci], vb, preferred_element_type=F32))
        o_s[d, pl.ds(r0, c), :] = o
        return s_mat * el_s[d, ci][0:1, :] + jnp.dot(ket_s[d, ci], vb, preferred_element_type=F32)

    def ctx_body(n, carry):
        return chunk_step(0, n, carry[0]), chunk_step(1, ncc - 1 - n, carry[1])

    def lat_body(n, carry):
        return chunk_step(0, ncc + n, carry[0]), chunk_step(1, nch - 1 - n, carry[1])

    zero = jnp.zeros((dk, dk), F32)
    carry = lax.fori_loop(0, ncc, ctx_body, (zero, zero))
    lax.fori_loop(0, nch - ncc, lat_body, carry)

    o = o_s[0] + o_s[1]
    on = o * lax.rsqrt(jnp.mean(o * o, axis=-1, keepdims=True) + EPS) * nw_ref[...]
    z = jnp.concatenate([zc_ref[...], zl_ref[...]], axis=0).astype(F32)
    out = (on * _silu(z)).astype(ol_ref.dtype)
    oc_ref[...] = out[:ctx]
    ol_ref[...] = out[ctx:]


def deltanet_mix(p_main, p_ba, conv_w, a_log, dt_bias, norm_w, *, bsz, seq, ctx, u_width):
    n_heads = a_log.shape[-1]
    dk = norm_w.shape[-1]
    c = DN_CHUNK
    t = seq + ctx
    nch = t // c
    assert seq % c == 0 and ctx % c == 0 and u_width % dk == 0 and 4 * n_heads <= p_ba.shape[1]
    cb = u_width // dk
    lat_rows = bsz * seq
    cw = conv_w.astype(F32).reshape(conv_w.shape[0], 3, n_heads, dk).transpose(2, 1, 0, 3)
    lanes = p_ba.shape[1]
    lp = jnp.zeros((2, lanes), F32)
    lp = lp.at[0, 2 * n_heads:4 * n_heads].set(jnp.exp(a_log.astype(F32)).reshape(-1))
    lp = lp.at[1, 2 * n_heads:4 * n_heads].set(dt_bias.astype(F32).reshape(-1))
    cblk = lat_rows // ctx

    def lat_spec(off):
        return pl.BlockSpec((seq, dk), lambda b, h: (b, off + h))

    def ctx_spec(off):
        return pl.BlockSpec((ctx, dk), lambda b, h: (cblk + b, off + h))

    yl, yc = pl.pallas_call(
        functools.partial(_dn_kernel, n_heads=n_heads, ctx=ctx),
        out_shape=(jax.ShapeDtypeStruct((lat_rows, n_heads * dk), BF16),
                   jax.ShapeDtypeStruct((bsz * ctx, n_heads * dk), BF16)),
        grid=(bsz, n_heads),
        in_specs=[
            lat_spec(cb), lat_spec(cb + n_heads), lat_spec(cb + 2 * n_heads),
            ctx_spec(cb), ctx_spec(cb + n_heads), ctx_spec(cb + 2 * n_heads),
            pl.BlockSpec((seq, lanes), lambda b, h: (b, 0)),
            pl.BlockSpec((ctx, lanes), lambda b, h: (cblk + b, 0)),
            lat_spec(cb + 3 * n_heads), ctx_spec(cb + 3 * n_heads),
            pl.BlockSpec((1, 3, conv_w.shape[0], dk), lambda b, h: (h, 0, 0, 0)),
            pl.BlockSpec((2, lanes), lambda b, h: (0, 0)),
            pl.BlockSpec((1, dk), lambda b, h: (0, 0)),
        ],
        out_specs=(pl.BlockSpec((seq, dk), lambda b, h: (b, h)),
                   pl.BlockSpec((ctx, dk), lambda b, h: (b, h))),
        scratch_shapes=[
            pltpu.VMEM((2, t, dk), F32),
            pltpu.VMEM((2, t, dk), BF16),
            pltpu.VMEM((2, t, dk), BF16),
            pltpu.VMEM((2, nch, dk, c), BF16),
            pltpu.VMEM((2, nch, c, c), BF16),
            pltpu.VMEM((2, nch, 8, dk), F32),
            pltpu.VMEM((2, t, dk), F32),
        ],
        compiler_params=_cparams(2),
        name="deltanet",
    )(p_main, p_main, p_main, p_main, p_main, p_main, p_ba, p_ba, p_main, p_main,
      cw, lp, norm_w.astype(F32).reshape(1, dk))
    return jnp.concatenate([yl, yc], axis=0)


def _experts_kernel(be_ref, nv_ref, x_ref, wg_ref, wu_ref, wd_ref, o_ref, wgb, wub, wdb):
    i = pl.program_id(0)
    prev = be_ref[jnp.maximum(i - 1, 0)]

    @pl.when((i == 0) | (be_ref[i] != prev))
    def _():
        wgb[...] = wg_ref[0].astype(BF16)
        wub[...] = wu_ref[0].astype(BF16)
        wdb[...] = wd_ref[0].astype(BF16)

    @pl.when(i < nv_ref[0])
    def _():
        x = x_ref[...]
        g = jnp.dot(x, wgb[...], preferred_element_type=F32)
        u = jnp.dot(x, wub[...], preferred_element_type=F32)
        a = (_silu(g) * u).astype(BF16)
        o_ref[...] = jnp.dot(a, wdb[...], preferred_element_type=F32).astype(o_ref.dtype)

    @pl.when(i >= nv_ref[0])
    def _():
        o_ref[...] = jnp.zeros_like(o_ref)


def routed_experts(xg, block_e, n_valid, w_gate, w_up, w_down):
    rows, d = xg.shape
    f = w_gate.shape[-1]
    mb = MOE_BLOCK
    w_gate, w_up = w_gate.reshape(-1, d, f), w_up.reshape(-1, d, f)
    w_down = w_down.reshape(-1, f, d)
    return pl.pallas_call(
        _experts_kernel,
        out_shape=jax.ShapeDtypeStruct((rows, d), BF16),
        grid_spec=pltpu.PrefetchScalarGridSpec(
            num_scalar_prefetch=2,
            grid=(rows // mb,),
            in_specs=[
                pl.BlockSpec((mb, d), lambda i, be, nv: (i, 0)),
                pl.BlockSpec((1, d, f), lambda i, be, nv: (be[i], 0, 0)),
                pl.BlockSpec((1, d, f), lambda i, be, nv: (be[i], 0, 0)),
                pl.BlockSpec((1, f, d), lambda i, be, nv: (be[i], 0, 0)),
            ],
            out_specs=pl.BlockSpec((mb, d), lambda i, be, nv: (i, 0)),
            scratch_shapes=[pltpu.VMEM((d, f), BF16), pltpu.VMEM((d, f), BF16), pltpu.VMEM((f, d), BF16)],
        ),
        compiler_params=_cparams(1),
        name="routed_experts",
    )(block_e, n_valid, xg, w_gate, w_up, w_down)


def _swiglu_kernel(x_ref, wg_ref, wu_ref, wd_ref, o_ref, wgb, wub, wdb):
    @pl.when(pl.program_id(0) == 0)
    def _():
        wgb[...] = wg_ref[...].astype(BF16)
        wub[...] = wu_ref[...].astype(BF16)
        wdb[...] = wd_ref[...].astype(BF16)

    x = x_ref[...]
    g = jnp.dot(x, wgb[...], preferred_element_type=F32)
    u = jnp.dot(x, wub[...], preferred_element_type=F32)
    a = (_silu(g) * u).astype(BF16)
    o_ref[...] = jnp.dot(a, wdb[...], preferred_element_type=F32).astype(o_ref.dtype)


def shared_expert(x, wg, wu, wd, *, layer, rows, tm=None):
    d = x.shape[1]
    f = wg.shape[-1]
    tm = _row_tile(rows, tm)
    const = lambda i: (0, 0)
    return pl.pallas_call(
        _swiglu_kernel,
        out_shape=jax.ShapeDtypeStruct((rows, d), F32),
        grid=(rows // tm,),
        in_specs=[
            pl.BlockSpec((tm, d), lambda i: (i, 0)),
            _wspec(wg, layer, (d, f), const, pipeline_mode=pl.Buffered(1)),
            _wspec(wu, layer, (d, f), const, pipeline_mode=pl.Buffered(1)),
            _wspec(wd, layer, (f, d), const, pipeline_mode=pl.Buffered(1)),
        ],
        out_specs=pl.BlockSpec((tm, d), lambda i: (i, 0)),
        scratch_shapes=[pltpu.VMEM((d, f), BF16), pltpu.VMEM((d, f), BF16), pltpu.VMEM((f, d), BF16)],
        compiler_params=_cparams(1),
        name="shared_expert",
    )(x, wg, wu, wd)


def _ffn_res_kernel(r_ref, s_ref, x_ref, nw_ref, g_ref, o_ref):
    f = r_ref[...] + s_ref[...]
    fn = f * lax.rsqrt(jnp.mean(f * f, axis=-1, keepdims=True) + EPS) * nw_ref[...]
    o_ref[...] = x_ref[...] + g_ref[0] * fn


def ffn_residual(routed, shared, x, nw, gate, set_of_tile, *, rows, tm=512):
    d = x.shape[1]
    tm = _row_tile(rows, tm)
    return pl.pallas_call(
        _ffn_res_kernel,
        out_shape=jax.ShapeDtypeStruct((rows, d), F32),
        grid=(rows // tm,),
        in_specs=[
            pl.BlockSpec((tm, d), lambda i: (i, 0)),
            pl.BlockSpec((tm, d), lambda i: (i, 0)),
            pl.BlockSpec((tm, d), lambda i: (i, 0)),
            pl.BlockSpec((1, d), lambda i: (0, 0)),
            pl.BlockSpec((1, 1, d), lambda i: (set_of_tile(i, tm), 0, 0)),
        ],
        out_specs=pl.BlockSpec((tm, d), lambda i: (i, 0)),
        compiler_params=_cparams(1),
        name="ffn_residual",
    )(routed, shared, x, nw.reshape(1, d), gate)


def moe_ffn(h, w_router, e_bias, w_gate, w_up, w_down, ws_gate, ws_up, ws_down, *, layer, rows):
    d = h.shape[1]
    n_exp = w_router.shape[1]
    lanes = 128
    wr = jnp.pad(w_router, ((0, 0), (0, lanes - n_exp)))
    logits = matmul(h, wr, rows=rows, tn=lanes, name="router")[:, :n_exp]
    scores = jax.nn.sigmoid(logits)
    sel = (scores + e_bias.astype(F32)).reshape(rows, N_GROUPS, n_exp // N_GROUPS)
    group_score = lax.top_k(sel, 2)[0].sum(-1)
    _, gidx = lax.top_k(group_score, TOPK_GROUPS)
    gmask = jnp.any(gidx[:, :, None] == jnp.arange(N_GROUPS)[None, None, :], axis=1)
    sel = jnp.where(gmask[:, :, None], sel, -jnp.inf).reshape(rows, n_exp)
    _, eidx = lax.top_k(sel, TOP_K)
    wts = jnp.take_along_axis(scores, eidx, axis=-1)
    wts = wts / jnp.sum(wts, axis=-1, keepdims=True) * ROUTE_SCALE
    mb = MOE_BLOCK
    onehot = (eidx[:, :, None] == jnp.arange(n_exp)[None, None, :])
    mask = jnp.any(onehot, axis=1).astype(jnp.int32)
    counts = mask.sum(0)
    padded = (counts + mb - 1) // mb * mb
    pad_end = jnp.cumsum(padded)
    pad_start = pad_end - padded
    rank = jnp.cumsum(mask, axis=0) - mask
    slot = pad_start[None, :] + rank
    dest = jnp.take_along_axis(slot, eidx, axis=1)
    n_blocks = (rows * TOP_K + n_exp * (mb - 1) + mb - 1) // mb
    tok = jnp.zeros((n_blocks * mb,), jnp.int32).at[dest.reshape(-1)].set(
        jnp.repeat(jnp.arange(rows, dtype=jnp.int32), TOP_K))
    n_valid = (pad_end[-1] // mb).astype(jnp.int32)
    starts = jnp.arange(n_blocks, dtype=jnp.int32) * mb
    block_e = jnp.minimum(jnp.searchsorted(pad_end, starts, side='right'), n_exp - 1).astype(jnp.int32)
    last_e = block_e[jnp.maximum(n_valid - 1, 0)]
    block_e = jnp.where(jnp.arange(n_blocks) < n_valid, block_e, last_e)
    y = routed_experts(h[tok], block_e + layer * n_exp, n_valid.reshape(1), w_gate, w_up, w_down)
    routed = jnp.einsum('tkd,tk->td', y[dest].astype(F32), wts)
    shared = shared_expert(h, ws_gate, ws_up, ws_down, layer=layer, rows=rows)
    return routed, shared


def kernel(x, c, ctx, c_ctx, w_mod, b_mod, norm_mix_pre, norm_mix_post, norm_ffn_pre, norm_ffn_post,
           w_in, s5_lam_re, s5_lam_im, s5_log_step, s5_b_re, s5_b_im, s5_c_re, s5_c_im, s5_d, s5_w_glu,
           dn_conv, dn_a_log, dn_dt_bias, dn_norm, w_br_s5, w_br_dn, w_out,
           moe_router, moe_bias, moe_w_gate, moe_w_up, moe_w_down, sh_w_gate, sh_w_up, sh_w_down):
    bsz, seq, d = x.shape
    n_ctx = ctx.shape[1]
    depth = w_mod.shape[0]
    lat_rows, ctx_rows = bsz * seq, bsz * n_ctx
    all_rows = lat_rows + ctx_rows
    s5_width = s5_d.shape[1]
    dn_width = w_br_dn.shape[1]
    n_heads = dn_a_log.shape[-1]
    main_cols = s5_width + 4 * dn_width
    ba_cols = 4 * n_heads
    lanes = 128
    nc = (seq + n_ctx) // S5_CHUNK
    n_levels = max(1, (nc - 1).bit_length())

    def set_of_tile(i, tm):
        return jnp.minimum((i * tm) // seq, bsz)

    xs = jnp.concatenate([x.reshape(lat_rows, d), ctx.reshape(ctx_rows, d)], axis=0)
    n_sets = 8
    cin = jnp.zeros((n_sets, d), F32).at[:bsz].set(_silu(c)).at[bsz].set(_silu(c_ctx))
    for i in range(depth):
        last = i == depth - 1
        rows = lat_rows if last else all_rows
        mods = matmul(cin, w_mod, b_mod[i].reshape(1, -1), layer=i, tm=n_sets, name="mods").reshape(n_sets, 6, 1, d)
        mod = [mods[:, k] for k in range(6)]
        hmix = prenorm(xs, norm_mix_pre[i], mod[0], mod[1], set_of_tile)
        p_main = matmul(hmix, w_in, layer=i, n_cols=main_cols, out_dtype=BF16, name="in_proj")
        w_tail = w_in[i, :, main_cols:]
        p_ba = matmul(hmix, jnp.pad(w_tail[:, :ba_cols], ((0, 0), (0, lanes - ba_cols))), tn=lanes, name="in_proj_ba")
        gates = matmul(hmix, w_tail[:, ba_cols:], rows=rows, out_dtype=BF16, name="in_proj_gates")
        ops = s5_operators(s5_lam_re[i], s5_lam_im[i], s5_log_step[i], s5_b_re[i], s5_b_im[i],
                           s5_c_re[i], s5_c_im[i], n_levels)
        y_s5 = s5_mix(p_main[:, :s5_width], ops, s5_d[i], bsz=bsz, seq=seq, ctx=n_ctx)
        y_s5 = s5_glu(y_s5, s5_w_glu, layer=i, rows=rows)
        y_dn = deltanet_mix(p_main, p_ba, dn_conv[i], dn_a_log[i], dn_dt_bias[i], dn_norm[i],
                            bsz=bsz, seq=seq, ctx=n_ctx, u_width=s5_width)
        m = branch_merge(y_s5, y_dn, gates, w_br_s5, w_br_dn, layer=i, rows=rows)
        xs = outproj_residual(m, w_out, xs, norm_mix_post[i], mod[2], set_of_tile, layer=i, rows=rows)
        hffn = prenorm(xs, norm_ffn_pre[i], mod[3], mod[4], set_of_tile, rows=rows)
        routed, shared = moe_ffn(hffn, moe_router[i], moe_bias[i], moe_w_gate, moe_w_up, moe_w_down,
                                 sh_w_gate, sh_w_up, sh_w_down, layer=i, rows=rows)
        xs = ffn_residual(routed, shared, xs, norm_ffn_post[i], mod[5], set_of_tile, rows=rows)
    return xs[:lat_rows].reshape(bsz, seq, d)
```

```python
import functools

import jax
import jax.numpy as jnp
from jax import lax
from jax.experimental import pallas as pl
from jax.experimental.pallas import tpu as pltpu

F32 = jnp.float32
BF16 = jnp.bfloat16

EPS = 1e-6
GRID_W = 64
S5_CHUNK = 16
DN_CHUNK = 64
DN_SUB = 16
TOP_K = 8
N_GROUPS = 8
TOPK_GROUPS = 4
ROUTE_SCALE = 2.5
MOE_BLOCK = 256

V7X_VMEM_LIMIT = 56 * 1024 * 1024
LANES = 128
ROW_TILE = 1024


def _cparams(n_axes, vmem=V7X_VMEM_LIMIT):
    return pltpu.CompilerParams(dimension_semantics=("arbitrary",) * n_axes, vmem_limit_bytes=vmem)


def _silu(x):
    return x * jax.nn.sigmoid(x)


def _row_tile(rows, tm=None):
    tm = min(tm or ROW_TILE, ROW_TILE, rows)
    assert rows % tm == 0, (rows, tm)
    return tm


def _col_tile(n, tn, col0=0):
    tn = min(tn, n)
    while n % tn or col0 % tn:
        tn -= LANES
    return tn


def _wspec(w, layer, block, index_map, **kw):
    if w.ndim == len(block):
        return pl.BlockSpec(block, index_map, **kw)
    return pl.BlockSpec((None,) + tuple(block), lambda *a: (layer,) + tuple(index_map(*a)), **kw)


def _mm_kernel(x_ref, w_ref, b_ref, o_ref, wbf_ref):
    @pl.when(pl.program_id(1) == 0)
    def _():
        wbf_ref[...] = w_ref[...].astype(BF16)

    acc = jnp.dot(x_ref[...].astype(BF16), wbf_ref[...], preferred_element_type=F32)
    o_ref[...] = (acc + b_ref[...]).astype(o_ref.dtype)


def matmul(x, w, bias=None, *, layer=0, n_cols=None, col0=0, out_dtype=F32, tm=None, tn=1024, rows=None,
           name="matmul"):
    m, k = x.shape
    rows = m if rows is None else rows
    n_cols = w.shape[-1] - col0 if n_cols is None else n_cols
    tm = _row_tile(rows, tm)
    tn = _col_tile(n_cols, tn, col0)
    assert rows % tm == 0 and n_cols % tn == 0 and col0 % tn == 0, (rows, tm, n_cols, tn, col0)
    if bias is None:
        bias = jnp.zeros((1, n_cols), F32)
    cb0 = col0 // tn
    return pl.pallas_call(
        _mm_kernel,
        out_shape=jax.ShapeDtypeStruct((rows, n_cols), out_dtype),
        grid=(n_cols // tn, rows // tm),
        in_specs=[
            pl.BlockSpec((tm, k), lambda j, i: (i, 0)),
            _wspec(w, layer, (k, tn), lambda j, i: (0, j + cb0)),
            pl.BlockSpec((1, tn), lambda j, i: (0, j)),
        ],
        out_specs=pl.BlockSpec((tm, tn), lambda j, i: (i, j)),
        scratch_shapes=[pltpu.VMEM((k, tn), BF16)],
        compiler_params=_cparams(2),
        name=name,
    )(x, w, bias)


def _prenorm_kernel(x_ref, w_ref, sh_ref, sc_ref, o_ref):
    x = x_ref[...]
    y = x * lax.rsqrt(jnp.mean(x * x, axis=-1, keepdims=True) + EPS) * w_ref[...]
    o_ref[...] = (y * (1.0 + sc_ref[0]) + sh_ref[0]).astype(o_ref.dtype)


def prenorm(x, w, shift, scale, set_of_tile, *, rows=None, tm=None):
    m, d = x.shape
    rows = m if rows is None else rows
    tm = _row_tile(rows, tm)
    return pl.pallas_call(
        _prenorm_kernel,
        out_shape=jax.ShapeDtypeStruct((rows, d), BF16),
        grid=(rows // tm,),
        in_specs=[
            pl.BlockSpec((tm, d), lambda i: (i, 0)),
            pl.BlockSpec((1, d), lambda i: (0, 0)),
            pl.BlockSpec((1, 1, d), lambda i: (set_of_tile(i, tm), 0, 0)),
            pl.BlockSpec((1, 1, d), lambda i: (set_of_tile(i, tm), 0, 0)),
        ],
        out_specs=pl.BlockSpec((tm, d), lambda i: (i, 0)),
        compiler_params=_cparams(1),
        name="prenorm",
    )(x, w.reshape(1, d), shift, scale)


def _glu_kernel(y_ref, w_ref, o_ref, wbf_ref, *, tn):
    @pl.when(pl.program_id(1) == 0)
    def _():
        wbf_ref[...] = w_ref[...].astype(BF16)

    j = pl.program_id(0)
    g = jax.nn.gelu(y_ref[...].astype(F32))
    acc = jnp.dot(g.astype(BF16), wbf_ref[...], preferred_element_type=F32)
    gj = jax.nn.gelu(y_ref[:, pl.ds(pl.multiple_of(j * tn, tn), tn)].astype(F32))
    o_ref[...] = (gj * jax.nn.sigmoid(acc)).astype(o_ref.dtype)


def s5_glu(y, w, *, layer, rows, tm=None, tn=512):
    m, k = y.shape
    tm = _row_tile(rows, tm)
    tn = _col_tile(k, tn)
    return pl.pallas_call(
        functools.partial(_glu_kernel, tn=tn),
        out_shape=jax.ShapeDtypeStruct((rows, k), BF16),
        grid=(k // tn, rows // tm),
        in_specs=[
            pl.BlockSpec((tm, k), lambda j, i: (i, 0)),
            _wspec(w, layer, (k, tn), lambda j, i: (0, j)),
        ],
        out_specs=pl.BlockSpec((tm, tn), lambda j, i: (i, j)),
        scratch_shapes=[pltpu.VMEM((k, tn), BF16)],
        compiler_params=_cparams(2),
        name="s5_glu",
    )(y, w)


def _merge_kernel(a_ref, b_ref, ga_ref, gb_ref, wa_ref, wb_ref, o_ref, wabf_ref, wbbf_ref):
    @pl.when(pl.program_id(1) == 0)
    def _():
        wabf_ref[...] = wa_ref[...].astype(BF16)
        wbbf_ref[...] = wb_ref[...].astype(BF16)

    ya = jnp.dot(a_ref[...], wabf_ref[...], preferred_element_type=F32)
    yb = jnp.dot(b_ref[...], wbbf_ref[...], preferred_element_type=F32)
    m = jax.nn.sigmoid(ga_ref[...].astype(F32)) * ya + jax.nn.sigmoid(gb_ref[...].astype(F32)) * yb
    o_ref[...] = m.astype(o_ref.dtype)


def branch_merge(ya, yb, gates, wa, wb, *, layer, rows, tm=None, tn=512):
    ka, kb = ya.shape[1], yb.shape[1]
    d = wa.shape[-1]
    tm = _row_tile(rows, tm)
    tn = _col_tile(d, tn)
    nb = d // tn
    return pl.pallas_call(
        _merge_kernel,
        out_shape=jax.ShapeDtypeStruct((rows, d), BF16),
        grid=(nb, rows // tm),
        in_specs=[
            pl.BlockSpec((tm, ka), lambda j, i: (i, 0)),
            pl.BlockSpec((tm, kb), lambda j, i: (i, 0)),
            pl.BlockSpec((tm, tn), lambda j, i: (i, j)),
            pl.BlockSpec((tm, tn), lambda j, i: (i, j + nb)),
            _wspec(wa, layer, (ka, tn), lambda j, i: (0, j)),
            _wspec(wb, layer, (kb, tn), lambda j, i: (0, j)),
        ],
        out_specs=pl.BlockSpec((tm, tn), lambda j, i: (i, j)),
        scratch_shapes=[pltpu.VMEM((ka, tn), BF16), pltpu.VMEM((kb, tn), BF16)],
        compiler_params=_cparams(2),
        name="branch_merge",
    )(ya, yb, gates, gates, wa, wb)


def _outproj_kernel(m_ref, w_ref, x_ref, nw_ref, g_ref, o_ref, wbf_ref):
    @pl.when(pl.program_id(0) == 0)
    def _():
        wbf_ref[...] = w_ref[...].astype(BF16)

    y = jnp.dot(m_ref[...], wbf_ref[...], preferred_element_type=F32)
    yn = y * lax.rsqrt(jnp.mean(y * y, axis=-1, keepdims=True) + EPS) * nw_ref[...]
    o_ref[...] = x_ref[...] + g_ref[0] * yn


def outproj_residual(m, w, x, nw, gate, set_of_tile, *, layer, rows, tm=512):
    d = w.shape[-1]
    k = w.shape[-2]
    tm = _row_tile(rows, tm)
    return pl.pallas_call(
        _outproj_kernel,
        out_shape=jax.ShapeDtypeStruct((rows, d), F32),
        grid=(rows // tm,),
        in_specs=[
            pl.BlockSpec((tm, k), lambda i: (i, 0)),
            _wspec(w, layer, (k, d), lambda i: (0, 0), pipeline_mode=pl.Buffered(1)),
            pl.BlockSpec((tm, d), lambda i: (i, 0)),
            pl.BlockSpec((1, d), lambda i: (0, 0)),
            pl.BlockSpec((1, 1, d), lambda i: (set_of_tile(i, tm), 0, 0)),
        ],
        out_specs=pl.BlockSpec((tm, d), lambda i: (i, 0)),
        scratch_shapes=[pltpu.VMEM((k, d), BF16)],
        compiler_params=_cparams(1),
        name="outproj_residual",
    )(m, w, x, nw.reshape(1, d), gate)


def s5_operators(lam_re, lam_im, log_step, b_re, b_im, c_re, c_im, n_levels):
    tc = S5_CHUNK
    hp = lax.Precision.HIGHEST
    lr, li = lam_re.astype(F32), lam_im.astype(F32)
    step = jnp.exp(log_step.astype(F32))[..., None]

    def apow(l):
        mag = jnp.exp(lr * step * l)
        return mag * jnp.cos(li * step * l), mag * jnp.sin(li * step * l)

    ab_re, ab_im = apow(1.0)
    den = lr * lr + li * li
    nr = ab_re - 1.0
    cr = (nr * lr + ab_im * li) / den
    ci = (ab_im * lr - nr * li) / den
    br, bi = b_re.astype(F32), b_im.astype(F32)
    bb_re = cr[..., None] * br - ci[..., None] * bi
    bb_im = cr[..., None] * bi + ci[..., None] * br
    lags = jnp.arange(tc + 1, dtype=F32)[:, None, None, None]
    pw_re, pw_im = apow(lags)
    pw_re, pw_im = jnp.moveaxis(pw_re, 0, 2), jnp.moveaxis(pw_im, 0, 2)
    cre, cim = c_re.astype(F32), c_im.astype(F32)
    cp_re = cre[:, :, None] * pw_re[:, :, :, None] - cim[:, :, None] * pw_im[:, :, :, None]
    cp_im = cre[:, :, None] * pw_im[:, :, :, None] + cim[:, :, None] * pw_re[:, :, :, None]
    kl = (jnp.einsum('dgljp,dgpi->dglji', cp_re[:, :, :tc], bb_re, precision=hp)
          - jnp.einsum('dgljp,dgpi->dglji', cp_im[:, :, :tc], bb_im, precision=hp))
    s_idx = jnp.arange(tc)[:, None]
    t_idx = jnp.arange(tc)[None, :]
    lag = t_idx - s_idx
    tmat = jnp.where((lag >= 0)[None, None, :, :, None, None], kl[:, :, jnp.clip(lag, 0, tc - 1)], 0.0)
    tmat = tmat.transpose(0, 1, 2, 5, 3, 4)
    pr, pi = pw_re[:, :, tc - 1 - jnp.arange(tc)], pw_im[:, :, tc - 1 - jnp.arange(tc)]
    win_re = pr[:, :, :, None, :] * bb_re.transpose(0, 1, 3, 2)[:, :, None] - pi[:, :, :, None, :] * bb_im.transpose(0, 1, 3, 2)[:, :, None]
    win_im = pr[:, :, :, None, :] * bb_im.transpose(0, 1, 3, 2)[:, :, None] + pi[:, :, :, None, :] * bb_re.transpose(0, 1, 3, 2)[:, :, None]
    win = jnp.concatenate([win_re, win_im], axis=-1)
    wo_re = cp_re[:, :, 1:].transpose(0, 1, 4, 2, 3)
    wo_im = -cp_im[:, :, 1:].transpose(0, 1, 4, 2, 3)
    wout = jnp.concatenate([wo_re, wo_im], axis=2)
    tmat = jnp.stack([tmat[0], tmat[1, :, ::-1, :, ::-1, :]])
    win = jnp.stack([win[0], win[1, :, ::-1]])
    wout = jnp.stack([wout[0], wout[1, :, :, ::-1]])
    g = lr.shape[1]
    j = br.shape[-1]
    p = lr.shape[-1]
    rows = []
    for k in range(n_levels):
        ar, ai = apow(float(tc * 2 ** k))
        rows.append(jnp.concatenate([ar, ar], axis=-1))
        rows.append(jnp.concatenate([-ai, ai], axis=-1))
    apw = jnp.stack(rows, axis=2)
    return (tmat.reshape(2, g, tc * j, tc * j).transpose(1, 0, 2, 3).astype(BF16),
            win.reshape(2, g, tc * j, 2 * p).transpose(1, 0, 2, 3).astype(BF16),
            wout.reshape(2, g, 2 * p, tc * j).transpose(1, 0, 2, 3).astype(BF16),
            apw.transpose(1, 0, 2, 3))


def _s5_kernel(u_ref, tm_ref, win_ref, wout_ref, apw_ref, dsk_ref, y_ref, *, nb, ctx_rows, n_levels, p):
    u = u_ref[0]
    n = u.shape[0]
    row = lax.broadcasted_iota(jnp.int32, (n, 1), 0)
    y = u.astype(F32) * dsk_ref[0]
    for d in range(2):
        y = y + jnp.dot(u, tm_ref[0, d], preferred_element_type=F32)
        x = jnp.dot(u, win_ref[0, d], preferred_element_type=F32)
        if d == 0:
            def shift(a, s):
                return jnp.where(row >= s, pltpu.roll(a, s, 0), 0.0)
        else:
            if ctx_rows:
                x = pltpu.roll(x, n - ctx_rows, 0)

            def shift(a, s):
                return jnp.where(row < n - s, pltpu.roll(a, n - s, 0), 0.0)
        x = shift(x, nb)
        for k in range(n_levels):
            sh = shift(x, nb * 2 ** k)
            a1 = apw_ref[0, d, 2 * k:2 * k + 1, :]
            a2 = apw_ref[0, d, 2 * k + 1:2 * k + 2, :]
            x = x + a1 * sh + a2 * pltpu.roll(sh, p, 1)
        if d == 1 and ctx_rows:
            x = pltpu.roll(x, ctx_rows, 0)
        y = y + jnp.dot(x.astype(BF16), wout_ref[0, d], preferred_element_type=F32)
    y_ref[0] = y.astype(y_ref.dtype)


def s5_scan(uc, tmat, win, wout, apw, dsk, *, nb, ctx_rows, n_levels):
    g, n, lanes = uc.shape
    p2 = win.shape[-1]
    return pl.pallas_call(
        functools.partial(_s5_kernel, nb=nb, ctx_rows=ctx_rows, n_levels=n_levels, p=p2 // 2),
        out_shape=jax.ShapeDtypeStruct((g, n, lanes), BF16),
        grid=(g,),
        in_specs=[
            pl.BlockSpec((1, n, lanes), lambda i: (i, 0, 0)),
            pl.BlockSpec((1, 2, lanes, lanes), lambda i: (i, 0, 0, 0)),
            pl.BlockSpec((1, 2, lanes, p2), lambda i: (i, 0, 0, 0)),
            pl.BlockSpec((1, 2, p2, lanes), lambda i: (i, 0, 0, 0)),
            pl.BlockSpec((1, 2, 2 * n_levels, p2), lambda i: (i, 0, 0, 0)),
            pl.BlockSpec((1, 1, lanes), lambda i: (i, 0, 0)),
        ],
        out_specs=pl.BlockSpec((1, n, lanes), lambda i: (i, 0, 0)),
        compiler_params=_cparams(1),
        name="s5_scan",
    )(uc, tmat, win, wout, apw, dsk)


def s5_mix(u_rows, ops, d_skip, *, bsz, seq, ctx):
    tmat, win, wout, apw = ops
    g = tmat.shape[0]
    lanes = tmat.shape[-1]
    tc = S5_CHUNK
    j = lanes // tc
    hgt = seq // GRID_W
    ul = u_rows[:bsz * seq].reshape(bsz, hgt, GRID_W, g, j).transpose(0, 2, 1, 3, 4).reshape(bsz, seq, g, j)
    uc = u_rows[bsz * seq:].reshape(bsz, ctx, g, j)
    useq = jnp.concatenate([uc, ul], axis=1)
    nc = (seq + ctx) // tc
    uch = useq.reshape(bsz, nc, tc, g, j).transpose(3, 1, 0, 2, 4).reshape(g, nc * bsz, lanes)
    n_levels = apw.shape[2] // 2
    dsk = jnp.tile(d_skip.astype(F32).reshape(g, 1, j), (1, tc, 1)).reshape(g, 1, lanes)
    ych = s5_scan(uch.astype(BF16), tmat, win, wout, apw, dsk, nb=bsz, ctx_rows=(ctx // tc) * bsz, n_levels=n_levels)
    yseq = ych.reshape(g, nc, bsz, tc, j).transpose(2, 1, 3, 0, 4).reshape(bsz, seq + ctx, g * j)
    yc = yseq[:, :ctx].reshape(bsz * ctx, g * j)
    yl = yseq[:, ctx:].reshape(bsz, GRID_W, hgt, g * j).transpose(0, 2, 1, 3).reshape(bsz * seq, g * j)
    return jnp.concatenate([yl, yc], axis=0)


def _softplus(x):
    return jnp.maximum(x, 0.0) + jnp.log1p(jnp.exp(-jnp.abs(x)))


def _dn_kernel(ql_ref, kl_ref, vl_ref, qc_ref, kc_ref, vc_ref, bal_ref, bac_ref, zl_ref, zc_ref,
               cw_ref, lp_ref, nw_ref, ol_ref, oc_ref,
               nt_s, w2t_s, qp_s, el_s, o_s, *, n_heads, ctx):
    c = DN_CHUNK
    seq = ql_ref.shape[0]
    t = seq + ctx
    dk = ql_ref.shape[1]
    nch = t // c
    ncc = ctx // c
    h = pl.program_id(1)
    row = lax.broadcasted_iota(jnp.int32, (t, 1), 0)
    rowc = row % c
    lane = lax.broadcasted_iota(jnp.int32, (1, dk), 1)

    first = (row == 0) | (row == ctx)
    last = (row == ctx - 1) | (row == t - 1)

    def conv_silu(xc_ref, xl_ref, kind):
        x = jnp.concatenate([xc_ref[...], xl_ref[...]], axis=0).astype(F32)
        xp = jnp.where(first, 0.0, pltpu.roll(x, 1, 0))
        xn = jnp.where(last, 0.0, pltpu.roll(x, t - 1, 0))
        w = cw_ref[0, kind]
        return _silu(xp * w[0:1] + x * w[1:2] + xn * w[2:3])

    def l2n(x):
        return x * lax.rsqrt(jnp.sum(x * x, axis=-1, keepdims=True) + EPS)

    q = l2n(conv_silu(qc_ref, ql_ref, 0)) * (dk ** -0.5)
    k = l2n(conv_silu(kc_ref, kl_ref, 1))
    v = conv_silu(vc_ref, vl_ref, 2)

    ba = jnp.concatenate([bac_ref[...], bal_ref[...]], axis=0)
    beta_all = jax.nn.sigmoid(ba)
    g_all = -lp_ref[0:1, :] * _softplus(ba + lp_ref[1:2, :])
    pf, sf = g_all, g_all
    s = 1
    while s < c:
        pf = pf + jnp.where(rowc >= s, pltpu.roll(pf, s, 0), 0.0)
        sf = sf + jnp.where(rowc < c - s, pltpu.roll(sf, t - s, 0), 0.0)
        s *= 2

    def col(a, idx):
        return jnp.sum(jnp.where(lane == idx, a, 0.0), axis=1, keepdims=True)

    causal_f = (lax.broadcasted_iota(jnp.int32, (c, c), 0) >= lax.broadcasted_iota(jnp.int32, (c, c), 1))[None]
    strict_f = (lax.broadcasted_iota(jnp.int32, (c, c), 0) > lax.broadcasted_iota(jnp.int32, (c, c), 1))[None]
    causal_b = (lax.broadcasted_iota(jnp.int32, (c, c), 0) <= lax.broadcasted_iota(jnp.int32, (c, c), 1))[None]
    strict_b = (lax.broadcasted_iota(jnp.int32, (c, c), 0) < lax.broadcasted_iota(jnp.int32, (c, c), 1))[None]
    eye = (lax.broadcasted_iota(jnp.int32, (c, c), 0) == lax.broadcasted_iota(jnp.int32, (c, c), 1))[None].astype(F32)

    sub_blk = (lax.broadcasted_iota(jnp.int32, (c, c), 0) // DN_SUB
               == lax.broadcasted_iota(jnp.int32, (c, c), 1) // DN_SUB)[None]

    def neg_pow_inverse(x, m, limit):
        pinv = eye + x
        while m < limit:
            xb = x.astype(BF16)
            x = jnp.einsum('cij,cjk->cik', xb, xb, preferred_element_type=F32)
            pinv = pinv + jnp.einsum('cij,cjk->cik', pinv.astype(BF16), x.astype(BF16), preferred_element_type=F32)
            m *= 2
        return pinv

    q3 = q.reshape(nch, c, dk)
    k3 = k.reshape(nch, c, dk)
    v3 = v.reshape(nch, c, dk)
    k3b = k3.astype(BF16)
    q3b = q3.astype(BF16)
    for d in range(2):
        causal, strict = (causal_f, strict_f) if d == 0 else (causal_b, strict_b)
        beta = col(beta_all, d * n_heads + h)
        gc = col(pf if d == 0 else sf, (2 + d) * n_heads + h)
        hi = gc.astype(BF16).astype(F32)
        mid = (gc - hi).astype(BF16).astype(F32)
        lo = gc - hi - mid
        pieces = (hi, mid, lo)
        g1 = jnp.zeros((t, dk), F32)
        g2 = jnp.zeros((t, dk), F32)
        for n_p, piece in enumerate(pieces):
            pb = jnp.broadcast_to(piece, (t, dk))
            g1 = jnp.where(lane == n_p, pb, jnp.where(lane == 3 + n_p, 1.0, g1))
            g2 = jnp.where(lane == n_p, 1.0, jnp.where(lane == 3 + n_p, -pb, g2))
        ldiff = jnp.einsum('cid,cjd->cij', g1.astype(BF16).reshape(nch, c, dk), g2.astype(BF16).reshape(nch, c, dk),
                           preferred_element_type=F32)
        decay = jnp.where(causal, jnp.exp(jnp.where(causal, ldiff, 0.0)), 0.0)
        beta3 = beta.reshape(nch, c, 1)
        gc3 = gc.reshape(nch, c, 1)
        glast3 = gc3[:, c - 1:c, :] if d == 0 else gc3[:, 0:1, :]
        kb = k3 * beta3
        a = jnp.einsum('cid,cjd->cij', kb.astype(BF16), k3b, preferred_element_type=F32)
        a = jnp.where(strict, a * decay, 0.0)
        qk = jnp.einsum('cid,cjd->cij', q3b, k3b, preferred_element_type=F32)
        qk = jnp.where(causal, qk * decay, 0.0)
        a_diag = jnp.where(sub_blk, a, 0.0)
        dinv = neg_pow_inverse(-a_diag, 2, DN_SUB)
        n_off = jnp.einsum('cij,cjk->cik', dinv.astype(BF16), (a - a_diag).astype(BF16), preferred_element_type=F32)
        pinv = neg_pow_inverse(-n_off, 2 * DN_SUB, c)
        pinv = jnp.einsum('cij,cjk->cik', pinv.astype(BF16), dinv.astype(BF16), preferred_element_type=F32)
        rhs = jnp.concatenate([v3 * beta3, kb * jnp.exp(gc3)], axis=-1)
        sol = jnp.einsum('cij,cjd->cid', pinv.astype(BF16), rhs.astype(BF16), preferred_element_type=F32)
        solb = sol.astype(BF16)
        qs = jnp.einsum('cij,cjd->cid', qk.astype(BF16), solb, preferred_element_type=F32)
        o_s[d] = qs[:, :, :dk].reshape(t, dk)
        qp_s[d] = (q3 * jnp.exp(gc3) - qs[:, :, dk:]).reshape(t, dk).astype(BF16)
        ke = (k3 * jnp.exp(glast3 - gc3)).astype(BF16)
        solt = jnp.swapaxes(sol, 1, 2).astype(BF16)
        nw2 = jnp.einsum('cdi,cik->cdk', solt, ke, preferred_element_type=F32)
        nt_s[d] = nw2[:, :dk, :]
        w2t_s[d] = nw2[:, dk:, :].astype(BF16)
        el_s[d] = jnp.broadcast_to(jnp.exp(glast3), (nch, 8, dk))

    def chunk_step(d, ci, st):
        r0 = pl.multiple_of(ci * c, c)
        stb = st.astype(BF16)
        o_s[d, pl.ds(r0, c), :] += lax.dot_general(qp_s[d, pl.ds(r0, c), :], stb, (((1,), (1,)), ((), ())),
                                                   preferred_element_type=F32)
        return (st * el_s[d, ci][0:1, :] + nt_s[d, ci]
                - jnp.dot(stb, w2t_s[d, ci], preferred_element_type=F32))

    def ctx_body(n, carry):
        return chunk_step(0, n, carry[0]), chunk_step(1, ncc - 1 - n, carry[1])

    def lat_body(n, carry):
        return chunk_step(0, ncc + n, carry[0]), chunk_step(1, nch - 1 - n, carry[1])

    zero = jnp.zeros((dk, dk), F32)
    carry = lax.fori_loop(0, ncc, ctx_body, (zero, zero))
    lax.fori_loop(0, nch - ncc, lat_body, carry)

    o = o_s[0] + o_s[1]
    on = o * lax.rsqrt(jnp.mean(o * o, axis=-1, keepdims=True) + EPS) * nw_ref[...]
    z = jnp.concatenate([zc_ref[...], zl_ref[...]], axis=0).astype(F32)
    out = (on * _silu(z)).astype(ol_ref.dtype)
    oc_ref[...] = out[:ctx]
    ol_ref[...] = out[ctx:]


def deltanet_mix(p_main, p_ba, conv_w, a_log, dt_bias, norm_w, *, bsz, seq, ctx, u_width):
    n_heads = a_log.shape[-1]
    dk = norm_w.shape[-1]
    c = DN_CHUNK
    t = seq + ctx
    nch = t // c
    assert seq % c == 0 and ctx % c == 0 and u_width % dk == 0 and 4 * n_heads <= p_ba.shape[1]
    cb = u_width // dk
    lat_rows = bsz * seq
    cw = conv_w.astype(F32).reshape(conv_w.shape[0], 3, n_heads, dk).transpose(2, 1, 0, 3)
    lanes = p_ba.shape[1]
    lp = jnp.zeros((2, lanes), F32)
    lp = lp.at[0, 2 * n_heads:4 * n_heads].set(jnp.exp(a_log.astype(F32)).reshape(-1))
    lp = lp.at[1, 2 * n_heads:4 * n_heads].set(dt_bias.astype(F32).reshape(-1))
    cblk = lat_rows // ctx

    def lat_spec(off):
        return pl.BlockSpec((seq, dk), lambda b, h: (b, off + h))

    def ctx_spec(off):
        return pl.BlockSpec((ctx, dk), lambda b, h: (cblk + b, off + h))

    yl, yc = pl.pallas_call(
        functools.partial(_dn_kernel, n_heads=n_heads, ctx=ctx),
        out_shape=(jax.ShapeDtypeStruct((lat_rows, n_heads * dk), BF16),
                   jax.ShapeDtypeStruct((bsz * ctx, n_heads * dk), BF16)),
        grid=(bsz, n_heads),
        in_specs=[
            lat_spec(cb), lat_spec(cb + n_heads), lat_spec(cb + 2 * n_heads),
            ctx_spec(cb), ctx_spec(cb + n_heads), ctx_spec(cb + 2 * n_heads),
            pl.BlockSpec((seq, lanes), lambda b, h: (b, 0)),
            pl.BlockSpec((ctx, lanes), lambda b, h: (cblk + b, 0)),
            lat_spec(cb + 3 * n_heads), ctx_spec(cb + 3 * n_heads),
            pl.BlockSpec((1, 3, conv_w.shape[0], dk), lambda b, h: (h, 0, 0, 0)),
            pl.BlockSpec((2, lanes), lambda b, h: (0, 0)),
            pl.BlockSpec((1, dk), lambda b, h: (0, 0)),
        ],
        out_specs=(pl.BlockSpec((seq, dk), lambda b, h: (b, h)),
                   pl.BlockSpec((ctx, dk), lambda b, h: (b, h))),
        scratch_shapes=[
            pltpu.VMEM((2, nch, dk, dk), F32),
            pltpu.VMEM((2, nch, dk, dk), BF16),
            pltpu.VMEM((2, t, dk), BF16),
            pltpu.VMEM((2, nch, 8, dk), F32),
            pltpu.VMEM((2, t, dk), F32),
        ],
        compiler_params=_cparams(2),
        name="deltanet",
    )(p_main, p_main, p_main, p_main, p_main, p_main, p_ba, p_ba, p_main, p_main,
      cw, lp, norm_w.astype(F32).reshape(1, dk))
    return jnp.concatenate([yl, yc], axis=0)


def _experts_kernel(be_ref, nv_ref, x_ref, wg_ref, wu_ref, wd_ref, o_ref, wgb, wub, wdb):
    i = pl.program_id(0)
    prev = be_ref[jnp.maximum(i - 1, 0)]

    @pl.when((i == 0) | (be_ref[i] != prev))
    def _():
        wgb[...] = wg_ref[0].astype(BF16)
        wub[...] = wu_ref[0].astype(BF16)
        wdb[...] = wd_ref[0].astype(BF16)

    @pl.when(i < nv_ref[0])
    def _():
        x = x_ref[...]
        g = jnp.dot(x, wgb[...], preferred_element_type=F32)
        u = jnp.dot(x, wub[...], preferred_element_type=F32)
        a = (_silu(g) * u).astype(BF16)
        o_ref[...] = jnp.dot(a, wdb[...], preferred_element_type=F32).astype(o_ref.dtype)

    @pl.when(i >= nv_ref[0])
    def _():
        o_ref[...] = jnp.zeros_like(o_ref)


def routed_experts(xg, block_e, n_valid, w_gate, w_up, w_down):
    rows, d = xg.shape
    f = w_gate.shape[-1]
    mb = MOE_BLOCK
    w_gate, w_up = w_gate.reshape(-1, d, f), w_up.reshape(-1, d, f)
    w_down = w_down.reshape(-1, f, d)
    return pl.pallas_call(
        _experts_kernel,
        out_shape=jax.ShapeDtypeStruct((rows, d), BF16),
        grid_spec=pltpu.PrefetchScalarGridSpec(
            num_scalar_prefetch=2,
            grid=(rows // mb,),
            in_specs=[
                pl.BlockSpec((mb, d), lambda i, be, nv: (i, 0)),
                pl.BlockSpec((1, d, f), lambda i, be, nv: (be[i], 0, 0)),
                pl.BlockSpec((1, d, f), lambda i, be, nv: (be[i], 0, 0)),
                pl.BlockSpec((1, f, d), lambda i, be, nv: (be[i], 0, 0)),
            ],
            out_specs=pl.BlockSpec((mb, d), lambda i, be, nv: (i, 0)),
            scratch_shapes=[pltpu.VMEM((d, f), BF16), pltpu.VMEM((d, f), BF16), pltpu.VMEM((f, d), BF16)],
        ),
        compiler_params=_cparams(1),
        name="routed_experts",
    )(block_e, n_valid, xg, w_gate, w_up, w_down)


def _swiglu_kernel(x_ref, wg_ref, wu_ref, wd_ref, o_ref, wgb, wub, wdb):
    @pl.when(pl.program_id(0) == 0)
    def _():
        wgb[...] = wg_ref[...].astype(BF16)
        wub[...] = wu_ref[...].astype(BF16)
        wdb[...] = wd_ref[...].astype(BF16)

    x = x_ref[...]
    g = jnp.dot(x, wgb[...], preferred_element_type=F32)
    u = jnp.dot(x, wub[...], preferred_element_type=F32)
    a = (_silu(g) * u).astype(BF16)
    o_ref[...] = jnp.dot(a, wdb[...], preferred_element_type=F32).astype(o_ref.dtype)


def shared_expert(x, wg, wu, wd, *, layer, rows, tm=None):
    d = x.shape[1]
    f = wg.shape[-1]
    tm = _row_tile(rows, tm)
    const = lambda i: (0, 0)
    return pl.pallas_call(
        _swiglu_kernel,
        out_shape=jax.ShapeDtypeStruct((rows, d), F32),
        grid=(rows // tm,),
        in_specs=[
            pl.BlockSpec((tm, d), lambda i: (i, 0)),
            _wspec(wg, layer, (d, f), const, pipeline_mode=pl.Buffered(1)),
            _wspec(wu, layer, (d, f), const, pipeline_mode=pl.Buffered(1)),
            _wspec(wd, layer, (f, d), const, pipeline_mode=pl.Buffered(1)),
        ],
        out_specs=pl.BlockSpec((tm, d), lambda i: (i, 0)),
        scratch_shapes=[pltpu.VMEM((d, f), BF16), pltpu.VMEM((d, f), BF16), pltpu.VMEM((f, d), BF16)],
        compiler_params=_cparams(1),
        name="shared_expert",
    )(x, wg, wu, wd)


def _ffn_res_kernel(yk_ref, w_ref, s_ref, x_ref, nw_ref, g_ref, o_ref, *, top_k):
    d = x_ref.shape[1]
    w = w_ref[...]
    f = s_ref[...]
    for k in range(top_k):
        f = f + yk_ref[:, k * d:(k + 1) * d].astype(F32) * w[:, k:k + 1]
    fn = f * lax.rsqrt(jnp.mean(f * f, axis=-1, keepdims=True) + EPS) * nw_ref[...]
    o_ref[...] = x_ref[...] + g_ref[0] * fn


def ffn_residual(yk, wts, shared, x, nw, gate, set_of_tile, *, rows, tm=256):
    d = x.shape[1]
    top_k = wts.shape[1]
    tm = _row_tile(rows, tm)
    return pl.pallas_call(
        functools.partial(_ffn_res_kernel, top_k=top_k),
        out_shape=jax.ShapeDtypeStruct((rows, d), F32),
        grid=(rows // tm,),
        in_specs=[
            pl.BlockSpec((tm, top_k * d), lambda i: (i, 0)),
            pl.BlockSpec((tm, top_k), lambda i: (i, 0)),
            pl.BlockSpec((tm, d), lambda i: (i, 0)),
            pl.BlockSpec((tm, d), lambda i: (i, 0)),
            pl.BlockSpec((1, d), lambda i: (0, 0)),
            pl.BlockSpec((1, 1, d), lambda i: (set_of_tile(i, tm), 0, 0)),
        ],
        out_specs=pl.BlockSpec((tm, d), lambda i: (i, 0)),
        compiler_params=_cparams(1),
        name="ffn_residual",
    )(yk, wts, shared, x, nw.reshape(1, d), gate)


def _route_kernel(h_ref, wr_ref, b_ref, eidx_ref, wts_ref, rank_ref, cnt_ref, wrb, tri, cnt_s, *,
                  n_groups, topk_groups, top_k):
    n_exp, tm = wr_ref.shape[0], h_ref.shape[0]
    gs = n_exp // n_groups
    ninf = -jnp.inf

    @pl.when(pl.program_id(0) == 0)
    def _():
        wrb[...] = wr_ref[...].astype(BF16)
        tri[...] = (lax.broadcasted_iota(jnp.int32, (tm, tm), 0)
                    < lax.broadcasted_iota(jnp.int32, (tm, tm), 1)).astype(BF16)
        cnt_s[...] = jnp.zeros_like(cnt_s)

    logits = lax.dot_general(wrb[...], h_ref[...], (((1,), (1,)), ((), ())), preferred_element_type=F32)
    scores = jax.nn.sigmoid(logits)
    sel = scores + b_ref[:, 0:1]
    s3 = sel.reshape(n_groups, gs, tm)
    io3 = lax.broadcasted_iota(jnp.int32, (n_groups, gs, tm), 1)
    m1 = jnp.max(s3, axis=1, keepdims=True)
    i1 = jnp.min(jnp.where(s3 == m1, io3, gs), axis=1, keepdims=True)
    m2 = jnp.max(jnp.where(io3 == i1, ninf, s3), axis=1, keepdims=True)
    gscore = (m1 + m2).reshape(n_groups, tm)
    iog = lax.broadcasted_iota(jnp.int32, (n_groups, tm), 0)
    gsel = jnp.zeros((n_groups, tm), jnp.bool_)
    for _ in range(topk_groups):
        gm = jnp.max(gscore, axis=0, keepdims=True)
        gi = jnp.min(jnp.where(gscore == gm, iog, n_groups), axis=0, keepdims=True)
        hit = iog == gi
        gsel = gsel | hit
        gscore = jnp.where(hit, ninf, gscore)
    x = jnp.where(gsel.reshape(n_groups, 1, tm), s3, ninf).reshape(n_exp, tm)
    ioe = lax.broadcasted_iota(jnp.int32, (n_exp, tm), 0)
    hits = []
    chosen = jnp.zeros((n_exp, tm), jnp.bool_)
    for k in range(top_k):
        m = jnp.max(x, axis=0, keepdims=True)
        idx = jnp.min(jnp.where(x == m, ioe, n_exp), axis=0, keepdims=True)
        hit = ioe == idx
        x = jnp.where(hit, ninf, x)
        chosen = chosen | hit
        hits.append(hit)
        eidx_ref[k:k + 1, :] = idx
    wsel = jnp.where(chosen, scores, 0.0)
    wd = wsel / jnp.sum(wsel, axis=0, keepdims=True) * ROUTE_SCALE
    cm = jnp.where(chosen, 1.0, 0.0)
    rank = jnp.dot(cm.astype(BF16), tri[...], preferred_element_type=F32) + cnt_s[:, 0:1]
    for k in range(top_k):
        wts_ref[k:k + 1, :] = jnp.sum(jnp.where(hits[k], wd, 0.0), axis=0, keepdims=True)
        rank_ref[k:k + 1, :] = jnp.sum(jnp.where(hits[k], rank, 0.0), axis=0, keepdims=True).astype(jnp.int32)
    cnt_s[...] = cnt_s[...] + jnp.sum(cm, axis=1, keepdims=True)
    cnt_ref[...] = cnt_s[...].astype(jnp.int32)


def route(h, w_router, e_bias, *, rows, tm=None):
    d = h.shape[1]
    n_exp = w_router.shape[1]
    tm = _row_tile(rows, tm)
    bias = jnp.broadcast_to(e_bias.astype(F32).reshape(n_exp, 1), (n_exp, LANES))
    kt = lambda i: (0, i)
    eidx, wts, rank, cnt = pl.pallas_call(
        functools.partial(_route_kernel, n_groups=N_GROUPS, topk_groups=TOPK_GROUPS, top_k=TOP_K),
        out_shape=(jax.ShapeDtypeStruct((TOP_K, rows), jnp.int32), jax.ShapeDtypeStruct((TOP_K, rows), F32),
                   jax.ShapeDtypeStruct((TOP_K, rows), jnp.int32), jax.ShapeDtypeStruct((n_exp, LANES), jnp.int32)),
        grid=(rows // tm,),
        in_specs=[
            pl.BlockSpec((tm, d), lambda i: (i, 0)),
            pl.BlockSpec((n_exp, d), lambda i: (0, 0)),
            pl.BlockSpec((n_exp, LANES), lambda i: (0, 0)),
        ],
        out_specs=(pl.BlockSpec((TOP_K, tm), kt), pl.BlockSpec((TOP_K, tm), kt), pl.BlockSpec((TOP_K, tm), kt),
                   pl.BlockSpec((n_exp, LANES), lambda i: (0, 0))),
        scratch_shapes=[pltpu.VMEM((n_exp, d), BF16), pltpu.VMEM((tm, tm), BF16), pltpu.VMEM((n_exp, LANES), F32)],
        compiler_params=_cparams(1),
        name="route",
    )(h, w_router.T, bias)
    return eidx, wts, rank, cnt[:, 0]


def moe_ffn(h, w_router, e_bias, w_gate, w_up, w_down, ws_gate, ws_up, ws_down, *, layer, rows):
    d = h.shape[1]
    n_exp = w_router.shape[1]
    eidx, wts, rank, counts = route(h, w_router, e_bias, rows=rows)
    mb = MOE_BLOCK
    padded = (counts + mb - 1) // mb * mb
    pad_end = jnp.cumsum(padded)
    pad_start = pad_end - padded
    dest = (pad_start[eidx] + rank).T
    n_blocks = (rows * TOP_K + n_exp * (mb - 1) + mb - 1) // mb
    tok = jnp.zeros((n_blocks * mb,), jnp.int32).at[dest.reshape(-1)].set(
        jnp.repeat(jnp.arange(rows, dtype=jnp.int32), TOP_K))
    n_valid = (pad_end[-1] // mb).astype(jnp.int32)
    starts = jnp.arange(n_blocks, dtype=jnp.int32) * mb
    block_e = jnp.minimum(jnp.sum(starts[:, None] >= pad_end[None, :], axis=1), n_exp - 1).astype(jnp.int32)
    last_e = block_e[jnp.maximum(n_valid - 1, 0)]
    block_e = jnp.where(jnp.arange(n_blocks) < n_valid, block_e, last_e)
    y = routed_experts(h[tok], block_e + layer * n_exp, n_valid.reshape(1), w_gate, w_up, w_down)
    yk = y[dest.reshape(-1)].reshape(rows, TOP_K * d)
    shared = shared_expert(h, ws_gate, ws_up, ws_down, layer=layer, rows=rows)
    return yk, wts.T, shared


def kernel(x, c, ctx, c_ctx, w_mod, b_mod, norm_mix_pre, norm_mix_post, norm_ffn_pre, norm_ffn_post,
           w_in, s5_lam_re, s5_lam_im, s5_log_step, s5_b_re, s5_b_im, s5_c_re, s5_c_im, s5_d, s5_w_glu,
           dn_conv, dn_a_log, dn_dt_bias, dn_norm, w_br_s5, w_br_dn, w_out,
           moe_router, moe_bias, moe_w_gate, moe_w_up, moe_w_down, sh_w_gate, sh_w_up, sh_w_down):
    bsz, seq, d = x.shape
    n_ctx = ctx.shape[1]
    depth = w_mod.shape[0]
    lat_rows, ctx_rows = bsz * seq, bsz * n_ctx
    all_rows = lat_rows + ctx_rows
    s5_width = s5_d.shape[1]
    dn_width = w_br_dn.shape[1]
    n_heads = dn_a_log.shape[-1]
    main_cols = s5_width + 4 * dn_width
    ba_cols = 4 * n_heads
    lanes = 128
    nc = (seq + n_ctx) // S5_CHUNK
    n_levels = max(1, (nc - 1).bit_length())

    def set_of_tile(i, tm):
        return jnp.minimum((i * tm) // seq, bsz)

    xs = jnp.concatenate([x.reshape(lat_rows, d), ctx.reshape(ctx_rows, d)], axis=0)
    n_sets = 8
    cin = jnp.zeros((n_sets, d), F32).at[:bsz].set(_silu(c)).at[bsz].set(_silu(c_ctx))
    for i in range(depth):
        last = i == depth - 1
        rows = lat_rows if last else all_rows
        mods = matmul(cin, w_mod, b_mod[i].reshape(1, -1), layer=i, tm=n_sets, name="mods").reshape(n_sets, 6, 1, d)
        mod = [mods[:, k] for k in range(6)]
        hmix = prenorm(xs, norm_mix_pre[i], mod[0], mod[1], set_of_tile)
        p_main = matmul(hmix, w_in, layer=i, n_cols=main_cols, out_dtype=BF16, name="in_proj")
        w_tail = w_in[i, :, main_cols:]
        p_ba = matmul(hmix, jnp.pad(w_tail[:, :ba_cols], ((0, 0), (0, lanes - ba_cols))), tn=lanes, name="in_proj_ba")
        gates = matmul(hmix, w_tail[:, ba_cols:], rows=rows, out_dtype=BF16, name="in_proj_gates")
        ops = s5_operators(s5_lam_re[i], s5_lam_im[i], s5_log_step[i], s5_b_re[i], s5_b_im[i],
                           s5_c_re[i], s5_c_im[i], n_levels)
        y_s5 = s5_mix(p_main[:, :s5_width], ops, s5_d[i], bsz=bsz, seq=seq, ctx=n_ctx)
        y_s5 = s5_glu(y_s5, s5_w_glu, layer=i, rows=rows)
        y_dn = deltanet_mix(p_main, p_ba, dn_conv[i], dn_a_log[i], dn_dt_bias[i], dn_norm[i],
                            bsz=bsz, seq=seq, ctx=n_ctx, u_width=s5_width)
        m = branch_merge(y_s5, y_dn, gates, w_br_s5, w_br_dn, layer=i, rows=rows)
        xs = outproj_residual(m, w_out, xs, norm_mix_post[i], mod[2], set_of_tile, layer=i, rows=rows)
        hffn = prenorm(xs, norm_ffn_pre[i], mod[3], mod[4], set_of_tile, rows=rows)
        yk, wts, shared = moe_ffn(hffn, moe_router[i], moe_bias[i], moe_w_gate, moe_w_up, moe_w_down,
                                  sh_w_gate, sh_w_up, sh_w_down, layer=i, rows=rows)
        xs = ffn_residual(yk, wts, shared, xs, norm_ffn_post[i], mod[5], set_of_tile, rows=rows)
    return xs[:lat_rows].reshape(bsz, seq, d)
```

```python
import functools

import jax
import jax.numpy as jnp
from jax import lax
from jax.experimental import pallas as pl
from jax.experimental.pallas import tpu as pltpu

F32 = jnp.float32
BF16 = jnp.bfloat16

EPS = 1e-6
GRID_W = 64
S5_CHUNK = 16
DN_CHUNK = 64
DN_SUB = 16
TOP_K = 8
N_GROUPS = 8
TOPK_GROUPS = 4
ROUTE_SCALE = 2.5
MOE_BLOCK = 256

V7X_VMEM_LIMIT = 56 * 1024 * 1024
LANES = 128
ROW_TILE = 1024


def _cparams(n_axes, vmem=V7X_VMEM_LIMIT):
    return pltpu.CompilerParams(dimension_semantics=("arbitrary",) * n_axes, vmem_limit_bytes=vmem)


def _silu(x):
    return x * jax.nn.sigmoid(x)


def _row_tile(rows, tm=None):
    tm = min(tm or ROW_TILE, ROW_TILE, rows)
    assert rows % tm == 0, (rows, tm)
    return tm


def _col_tile(n, tn, col0=0):
    tn = min(tn, n)
    while n % tn or col0 % tn:
        tn -= LANES
    return tn


def _wspec(w, layer, block, index_map, **kw):
    if w.ndim == len(block):
        return pl.BlockSpec(block, index_map, **kw)
    return pl.BlockSpec((None,) + tuple(block), lambda *a: (layer,) + tuple(index_map(*a)), **kw)


def _mm_kernel(x_ref, w_ref, b_ref, o_ref, wbf_ref):
    @pl.when(pl.program_id(1) == 0)
    def _():
        wbf_ref[...] = w_ref[...].astype(BF16)

    acc = jnp.dot(x_ref[...].astype(BF16), wbf_ref[...], preferred_element_type=F32)
    o_ref[...] = (acc + b_ref[...]).astype(o_ref.dtype)


def matmul(x, w, bias=None, *, layer=0, n_cols=None, col0=0, out_dtype=F32, tm=None, tn=1024, rows=None,
           name="matmul"):
    m, k = x.shape
    rows = m if rows is None else rows
    n_cols = w.shape[-1] - col0 if n_cols is None else n_cols
    tm = _row_tile(rows, tm)
    tn = _col_tile(n_cols, tn, col0)
    assert rows % tm == 0 and n_cols % tn == 0 and col0 % tn == 0, (rows, tm, n_cols, tn, col0)
    if bias is None:
        bias = jnp.zeros((1, n_cols), F32)
    cb0 = col0 // tn
    return pl.pallas_call(
        _mm_kernel,
        out_shape=jax.ShapeDtypeStruct((rows, n_cols), out_dtype),
        grid=(n_cols // tn, rows // tm),
        in_specs=[
            pl.BlockSpec((tm, k), lambda j, i: (i, 0)),
            _wspec(w, layer, (k, tn), lambda j, i: (0, j + cb0)),
            pl.BlockSpec((1, tn), lambda j, i: (0, j)),
        ],
        out_specs=pl.BlockSpec((tm, tn), lambda j, i: (i, j)),
        scratch_shapes=[pltpu.VMEM((k, tn), BF16)],
        compiler_params=_cparams(2),
        name=name,
    )(x, w, bias)


def _prenorm_kernel(x_ref, w_ref, sh_ref, sc_ref, o_ref):
    x = x_ref[...]
    y = x * lax.rsqrt(jnp.mean(x * x, axis=-1, keepdims=True) + EPS) * w_ref[...]
    o_ref[...] = (y * (1.0 + sc_ref[0]) + sh_ref[0]).astype(o_ref.dtype)


def prenorm(x, w, shift, scale, set_of_tile, *, rows=None, tm=None, out_dtype=BF16):
    m, d = x.shape
    rows = m if rows is None else rows
    tm = _row_tile(rows, tm)
    return pl.pallas_call(
        _prenorm_kernel,
        out_shape=jax.ShapeDtypeStruct((rows, d), out_dtype),
        grid=(rows // tm,),
        in_specs=[
            pl.BlockSpec((tm, d), lambda i: (i, 0)),
            pl.BlockSpec((1, d), lambda i: (0, 0)),
            pl.BlockSpec((1, 1, d), lambda i: (set_of_tile(i, tm), 0, 0)),
            pl.BlockSpec((1, 1, d), lambda i: (set_of_tile(i, tm), 0, 0)),
        ],
        out_specs=pl.BlockSpec((tm, d), lambda i: (i, 0)),
        compiler_params=_cparams(1),
        name="prenorm",
    )(x, w.reshape(1, d), shift, scale)


def _glu_kernel(y_ref, w_ref, o_ref, wbf_ref, *, tn):
    @pl.when(pl.program_id(1) == 0)
    def _():
        wbf_ref[...] = w_ref[...].astype(BF16)

    j = pl.program_id(0)
    g = jax.nn.gelu(y_ref[...].astype(F32))
    acc = jnp.dot(g.astype(BF16), wbf_ref[...], preferred_element_type=F32)
    gj = jax.nn.gelu(y_ref[:, pl.ds(pl.multiple_of(j * tn, tn), tn)].astype(F32))
    o_ref[...] = (gj * jax.nn.sigmoid(acc)).astype(o_ref.dtype)


def s5_glu(y, w, *, layer, rows, tm=None, tn=512):
    m, k = y.shape
    tm = _row_tile(rows, tm)
    tn = _col_tile(k, tn)
    return pl.pallas_call(
        functools.partial(_glu_kernel, tn=tn),
        out_shape=jax.ShapeDtypeStruct((rows, k), BF16),
        grid=(k // tn, rows // tm),
        in_specs=[
            pl.BlockSpec((tm, k), lambda j, i: (i, 0)),
            _wspec(w, layer, (k, tn), lambda j, i: (0, j)),
        ],
        out_specs=pl.BlockSpec((tm, tn), lambda j, i: (i, j)),
        scratch_shapes=[pltpu.VMEM((k, tn), BF16)],
        compiler_params=_cparams(2),
        name="s5_glu",
    )(y, w)


def _merge_kernel(a_ref, b_ref, ga_ref, gb_ref, wa_ref, wb_ref, o_ref, wabf_ref, wbbf_ref):
    @pl.when(pl.program_id(1) == 0)
    def _():
        wabf_ref[...] = wa_ref[...].astype(BF16)
        wbbf_ref[...] = wb_ref[...].astype(BF16)

    ya = jnp.dot(a_ref[...], wabf_ref[...], preferred_element_type=F32)
    yb = jnp.dot(b_ref[...], wbbf_ref[...], preferred_element_type=F32)
    m = jax.nn.sigmoid(ga_ref[...].astype(F32)) * ya + jax.nn.sigmoid(gb_ref[...].astype(F32)) * yb
    o_ref[...] = m.astype(o_ref.dtype)


def branch_merge(ya, yb, gates, wa, wb, *, layer, rows, tm=None, tn=512):
    ka, kb = ya.shape[1], yb.shape[1]
    d = wa.shape[-1]
    tm = _row_tile(rows, tm)
    tn = _col_tile(d, tn)
    nb = d // tn
    return pl.pallas_call(
        _merge_kernel,
        out_shape=jax.ShapeDtypeStruct((rows, d), BF16),
        grid=(nb, rows // tm),
        in_specs=[
            pl.BlockSpec((tm, ka), lambda j, i: (i, 0)),
            pl.BlockSpec((tm, kb), lambda j, i: (i, 0)),
            pl.BlockSpec((tm, tn), lambda j, i: (i, j)),
            pl.BlockSpec((tm, tn), lambda j, i: (i, j + nb)),
            _wspec(wa, layer, (ka, tn), lambda j, i: (0, j)),
            _wspec(wb, layer, (kb, tn), lambda j, i: (0, j)),
        ],
        out_specs=pl.BlockSpec((tm, tn), lambda j, i: (i, j)),
        scratch_shapes=[pltpu.VMEM((ka, tn), BF16), pltpu.VMEM((kb, tn), BF16)],
        compiler_params=_cparams(2),
        name="branch_merge",
    )(ya, yb, gates, gates, wa, wb)


def _outproj_kernel(m_ref, w_ref, x_ref, nw_ref, g_ref, o_ref, wbf_ref):
    @pl.when(pl.program_id(0) == 0)
    def _():
        wbf_ref[...] = w_ref[...].astype(BF16)

    y = jnp.dot(m_ref[...], wbf_ref[...], preferred_element_type=F32)
    yn = y * lax.rsqrt(jnp.mean(y * y, axis=-1, keepdims=True) + EPS) * nw_ref[...]
    o_ref[...] = x_ref[...] + g_ref[0] * yn


def outproj_residual(m, w, x, nw, gate, set_of_tile, *, layer, rows, tm=512):
    d = w.shape[-1]
    k = w.shape[-2]
    tm = _row_tile(rows, tm)
    return pl.pallas_call(
        _outproj_kernel,
        out_shape=jax.ShapeDtypeStruct((rows, d), F32),
        grid=(rows // tm,),
        in_specs=[
            pl.BlockSpec((tm, k), lambda i: (i, 0)),
            _wspec(w, layer, (k, d), lambda i: (0, 0), pipeline_mode=pl.Buffered(1)),
            pl.BlockSpec((tm, d), lambda i: (i, 0)),
            pl.BlockSpec((1, d), lambda i: (0, 0)),
            pl.BlockSpec((1, 1, d), lambda i: (set_of_tile(i, tm), 0, 0)),
        ],
        out_specs=pl.BlockSpec((tm, d), lambda i: (i, 0)),
        scratch_shapes=[pltpu.VMEM((k, d), BF16)],
        compiler_params=_cparams(1),
        name="outproj_residual",
    )(m, w, x, nw.reshape(1, d), gate)


def s5_operators(lam_re, lam_im, log_step, b_re, b_im, c_re, c_im, n_levels):
    tc = S5_CHUNK
    hp = lax.Precision.HIGHEST
    lr, li = lam_re.astype(F32), lam_im.astype(F32)
    step = jnp.exp(log_step.astype(F32))[..., None]

    def apow(l):
        mag = jnp.exp(lr * step * l)
        return mag * jnp.cos(li * step * l), mag * jnp.sin(li * step * l)

    ab_re, ab_im = apow(1.0)
    den = lr * lr + li * li
    nr = ab_re - 1.0
    cr = (nr * lr + ab_im * li) / den
    ci = (ab_im * lr - nr * li) / den
    br, bi = b_re.astype(F32), b_im.astype(F32)
    bb_re = cr[..., None] * br - ci[..., None] * bi
    bb_im = cr[..., None] * bi + ci[..., None] * br
    lags = jnp.arange(tc + 1, dtype=F32)[:, None, None, None]
    pw_re, pw_im = apow(lags)
    pw_re, pw_im = jnp.moveaxis(pw_re, 0, 2), jnp.moveaxis(pw_im, 0, 2)
    cre, cim = c_re.astype(F32), c_im.astype(F32)
    cp_re = cre[:, :, None] * pw_re[:, :, :, None] - cim[:, :, None] * pw_im[:, :, :, None]
    cp_im = cre[:, :, None] * pw_im[:, :, :, None] + cim[:, :, None] * pw_re[:, :, :, None]
    kl = (jnp.einsum('dgljp,dgpi->dglji', cp_re[:, :, :tc], bb_re, precision=hp)
          - jnp.einsum('dgljp,dgpi->dglji', cp_im[:, :, :tc], bb_im, precision=hp))
    s_idx = jnp.arange(tc)[:, None]
    t_idx = jnp.arange(tc)[None, :]
    lag = t_idx - s_idx
    tmat = jnp.where((lag >= 0)[None, None, :, :, None, None], kl[:, :, jnp.clip(lag, 0, tc - 1)], 0.0)
    tmat = tmat.transpose(0, 1, 2, 5, 3, 4)
    pr, pi = pw_re[:, :, tc - 1 - jnp.arange(tc)], pw_im[:, :, tc - 1 - jnp.arange(tc)]
    win_re = pr[:, :, :, None, :] * bb_re.transpose(0, 1, 3, 2)[:, :, None] - pi[:, :, :, None, :] * bb_im.transpose(0, 1, 3, 2)[:, :, None]
    win_im = pr[:, :, :, None, :] * bb_im.transpose(0, 1, 3, 2)[:, :, None] + pi[:, :, :, None, :] * bb_re.transpose(0, 1, 3, 2)[:, :, None]
    win = jnp.concatenate([win_re, win_im], axis=-1)
    wo_re = cp_re[:, :, 1:].transpose(0, 1, 4, 2, 3)
    wo_im = -cp_im[:, :, 1:].transpose(0, 1, 4, 2, 3)
    wout = jnp.concatenate([wo_re, wo_im], axis=2)
    tmat = jnp.stack([tmat[0], tmat[1, :, ::-1, :, ::-1, :]])
    win = jnp.stack([win[0], win[1, :, ::-1]])
    wout = jnp.stack([wout[0], wout[1, :, :, ::-1]])
    g = lr.shape[1]
    j = br.shape[-1]
    p = lr.shape[-1]
    rows = []
    for k in range(n_levels):
        ar, ai = apow(float(tc * 2 ** k))
        rows.append(jnp.concatenate([ar, ar], axis=-1))
        rows.append(jnp.concatenate([-ai, ai], axis=-1))
    apw = jnp.stack(rows, axis=2)
    return (tmat.reshape(2, g, tc * j, tc * j).transpose(1, 0, 2, 3).astype(BF16),
            win.reshape(2, g, tc * j, 2 * p).transpose(1, 0, 2, 3).astype(BF16),
            wout.reshape(2, g, 2 * p, tc * j).transpose(1, 0, 2, 3).astype(BF16),
            apw.transpose(1, 0, 2, 3))


def _s5_kernel(u_ref, tm_ref, win_ref, wout_ref, apw_ref, dsk_ref, y_ref, *, nb, ctx_rows, n_levels, p):
    u = u_ref[0]
    n = u.shape[0]
    row = lax.broadcasted_iota(jnp.int32, (n, 1), 0)
    y = u.astype(F32) * dsk_ref[0]
    for d in range(2):
        y = y + jnp.dot(u, tm_ref[0, d], preferred_element_type=F32)
        x = jnp.dot(u, win_ref[0, d], preferred_element_type=F32)
        if d == 0:
            def shift(a, s):
                return jnp.where(row >= s, pltpu.roll(a, s, 0), 0.0)
        else:
            if ctx_rows:
                x = pltpu.roll(x, n - ctx_rows, 0)

            def shift(a, s):
                return jnp.where(row < n - s, pltpu.roll(a, n - s, 0), 0.0)
        x = shift(x, nb)
        for k in range(n_levels):
            sh = shift(x, nb * 2 ** k)
            a1 = apw_ref[0, d, 2 * k:2 * k + 1, :]
            a2 = apw_ref[0, d, 2 * k + 1:2 * k + 2, :]
            x = x + a1 * sh + a2 * pltpu.roll(sh, p, 1)
        if d == 1 and ctx_rows:
            x = pltpu.roll(x, ctx_rows, 0)
        y = y + jnp.dot(x.astype(BF16), wout_ref[0, d], preferred_element_type=F32)
    y_ref[0] = y.astype(y_ref.dtype)


def s5_scan(uc, tmat, win, wout, apw, dsk, *, nb, ctx_rows, n_levels):
    g, n, lanes = uc.shape
    p2 = win.shape[-1]
    return pl.pallas_call(
        functools.partial(_s5_kernel, nb=nb, ctx_rows=ctx_rows, n_levels=n_levels, p=p2 // 2),
        out_shape=jax.ShapeDtypeStruct((g, n, lanes), BF16),
        grid=(g,),
        in_specs=[
            pl.BlockSpec((1, n, lanes), lambda i: (i, 0, 0)),
            pl.BlockSpec((1, 2, lanes, lanes), lambda i: (i, 0, 0, 0)),
            pl.BlockSpec((1, 2, lanes, p2), lambda i: (i, 0, 0, 0)),
            pl.BlockSpec((1, 2, p2, lanes), lambda i: (i, 0, 0, 0)),
            pl.BlockSpec((1, 2, 2 * n_levels, p2), lambda i: (i, 0, 0, 0)),
            pl.BlockSpec((1, 1, lanes), lambda i: (i, 0, 0)),
        ],
        out_specs=pl.BlockSpec((1, n, lanes), lambda i: (i, 0, 0)),
        compiler_params=_cparams(1),
        name="s5_scan",
    )(uc, tmat, win, wout, apw, dsk)


def s5_mix(u_rows, ops, d_skip, *, bsz, seq, ctx):
    tmat, win, wout, apw = ops
    g = tmat.shape[0]
    lanes = tmat.shape[-1]
    tc = S5_CHUNK
    j = lanes // tc
    hgt = seq // GRID_W
    ul = u_rows[:bsz * seq].reshape(bsz, hgt, GRID_W, g, j).transpose(0, 2, 1, 3, 4).reshape(bsz, seq, g, j)
    uc = u_rows[bsz * seq:].reshape(bsz, ctx, g, j)
    useq = jnp.concatenate([uc, ul], axis=1)
    nc = (seq + ctx) // tc
    uch = useq.reshape(bsz, nc, tc, g, j).transpose(3, 1, 0, 2, 4).reshape(g, nc * bsz, lanes)
    n_levels = apw.shape[2] // 2
    dsk = jnp.tile(d_skip.astype(F32).reshape(g, 1, j), (1, tc, 1)).reshape(g, 1, lanes)
    ych = s5_scan(uch.astype(BF16), tmat, win, wout, apw, dsk, nb=bsz, ctx_rows=(ctx // tc) * bsz, n_levels=n_levels)
    yseq = ych.reshape(g, nc, bsz, tc, j).transpose(2, 1, 3, 0, 4).reshape(bsz, seq + ctx, g * j)
    yc = yseq[:, :ctx].reshape(bsz * ctx, g * j)
    yl = yseq[:, ctx:].reshape(bsz, GRID_W, hgt, g * j).transpose(0, 2, 1, 3).reshape(bsz * seq, g * j)
    return jnp.concatenate([yl, yc], axis=0)


def _softplus(x):
    return jnp.maximum(x, 0.0) + jnp.log1p(jnp.exp(-jnp.abs(x)))


def _dn_kernel(ql_ref, kl_ref, vl_ref, qc_ref, kc_ref, vc_ref, bal_ref, bac_ref, zl_ref, zc_ref,
               cw_ref, lp_ref, nw_ref, ol_ref, oc_ref,
               nt_s, w2t_s, qp_s, el_s, o_s, *, n_heads, ctx):
    c = DN_CHUNK
    seq = ql_ref.shape[0]
    t = seq + ctx
    dk = ql_ref.shape[1]
    nch = t // c
    ncc = ctx // c
    h = pl.program_id(1)
    row = lax.broadcasted_iota(jnp.int32, (t, 1), 0)
    rowc = row % c
    lane = lax.broadcasted_iota(jnp.int32, (1, dk), 1)

    first = (row == 0) | (row == ctx)
    last = (row == ctx - 1) | (row == t - 1)

    def conv_silu(xc_ref, xl_ref, kind):
        x = jnp.concatenate([xc_ref[...], xl_ref[...]], axis=0).astype(F32)
        xp = jnp.where(first, 0.0, pltpu.roll(x, 1, 0))
        xn = jnp.where(last, 0.0, pltpu.roll(x, t - 1, 0))
        w = cw_ref[0, kind]
        return _silu(xp * w[0:1] + x * w[1:2] + xn * w[2:3])

    def l2n(x):
        return x * lax.rsqrt(jnp.sum(x * x, axis=-1, keepdims=True) + EPS)

    q = l2n(conv_silu(qc_ref, ql_ref, 0)) * (dk ** -0.5)
    k = l2n(conv_silu(kc_ref, kl_ref, 1))
    v = conv_silu(vc_ref, vl_ref, 2)

    ba = jnp.concatenate([bac_ref[...], bal_ref[...]], axis=0)
    beta_all = jax.nn.sigmoid(ba)
    g_all = -lp_ref[0:1, :] * _softplus(ba + lp_ref[1:2, :])
    pf, sf = g_all, g_all
    s = 1
    while s < c:
        pf = pf + jnp.where(rowc >= s, pltpu.roll(pf, s, 0), 0.0)
        sf = sf + jnp.where(rowc < c - s, pltpu.roll(sf, t - s, 0), 0.0)
        s *= 2

    def col(a, idx):
        return jnp.sum(jnp.where(lane == idx, a, 0.0), axis=1, keepdims=True)

    causal_f = (lax.broadcasted_iota(jnp.int32, (c, c), 0) >= lax.broadcasted_iota(jnp.int32, (c, c), 1))[None]
    strict_f = (lax.broadcasted_iota(jnp.int32, (c, c), 0) > lax.broadcasted_iota(jnp.int32, (c, c), 1))[None]
    causal_b = (lax.broadcasted_iota(jnp.int32, (c, c), 0) <= lax.broadcasted_iota(jnp.int32, (c, c), 1))[None]
    strict_b = (lax.broadcasted_iota(jnp.int32, (c, c), 0) < lax.broadcasted_iota(jnp.int32, (c, c), 1))[None]
    eye = (lax.broadcasted_iota(jnp.int32, (c, c), 0) == lax.broadcasted_iota(jnp.int32, (c, c), 1))[None].astype(F32)

    sub_blk = (lax.broadcasted_iota(jnp.int32, (c, c), 0) // DN_SUB
               == lax.broadcasted_iota(jnp.int32, (c, c), 1) // DN_SUB)[None]

    def neg_pow_inverse(x, m, limit):
        pinv = eye + x
        while m < limit:
            xb = x.astype(BF16)
            x = jnp.einsum('cij,cjk->cik', xb, xb, preferred_element_type=F32)
            pinv = pinv + jnp.einsum('cij,cjk->cik', pinv.astype(BF16), x.astype(BF16), preferred_element_type=F32)
            m *= 2
        return pinv

    q3 = q.reshape(nch, c, dk)
    k3 = k.reshape(nch, c, dk)
    v3 = v.reshape(nch, c, dk)
    k3b = k3.astype(BF16)
    q3b = q3.astype(BF16)
    for d in range(2):
        causal, strict = (causal_f, strict_f) if d == 0 else (causal_b, strict_b)
        beta = col(beta_all, d * n_heads + h)
        gc = col(pf if d == 0 else sf, (2 + d) * n_heads + h)
        hi = gc.astype(BF16).astype(F32)
        mid = (gc - hi).astype(BF16).astype(F32)
        lo = gc - hi - mid
        pieces = (hi, mid, lo)
        g1 = jnp.zeros((t, dk), F32)
        g2 = jnp.zeros((t, dk), F32)
        for n_p, piece in enumerate(pieces):
            pb = jnp.broadcast_to(piece, (t, dk))
            g1 = jnp.where(lane == n_p, pb, jnp.where(lane == 3 + n_p, 1.0, g1))
            g2 = jnp.where(lane == n_p, 1.0, jnp.where(lane == 3 + n_p, -pb, g2))
        ldiff = jnp.einsum('cid,cjd->cij', g1.astype(BF16).reshape(nch, c, dk), g2.astype(BF16).reshape(nch, c, dk),
                           preferred_element_type=F32)
        decay = jnp.where(causal, jnp.exp(jnp.where(causal, ldiff, 0.0)), 0.0)
        beta3 = beta.reshape(nch, c, 1)
        gc3 = gc.reshape(nch, c, 1)
        glast3 = gc3[:, c - 1:c, :] if d == 0 else gc3[:, 0:1, :]
        kb = k3 * beta3
        a = jnp.einsum('cid,cjd->cij', kb.astype(BF16), k3b, preferred_element_type=F32)
        a = jnp.where(strict, a * decay, 0.0)
        qk = jnp.einsum('cid,cjd->cij', q3b, k3b, preferred_element_type=F32)
        qk = jnp.where(causal, qk * decay, 0.0)
        a_diag = jnp.where(sub_blk, a, 0.0)
        dinv = neg_pow_inverse(-a_diag, 2, DN_SUB)
        n_off = jnp.einsum('cij,cjk->cik', dinv.astype(BF16), (a - a_diag).astype(BF16), preferred_element_type=F32)
        pinv = neg_pow_inverse(-n_off, 2 * DN_SUB, c)
        pinv = jnp.einsum('cij,cjk->cik', pinv.astype(BF16), dinv.astype(BF16), preferred_element_type=F32)
        rhs = jnp.concatenate([v3 * beta3, kb * jnp.exp(gc3)], axis=-1)
        sol = jnp.einsum('cij,cjd->cid', pinv.astype(BF16), rhs.astype(BF16), preferred_element_type=F32)
        solb = sol.astype(BF16)
        qs = jnp.einsum('cij,cjd->cid', qk.astype(BF16), solb, preferred_element_type=F32)
        o_s[d] = qs[:, :, :dk].reshape(t, dk)
        qp_s[d] = (q3 * jnp.exp(gc3) - qs[:, :, dk:]).reshape(t, dk).astype(BF16)
        ke = (k3 * jnp.exp(glast3 - gc3)).astype(BF16)
        solt = jnp.swapaxes(sol, 1, 2).astype(BF16)
        nw2 = jnp.einsum('cdi,cik->cdk', solt, ke, preferred_element_type=F32)
        nt_s[d] = nw2[:, :dk, :]
        w2t_s[d] = nw2[:, dk:, :].astype(BF16)
        el_s[d] = jnp.broadcast_to(jnp.exp(glast3), (nch, 8, dk))

    def chunk_step(d, ci, st):
        r0 = pl.multiple_of(ci * c, c)
        stb = st.astype(BF16)
        o_s[d, pl.ds(r0, c), :] += lax.dot_general(qp_s[d, pl.ds(r0, c), :], stb, (((1,), (1,)), ((), ())),
                                                   preferred_element_type=F32)
        return (st * el_s[d, ci][0:1, :] + nt_s[d, ci]
                - jnp.dot(stb, w2t_s[d, ci], preferred_element_type=F32))

    def ctx_body(n, carry):
        return chunk_step(0, n, carry[0]), chunk_step(1, ncc - 1 - n, carry[1])

    def lat_body(n, carry):
        return chunk_step(0, ncc + n, carry[0]), chunk_step(1, nch - 1 - n, carry[1])

    zero = jnp.zeros((dk, dk), F32)
    carry = lax.fori_loop(0, ncc, ctx_body, (zero, zero))
    lax.fori_loop(0, nch - ncc, lat_body, carry)

    o = o_s[0] + o_s[1]
    on = o * lax.rsqrt(jnp.mean(o * o, axis=-1, keepdims=True) + EPS) * nw_ref[...]
    z = jnp.concatenate([zc_ref[...], zl_ref[...]], axis=0).astype(F32)
    out = (on * _silu(z)).astype(ol_ref.dtype)
    oc_ref[...] = out[:ctx]
    ol_ref[...] = out[ctx:]


def deltanet_mix(p_main, p_ba, conv_w, a_log, dt_bias, norm_w, *, bsz, seq, ctx, u_width):
    n_heads = a_log.shape[-1]
    dk = norm_w.shape[-1]
    c = DN_CHUNK
    t = seq + ctx
    nch = t // c
    assert seq % c == 0 and ctx % c == 0 and u_width % dk == 0 and 4 * n_heads <= p_ba.shape[1]
    cb = u_width // dk
    lat_rows = bsz * seq
    cw = conv_w.astype(F32).reshape(conv_w.shape[0], 3, n_heads, dk).transpose(2, 1, 0, 3)
    lanes = p_ba.shape[1]
    lp = jnp.zeros((2, lanes), F32)
    lp = lp.at[0, 2 * n_heads:4 * n_heads].set(jnp.exp(a_log.astype(F32)).reshape(-1))
    lp = lp.at[1, 2 * n_heads:4 * n_heads].set(dt_bias.astype(F32).reshape(-1))
    cblk = lat_rows // ctx

    def lat_spec(off):
        return pl.BlockSpec((seq, dk), lambda b, h: (b, off + h))

    def ctx_spec(off):
        return pl.BlockSpec((ctx, dk), lambda b, h: (cblk + b, off + h))

    yl, yc = pl.pallas_call(
        functools.partial(_dn_kernel, n_heads=n_heads, ctx=ctx),
        out_shape=(jax.ShapeDtypeStruct((lat_rows, n_heads * dk), BF16),
                   jax.ShapeDtypeStruct((bsz * ctx, n_heads * dk), BF16)),
        grid=(bsz, n_heads),
        in_specs=[
            lat_spec(cb), lat_spec(cb + n_heads), lat_spec(cb + 2 * n_heads),
            ctx_spec(cb), ctx_spec(cb + n_heads), ctx_spec(cb + 2 * n_heads),
            pl.BlockSpec((seq, lanes), lambda b, h: (b, 0)),
            pl.BlockSpec((ctx, lanes), lambda b, h: (cblk + b, 0)),
            lat_spec(cb + 3 * n_heads), ctx_spec(cb + 3 * n_heads),
            pl.BlockSpec((1, 3, conv_w.shape[0], dk), lambda b, h: (h, 0, 0, 0)),
            pl.BlockSpec((2, lanes), lambda b, h: (0, 0)),
            pl.BlockSpec((1, dk), lambda b, h: (0, 0)),
        ],
        out_specs=(pl.BlockSpec((seq, dk), lambda b, h: (b, h)),
                   pl.BlockSpec((ctx, dk), lambda b, h: (b, h))),
        scratch_shapes=[
            pltpu.VMEM((2, nch, dk, dk), F32),
            pltpu.VMEM((2, nch, dk, dk), BF16),
            pltpu.VMEM((2, t, dk), BF16),
            pltpu.VMEM((2, nch, 8, dk), F32),
            pltpu.VMEM((2, t, dk), F32),
        ],
        compiler_params=_cparams(2),
        name="deltanet",
    )(p_main, p_main, p_main, p_main, p_main, p_main, p_ba, p_ba, p_main, p_main,
      cw, lp, norm_w.astype(F32).reshape(1, dk))
    return jnp.concatenate([yl, yc], axis=0)


def _row_gather(table_hbm, idx_ref, buf, sem, n_rows):
    def copy(r, src_row):
        return pltpu.make_async_copy(table_hbm.at[pl.ds(src_row, 1)], buf.at[pl.ds(r, 1)], sem)

    def start():
        def body(r, carry):
            copy(r, idx_ref[0, 0, r]).start()
            return carry
        lax.fori_loop(0, n_rows, body, 0, unroll=8)

    def wait():
        def body(r, carry):
            copy(r, 0).wait()
            return carry
        lax.fori_loop(0, n_rows, body, 0, unroll=8)

    return start, wait


def _experts_kernel(be_ref, nv_ref, x_ref, wg_ref, wu_ref, wd_ref, o_ref, wgb, wub, wdb):
    i = pl.program_id(0)
    nv = nv_ref[0]
    prev = be_ref[jnp.maximum(i - 1, 0)]

    @pl.when((i == 0) | (be_ref[i] != prev))
    def _():
        wgb[...] = wg_ref[0].astype(BF16)
        wub[...] = wu_ref[0].astype(BF16)
        wdb[...] = wd_ref[0].astype(BF16)

    @pl.when(i < nv)
    def _():
        x = x_ref[...]
        g = jnp.dot(x, wgb[...], preferred_element_type=F32)
        u = jnp.dot(x, wub[...], preferred_element_type=F32)
        a = (_silu(g) * u).astype(BF16)
        o_ref[...] = jnp.dot(a, wdb[...], preferred_element_type=F32).astype(o_ref.dtype)

    @pl.when(i >= nv)
    def _():
        o_ref[...] = jnp.zeros_like(o_ref)


def routed_experts(xg, block_e, n_valid, w_gate, w_up, w_down):
    rows, d = xg.shape
    f = w_gate.shape[-1]
    mb = MOE_BLOCK
    w_gate, w_up = w_gate.reshape(-1, d, f), w_up.reshape(-1, d, f)
    w_down = w_down.reshape(-1, f, d)
    return pl.pallas_call(
        _experts_kernel,
        out_shape=jax.ShapeDtypeStruct((rows, d), F32),
        grid_spec=pltpu.PrefetchScalarGridSpec(
            num_scalar_prefetch=2,
            grid=(rows // mb,),
            in_specs=[
                pl.BlockSpec((mb, d), lambda i, be, nv: (i, 0)),
                pl.BlockSpec((1, d, f), lambda i, be, nv: (be[i], 0, 0)),
                pl.BlockSpec((1, d, f), lambda i, be, nv: (be[i], 0, 0)),
                pl.BlockSpec((1, f, d), lambda i, be, nv: (be[i], 0, 0)),
            ],
            out_specs=pl.BlockSpec((mb, d), lambda i, be, nv: (i, 0)),
            scratch_shapes=[pltpu.VMEM((d, f), BF16), pltpu.VMEM((d, f), BF16), pltpu.VMEM((f, d), BF16)],
        ),
        compiler_params=_cparams(1),
        name="routed_experts",
    )(block_e, n_valid, xg, w_gate, w_up, w_down)


def _swiglu_kernel(x_ref, wg_ref, wu_ref, wd_ref, o_ref, wgb, wub, wdb):
    @pl.when(pl.program_id(0) == 0)
    def _():
        wgb[...] = wg_ref[...].astype(BF16)
        wub[...] = wu_ref[...].astype(BF16)
        wdb[...] = wd_ref[...].astype(BF16)

    x = x_ref[...].astype(BF16)
    g = jnp.dot(x, wgb[...], preferred_element_type=F32)
    u = jnp.dot(x, wub[...], preferred_element_type=F32)
    a = (_silu(g) * u).astype(BF16)
    o_ref[...] = jnp.dot(a, wdb[...], preferred_element_type=F32).astype(o_ref.dtype)


def shared_expert(x, wg, wu, wd, *, layer, rows, tm=None):
    d = x.shape[1]
    f = wg.shape[-1]
    tm = _row_tile(rows, tm)
    const = lambda i: (0, 0)
    return pl.pallas_call(
        _swiglu_kernel,
        out_shape=jax.ShapeDtypeStruct((rows, d), F32),
        grid=(rows // tm,),
        in_specs=[
            pl.BlockSpec((tm, d), lambda i: (i, 0)),
            _wspec(wg, layer, (d, f), const, pipeline_mode=pl.Buffered(1)),
            _wspec(wu, layer, (d, f), const, pipeline_mode=pl.Buffered(1)),
            _wspec(wd, layer, (f, d), const, pipeline_mode=pl.Buffered(1)),
        ],
        out_specs=pl.BlockSpec((tm, d), lambda i: (i, 0)),
        scratch_shapes=[pltpu.VMEM((d, f), BF16), pltpu.VMEM((d, f), BF16), pltpu.VMEM((f, d), BF16)],
        compiler_params=_cparams(1),
        name="shared_expert",
    )(x, wg, wu, wd)


def _ffn_res_kernel(dst_ref, dstn_ref, y_hbm, w_ref, s_ref, x_ref, nw_ref, g_ref, o_ref, buf, sem, *, top_k):
    i = pl.program_id(0)
    tm = x_ref.shape[0]
    nr = top_k * tm
    slot = i % 2
    start_first, _ = _row_gather(y_hbm, dst_ref, buf.at[0], sem.at[0], nr)
    start_next, _ = _row_gather(y_hbm, dstn_ref, buf.at[1 - slot], sem.at[1 - slot], nr)
    _, wait_cur = _row_gather(y_hbm, dst_ref, buf.at[slot], sem.at[slot], nr)

    @pl.when(i == 0)
    def _():
        start_first()

    @pl.when(i + 1 < pl.num_programs(0))
    def _():
        start_next()

    wait_cur()
    w = w_ref[...]
    f = s_ref[...]
    for k in range(top_k):
        f = f + buf[slot, k * tm:(k + 1) * tm, :] * w[:, k:k + 1]
    fn = f * lax.rsqrt(jnp.mean(f * f, axis=-1, keepdims=True) + EPS) * nw_ref[...]
    o_ref[...] = x_ref[...] + g_ref[0] * fn


def ffn_residual(y, dest, wts, shared, x, nw, gate, set_of_tile, *, rows, tm=128):
    d = x.shape[1]
    top_k = wts.shape[1]
    tm = _row_tile(rows, tm)
    n_tiles = rows // tm
    dst3 = dest.reshape(top_k, n_tiles, tm).transpose(1, 0, 2).reshape(n_tiles, 1, top_k * tm)
    return pl.pallas_call(
        functools.partial(_ffn_res_kernel, top_k=top_k),
        out_shape=jax.ShapeDtypeStruct((rows, d), F32),
        grid=(n_tiles,),
        in_specs=[
            pl.BlockSpec((1, 1, top_k * tm), lambda i: (i, 0, 0), memory_space=pltpu.SMEM),
            pl.BlockSpec((1, 1, top_k * tm), lambda i: (jnp.minimum(i + 1, n_tiles - 1), 0, 0),
                         memory_space=pltpu.SMEM),
            pl.BlockSpec(memory_space=pl.ANY),
            pl.BlockSpec((tm, top_k), lambda i: (i, 0)),
            pl.BlockSpec((tm, d), lambda i: (i, 0)),
            pl.BlockSpec((tm, d), lambda i: (i, 0)),
            pl.BlockSpec((1, d), lambda i: (0, 0)),
            pl.BlockSpec((1, 1, d), lambda i: (set_of_tile(i, tm), 0, 0)),
        ],
        out_specs=pl.BlockSpec((tm, d), lambda i: (i, 0)),
        scratch_shapes=[pltpu.VMEM((2, top_k * tm, d), F32), pltpu.SemaphoreType.DMA((2,))],
        compiler_params=_cparams(1),
        name="ffn_residual",
    )(dst3, dst3, y, wts, shared, x, nw.reshape(1, d), gate)


def _route_kernel(h_ref, wr_ref, b_ref, eidx_ref, wts_ref, rank_ref, cnt_ref, wrb, tri, cnt_s, *,
                  n_groups, topk_groups, top_k):
    n_exp, tm = wr_ref.shape[0], h_ref.shape[0]
    gs = n_exp // n_groups
    ninf = -jnp.inf

    @pl.when(pl.program_id(0) == 0)
    def _():
        wrb[...] = wr_ref[...].astype(BF16)
        tri[...] = (lax.broadcasted_iota(jnp.int32, (tm, tm), 0)
                    < lax.broadcasted_iota(jnp.int32, (tm, tm), 1)).astype(BF16)
        cnt_s[...] = jnp.zeros_like(cnt_s)

    logits = lax.dot_general(wrb[...], h_ref[...].astype(BF16), (((1,), (1,)), ((), ())),
                             preferred_element_type=F32)
    scores = jax.nn.sigmoid(logits)
    sel = scores + b_ref[:, 0:1]
    s3 = sel.reshape(n_groups, gs, tm)
    io3 = lax.broadcasted_iota(jnp.int32, (n_groups, gs, tm), 1)
    m1 = jnp.max(s3, axis=1, keepdims=True)
    i1 = jnp.min(jnp.where(s3 == m1, io3, gs), axis=1, keepdims=True)
    m2 = jnp.max(jnp.where(io3 == i1, ninf, s3), axis=1, keepdims=True)
    gscore = (m1 + m2).reshape(n_groups, tm)
    iog = lax.broadcasted_iota(jnp.int32, (n_groups, tm), 0)
    gsel = jnp.zeros((n_groups, tm), jnp.bool_)
    for _ in range(topk_groups):
        gm = jnp.max(gscore, axis=0, keepdims=True)
        gi = jnp.min(jnp.where(gscore == gm, iog, n_groups), axis=0, keepdims=True)
        hit = iog == gi
        gsel = gsel | hit
        gscore = jnp.where(hit, ninf, gscore)
    x = jnp.where(gsel.reshape(n_groups, 1, tm), s3, ninf).reshape(n_exp, tm)
    ioe = lax.broadcasted_iota(jnp.int32, (n_exp, tm), 0)
    hits = []
    chosen = jnp.zeros((n_exp, tm), jnp.bool_)
    for k in range(top_k):
        m = jnp.max(x, axis=0, keepdims=True)
        idx = jnp.min(jnp.where(x == m, ioe, n_exp), axis=0, keepdims=True)
        hit = ioe == idx
        x = jnp.where(hit, ninf, x)
        chosen = chosen | hit
        hits.append(hit)
        eidx_ref[k:k + 1, :] = idx
    wsel = jnp.where(chosen, scores, 0.0)
    wd = wsel / jnp.sum(wsel, axis=0, keepdims=True) * ROUTE_SCALE
    cm = jnp.where(chosen, 1.0, 0.0)
    rank = jnp.dot(cm.astype(BF16), tri[...], preferred_element_type=F32) + cnt_s[:, 0:1]
    for k in range(top_k):
        wts_ref[k:k + 1, :] = jnp.sum(jnp.where(hits[k], wd, 0.0), axis=0, keepdims=True)
        rank_ref[k:k + 1, :] = jnp.sum(jnp.where(hits[k], rank, 0.0), axis=0, keepdims=True).astype(jnp.int32)
    cnt_s[...] = cnt_s[...] + jnp.sum(cm, axis=1, keepdims=True)
    cnt_ref[...] = cnt_s[...].astype(jnp.int32)


def route(h, w_router, e_bias, *, rows, tm=None):
    d = h.shape[1]
    n_exp = w_router.shape[1]
    tm = _row_tile(rows, tm)
    bias = jnp.broadcast_to(e_bias.astype(F32).reshape(n_exp, 1), (n_exp, LANES))
    kt = lambda i: (0, i)
    eidx, wts, rank, cnt = pl.pallas_call(
        functools.partial(_route_kernel, n_groups=N_GROUPS, topk_groups=TOPK_GROUPS, top_k=TOP_K),
        out_shape=(jax.ShapeDtypeStruct((TOP_K, rows), jnp.int32), jax.ShapeDtypeStruct((TOP_K, rows), F32),
                   jax.ShapeDtypeStruct((TOP_K, rows), jnp.int32), jax.ShapeDtypeStruct((n_exp, LANES), jnp.int32)),
        grid=(rows // tm,),
        in_specs=[
            pl.BlockSpec((tm, d), lambda i: (i, 0)),
            pl.BlockSpec((n_exp, d), lambda i: (0, 0)),
            pl.BlockSpec((n_exp, LANES), lambda i: (0, 0)),
        ],
        out_specs=(pl.BlockSpec((TOP_K, tm), kt), pl.BlockSpec((TOP_K, tm), kt), pl.BlockSpec((TOP_K, tm), kt),
                   pl.BlockSpec((n_exp, LANES), lambda i: (0, 0))),
        scratch_shapes=[pltpu.VMEM((n_exp, d), BF16), pltpu.VMEM((tm, tm), BF16), pltpu.VMEM((n_exp, LANES), F32)],
        compiler_params=_cparams(1),
        name="route",
    )(h, w_router.T, bias)
    return eidx, wts, rank, cnt[:, 0]


def moe_ffn(h, w_router, e_bias, w_gate, w_up, w_down, ws_gate, ws_up, ws_down, *, layer, rows):
    n_exp = w_router.shape[1]
    eidx, wts, rank, counts = route(h, w_router, e_bias, rows=rows)
    mb = MOE_BLOCK
    padded = (counts + mb - 1) // mb * mb
    pad_end = jnp.cumsum(padded)
    pad_start = pad_end - padded
    dest = pad_start[eidx] + rank
    n_blocks = (rows * TOP_K + n_exp * (mb - 1) + mb - 1) // mb
    tok = jnp.zeros((n_blocks * mb,), jnp.int32).at[dest.reshape(-1)].set(
        jnp.tile(jnp.arange(rows, dtype=jnp.int32), TOP_K))
    n_valid = (pad_end[-1] // mb).astype(jnp.int32)
    starts = jnp.arange(n_blocks, dtype=jnp.int32) * mb
    block_e = jnp.minimum(jnp.sum(starts[:, None] >= pad_end[None, :], axis=1), n_exp - 1).astype(jnp.int32)
    last_e = block_e[jnp.maximum(n_valid - 1, 0)]
    block_e = jnp.where(jnp.arange(n_blocks) < n_valid, block_e, last_e)
    y = routed_experts(h[tok], block_e + layer * n_exp, n_valid.reshape(1), w_gate, w_up, w_down)
    shared = shared_expert(h, ws_gate, ws_up, ws_down, layer=layer, rows=rows)
    return y, dest, wts.T, shared


def kernel(x, c, ctx, c_ctx, w_mod, b_mod, norm_mix_pre, norm_mix_post, norm_ffn_pre, norm_ffn_post,
           w_in, s5_lam_re, s5_lam_im, s5_log_step, s5_b_re, s5_b_im, s5_c_re, s5_c_im, s5_d, s5_w_glu,
           dn_conv, dn_a_log, dn_dt_bias, dn_norm, w_br_s5, w_br_dn, w_out,
           moe_router, moe_bias, moe_w_gate, moe_w_up, moe_w_down, sh_w_gate, sh_w_up, sh_w_down):
    bsz, seq, d = x.shape
    n_ctx = ctx.shape[1]
    depth = w_mod.shape[0]
    lat_rows, ctx_rows = bsz * seq, bsz * n_ctx
    all_rows = lat_rows + ctx_rows
    s5_width = s5_d.shape[1]
    dn_width = w_br_dn.shape[1]
    n_heads = dn_a_log.shape[-1]
    main_cols = s5_width + 4 * dn_width
    ba_cols = 4 * n_heads
    lanes = 128
    nc = (seq + n_ctx) // S5_CHUNK
    n_levels = max(1, (nc - 1).bit_length())

    def set_of_tile(i, tm):
        return jnp.minimum((i * tm) // seq, bsz)

    xs = jnp.concatenate([x.reshape(lat_rows, d), ctx.reshape(ctx_rows, d)], axis=0)
    n_sets = 8
    cin = jnp.zeros((n_sets, d), F32).at[:bsz].set(_silu(c)).at[bsz].set(_silu(c_ctx))
    for i in range(depth):
        last = i == depth - 1
        rows = lat_rows if last else all_rows
        mods = matmul(cin, w_mod, b_mod[i].reshape(1, -1), layer=i, tm=n_sets, name="mods").reshape(n_sets, 6, 1, d)
        mod = [mods[:, k] for k in range(6)]
        hmix = prenorm(xs, norm_mix_pre[i], mod[0], mod[1], set_of_tile)
        p_main = matmul(hmix, w_in, layer=i, n_cols=main_cols, out_dtype=BF16, name="in_proj")
        w_tail = w_in[i, :, main_cols:]
        p_ba = matmul(hmix, jnp.pad(w_tail[:, :ba_cols], ((0, 0), (0, lanes - ba_cols))), tn=lanes, name="in_proj_ba")
        gates = matmul(hmix, w_tail[:, ba_cols:], rows=rows, out_dtype=BF16, name="in_proj_gates")
        ops = s5_operators(s5_lam_re[i], s5_lam_im[i], s5_log_step[i], s5_b_re[i], s5_b_im[i],
                           s5_c_re[i], s5_c_im[i], n_levels)
        y_s5 = s5_mix(p_main[:, :s5_width], ops, s5_d[i], bsz=bsz, seq=seq, ctx=n_ctx)
        y_s5 = s5_glu(y_s5, s5_w_glu, layer=i, rows=rows)
        y_dn = deltanet_mix(p_main, p_ba, dn_conv[i], dn_a_log[i], dn_dt_bias[i], dn_norm[i],
                            bsz=bsz, seq=seq, ctx=n_ctx, u_width=s5_width)
        m = branch_merge(y_s5, y_dn, gates, w_br_s5, w_br_dn, layer=i, rows=rows)
        xs = outproj_residual(m, w_out, xs, norm_mix_post[i], mod[2], set_of_tile, layer=i, rows=rows)
        hffn = prenorm(xs, norm_ffn_pre[i], mod[3], mod[4], set_of_tile, rows=rows)
        y, dest, wts, shared = moe_ffn(hffn, moe_router[i], moe_bias[i], moe_w_gate, moe_w_up, moe_w_down,
                                       sh_w_gate, sh_w_up, sh_w_down, layer=i, rows=rows)
        xs = ffn_residual(y, dest, wts, shared, xs, norm_ffn_post[i], mod[5], set_of_tile, rows=rows)
    return xs[:lat_rows].reshape(bsz, seq, d)
```

```python
import functools

import jax
import jax.numpy as jnp
from jax import lax
from jax.experimental import pallas as pl
from jax.experimental.pallas import tpu as pltpu

F32 = jnp.float32
BF16 = jnp.bfloat16

EPS = 1e-6
GRID_W = 64
S5_CHUNK = 16
DN_CHUNK = 64
DN_SUB = 16
TOP_K = 8
N_GROUPS = 8
TOPK_GROUPS = 4
ROUTE_SCALE = 2.5
MOE_BLOCK = 256

V7X_VMEM_LIMIT = 56 * 1024 * 1024
LANES = 128
ROW_TILE = 1024


def _cparams(n_axes, vmem=V7X_VMEM_LIMIT):
    return pltpu.CompilerParams(dimension_semantics=("arbitrary",) * n_axes, vmem_limit_bytes=vmem)


def _silu(x):
    return x * jax.nn.sigmoid(x)


def _row_tile(rows, tm=None):
    tm = min(tm or ROW_TILE, ROW_TILE, rows)
    assert rows % tm == 0, (rows, tm)
    return tm


def _col_tile(n, tn, col0=0):
    tn = min(tn, n)
    while n % tn or col0 % tn:
        tn -= LANES
    return tn


def _wspec(w, layer, block, index_map, **kw):
    if w.ndim == len(block):
        return pl.BlockSpec(block, index_map, **kw)
    return pl.BlockSpec((None,) + tuple(block), lambda *a: (layer,) + tuple(index_map(*a)), **kw)


def _mm_kernel(x_ref, w_ref, b_ref, o_ref, wbf_ref):
    @pl.when(pl.program_id(1) == 0)
    def _():
        wbf_ref[...] = w_ref[...].astype(BF16)

    acc = jnp.dot(x_ref[...].astype(BF16), wbf_ref[...], preferred_element_type=F32)
    o_ref[...] = (acc + b_ref[...]).astype(o_ref.dtype)


def matmul(x, w, bias=None, *, layer=0, n_cols=None, col0=0, out_dtype=F32, tm=None, tn=1024, rows=None,
           name="matmul"):
    m, k = x.shape
    rows = m if rows is None else rows
    n_cols = w.shape[-1] - col0 if n_cols is None else n_cols
    tm = _row_tile(rows, tm)
    tn = _col_tile(n_cols, tn, col0)
    assert rows % tm == 0 and n_cols % tn == 0 and col0 % tn == 0, (rows, tm, n_cols, tn, col0)
    if bias is None:
        bias = jnp.zeros((1, n_cols), F32)
    cb0 = col0 // tn
    return pl.pallas_call(
        _mm_kernel,
        out_shape=jax.ShapeDtypeStruct((rows, n_cols), out_dtype),
        grid=(n_cols // tn, rows // tm),
        in_specs=[
            pl.BlockSpec((tm, k), lambda j, i: (i, 0)),
            _wspec(w, layer, (k, tn), lambda j, i: (0, j + cb0)),
            pl.BlockSpec((1, tn), lambda j, i: (0, j)),
        ],
        out_specs=pl.BlockSpec((tm, tn), lambda j, i: (i, j)),
        scratch_shapes=[pltpu.VMEM((k, tn), BF16)],
        compiler_params=_cparams(2),
        name=name,
    )(x, w, bias)


def _prenorm_kernel(x_ref, w_ref, sh_ref, sc_ref, o_ref):
    x = x_ref[...]
    y = x * lax.rsqrt(jnp.mean(x * x, axis=-1, keepdims=True) + EPS) * w_ref[...]
    o_ref[...] = (y * (1.0 + sc_ref[0]) + sh_ref[0]).astype(o_ref.dtype)


def prenorm(x, w, shift, scale, set_of_tile, *, rows=None, tm=None, out_dtype=BF16):
    m, d = x.shape
    rows = m if rows is None else rows
    tm = _row_tile(rows, tm)
    return pl.pallas_call(
        _prenorm_kernel,
        out_shape=jax.ShapeDtypeStruct((rows, d), out_dtype),
        grid=(rows // tm,),
        in_specs=[
            pl.BlockSpec((tm, d), lambda i: (i, 0)),
            pl.BlockSpec((1, d), lambda i: (0, 0)),
            pl.BlockSpec((1, 1, d), lambda i: (set_of_tile(i, tm), 0, 0)),
            pl.BlockSpec((1, 1, d), lambda i: (set_of_tile(i, tm), 0, 0)),
        ],
        out_specs=pl.BlockSpec((tm, d), lambda i: (i, 0)),
        compiler_params=_cparams(1),
        name="prenorm",
    )(x, w.reshape(1, d), shift, scale)


def _glu_kernel(y_ref, w_ref, o_ref, wbf_ref, *, tn):
    @pl.when(pl.program_id(1) == 0)
    def _():
        wbf_ref[...] = w_ref[...].astype(BF16)

    j = pl.program_id(0)
    g = jax.nn.gelu(y_ref[...].astype(F32))
    acc = jnp.dot(g.astype(BF16), wbf_ref[...], preferred_element_type=F32)
    gj = jax.nn.gelu(y_ref[:, pl.ds(pl.multiple_of(j * tn, tn), tn)].astype(F32))
    o_ref[...] = (gj * jax.nn.sigmoid(acc)).astype(o_ref.dtype)


def s5_glu(y, w, *, layer, rows, tm=None, tn=512):
    m, k = y.shape
    tm = _row_tile(rows, tm)
    tn = _col_tile(k, tn)
    return pl.pallas_call(
        functools.partial(_glu_kernel, tn=tn),
        out_shape=jax.ShapeDtypeStruct((rows, k), BF16),
        grid=(k // tn, rows // tm),
        in_specs=[
            pl.BlockSpec((tm, k), lambda j, i: (i, 0)),
            _wspec(w, layer, (k, tn), lambda j, i: (0, j)),
        ],
        out_specs=pl.BlockSpec((tm, tn), lambda j, i: (i, j)),
        scratch_shapes=[pltpu.VMEM((k, tn), BF16)],
        compiler_params=_cparams(2),
        name="s5_glu",
    )(y, w)


def _merge_kernel(a_ref, b_ref, ga_ref, gb_ref, wa_ref, wb_ref, o_ref, wabf_ref, wbbf_ref):
    @pl.when(pl.program_id(1) == 0)
    def _():
        wabf_ref[...] = wa_ref[...].astype(BF16)
        wbbf_ref[...] = wb_ref[...].astype(BF16)

    ya = jnp.dot(a_ref[...], wabf_ref[...], preferred_element_type=F32)
    yb = jnp.dot(b_ref[...], wbbf_ref[...], preferred_element_type=F32)
    m = jax.nn.sigmoid(ga_ref[...].astype(F32)) * ya + jax.nn.sigmoid(gb_ref[...].astype(F32)) * yb
    o_ref[...] = m.astype(o_ref.dtype)


def branch_merge(ya, yb, gates, wa, wb, *, layer, rows, tm=None, tn=512):
    ka, kb = ya.shape[1], yb.shape[1]
    d = wa.shape[-1]
    tm = _row_tile(rows, tm)
    tn = _col_tile(d, tn)
    nb = d // tn
    return pl.pallas_call(
        _merge_kernel,
        out_shape=jax.ShapeDtypeStruct((rows, d), BF16),
        grid=(nb, rows // tm),
        in_specs=[
            pl.BlockSpec((tm, ka), lambda j, i: (i, 0)),
            pl.BlockSpec((tm, kb), lambda j, i: (i, 0)),
            pl.BlockSpec((tm, tn), lambda j, i: (i, j)),
            pl.BlockSpec((tm, tn), lambda j, i: (i, j + nb)),
            _wspec(wa, layer, (ka, tn), lambda j, i: (0, j)),
            _wspec(wb, layer, (kb, tn), lambda j, i: (0, j)),
        ],
        out_specs=pl.BlockSpec((tm, tn), lambda j, i: (i, j)),
        scratch_shapes=[pltpu.VMEM((ka, tn), BF16), pltpu.VMEM((kb, tn), BF16)],
        compiler_params=_cparams(2),
        name="branch_merge",
    )(ya, yb, gates, gates, wa, wb)


def _outproj_kernel(m_ref, w_ref, x_ref, nw_ref, g_ref, o_ref, wbf_ref):
    @pl.when(pl.program_id(0) == 0)
    def _():
        wbf_ref[...] = w_ref[...].astype(BF16)

    y = jnp.dot(m_ref[...], wbf_ref[...], preferred_element_type=F32)
    yn = y * lax.rsqrt(jnp.mean(y * y, axis=-1, keepdims=True) + EPS) * nw_ref[...]
    o_ref[...] = x_ref[...] + g_ref[0] * yn


def outproj_residual(m, w, x, nw, gate, set_of_tile, *, layer, rows, tm=512):
    d = w.shape[-1]
    k = w.shape[-2]
    tm = _row_tile(rows, tm)
    return pl.pallas_call(
        _outproj_kernel,
        out_shape=jax.ShapeDtypeStruct((rows, d), F32),
        grid=(rows // tm,),
        in_specs=[
            pl.BlockSpec((tm, k), lambda i: (i, 0)),
            _wspec(w, layer, (k, d), lambda i: (0, 0), pipeline_mode=pl.Buffered(1)),
            pl.BlockSpec((tm, d), lambda i: (i, 0)),
            pl.BlockSpec((1, d), lambda i: (0, 0)),
            pl.BlockSpec((1, 1, d), lambda i: (set_of_tile(i, tm), 0, 0)),
        ],
        out_specs=pl.BlockSpec((tm, d), lambda i: (i, 0)),
        scratch_shapes=[pltpu.VMEM((k, d), BF16)],
        compiler_params=_cparams(1),
        name="outproj_residual",
    )(m, w, x, nw.reshape(1, d), gate)


def s5_operators(lam_re, lam_im, log_step, b_re, b_im, c_re, c_im, n_levels):
    tc = S5_CHUNK
    hp = lax.Precision.HIGHEST
    lr, li = lam_re.astype(F32), lam_im.astype(F32)
    step = jnp.exp(log_step.astype(F32))[..., None]

    def apow(l):
        mag = jnp.exp(lr * step * l)
        return mag * jnp.cos(li * step * l), mag * jnp.sin(li * step * l)

    ab_re, ab_im = apow(1.0)
    den = lr * lr + li * li
    nr = ab_re - 1.0
    cr = (nr * lr + ab_im * li) / den
    ci = (ab_im * lr - nr * li) / den
    br, bi = b_re.astype(F32), b_im.astype(F32)
    bb_re = cr[..., None] * br - ci[..., None] * bi
    bb_im = cr[..., None] * bi + ci[..., None] * br
    lags = jnp.arange(tc + 1, dtype=F32)[:, None, None, None]
    pw_re, pw_im = apow(lags)
    pw_re, pw_im = jnp.moveaxis(pw_re, 0, 2), jnp.moveaxis(pw_im, 0, 2)
    cre, cim = c_re.astype(F32), c_im.astype(F32)
    cp_re = cre[:, :, None] * pw_re[:, :, :, None] - cim[:, :, None] * pw_im[:, :, :, None]
    cp_im = cre[:, :, None] * pw_im[:, :, :, None] + cim[:, :, None] * pw_re[:, :, :, None]
    bbt = jnp.concatenate([bb_re.transpose(0, 1, 3, 2), -bb_im.transpose(0, 1, 3, 2)], axis=-1)
    cps = jnp.concatenate([cp_re[:, :, :tc], cp_im[:, :, :tc]], axis=-1)
    cps = jnp.stack([cps[0], cps[1, :, ::-1]])
    cps = cps.transpose(0, 1, 4, 2, 3)
    pr, pi = pw_re[:, :, tc - 1 - jnp.arange(tc)], pw_im[:, :, tc - 1 - jnp.arange(tc)]
    win_re = pr[:, :, :, None, :] * bb_re.transpose(0, 1, 3, 2)[:, :, None] - pi[:, :, :, None, :] * bb_im.transpose(0, 1, 3, 2)[:, :, None]
    win_im = pr[:, :, :, None, :] * bb_im.transpose(0, 1, 3, 2)[:, :, None] + pi[:, :, :, None, :] * bb_re.transpose(0, 1, 3, 2)[:, :, None]
    win = jnp.concatenate([win_re, win_im], axis=-1)
    wo_re = cp_re[:, :, 1:].transpose(0, 1, 4, 2, 3)
    wo_im = -cp_im[:, :, 1:].transpose(0, 1, 4, 2, 3)
    wout = jnp.concatenate([wo_re, wo_im], axis=2)
    win = jnp.stack([win[0], win[1, :, ::-1]])
    wout = jnp.stack([wout[0], wout[1, :, :, ::-1]])
    g = lr.shape[1]
    j = br.shape[-1]
    p = lr.shape[-1]
    rows = []
    for k in range(n_levels):
        ar, ai = apow(float(tc * 2 ** k))
        rows.append(jnp.concatenate([ar, ar], axis=-1))
        rows.append(jnp.concatenate([-ai, ai], axis=-1))
    apw = jnp.stack(rows, axis=2)
    return ((bbt.transpose(1, 0, 2, 3), cps.reshape(2, g, 2 * p, tc * j).transpose(1, 0, 2, 3)),
            win.reshape(2, g, tc * j, 2 * p).transpose(1, 0, 2, 3).astype(BF16),
            wout.reshape(2, g, 2 * p, tc * j).transpose(1, 0, 2, 3).astype(BF16),
            apw.transpose(1, 0, 2, 3))


def _s5_kernel(u_ref, bbt_ref, cps_ref, win_ref, wout_ref, apw_ref, dsk_ref, y_ref, *, nb, ctx_rows, n_levels, p):
    u = u_ref[0]
    n, lanes = u.shape
    jw = bbt_ref.shape[2]
    tc = lanes // jw
    row = lax.broadcasted_iota(jnp.int32, (n, 1), 0)
    lane = lax.broadcasted_iota(jnp.int32, (jw, lanes), 1)
    y = u.astype(F32) * dsk_ref[0]
    for d in range(2):
        kt = jnp.dot(bbt_ref[0, d], cps_ref[0, d], precision=lax.Precision.HIGHEST, preferred_element_type=F32)
        blocks = []
        for s in range(tc):
            sh = jw * s if d == 0 else jw * (tc - 1 - s)
            if sh == 0:
                blocks.append(kt)
            elif d == 0:
                blocks.append(jnp.where(lane >= sh, pltpu.roll(kt, sh, 1), 0.0))
            else:
                blocks.append(jnp.where(lane < lanes - sh, pltpu.roll(kt, lanes - sh, 1), 0.0))
        tmat = jnp.concatenate(blocks, axis=0).astype(BF16)
        y = y + jnp.dot(u, tmat, preferred_element_type=F32)
        x = jnp.dot(u, win_ref[0, d], preferred_element_type=F32)
        if d == 0:
            def shift(a, s):
                return jnp.where(row >= s, pltpu.roll(a, s, 0), 0.0)
        else:
            if ctx_rows:
                x = pltpu.roll(x, n - ctx_rows, 0)

            def shift(a, s):
                return jnp.where(row < n - s, pltpu.roll(a, n - s, 0), 0.0)
        x = shift(x, nb)
        for k in range(n_levels):
            sh = shift(x, nb * 2 ** k)
            a1 = apw_ref[0, d, 2 * k:2 * k + 1, :]
            a2 = apw_ref[0, d, 2 * k + 1:2 * k + 2, :]
            x = x + a1 * sh + a2 * pltpu.roll(sh, p, 1)
        if d == 1 and ctx_rows:
            x = pltpu.roll(x, ctx_rows, 0)
        y = y + jnp.dot(x.astype(BF16), wout_ref[0, d], preferred_element_type=F32)
    y_ref[0] = y.astype(y_ref.dtype)


def s5_scan(uc, toep, win, wout, apw, dsk, *, nb, ctx_rows, n_levels):
    g, n, lanes = uc.shape
    p2 = win.shape[-1]
    bbt, cps = toep
    jw = bbt.shape[2]
    return pl.pallas_call(
        functools.partial(_s5_kernel, nb=nb, ctx_rows=ctx_rows, n_levels=n_levels, p=p2 // 2),
        out_shape=jax.ShapeDtypeStruct((g, n, lanes), BF16),
        grid=(g,),
        in_specs=[
            pl.BlockSpec((1, n, lanes), lambda i: (i, 0, 0)),
            pl.BlockSpec((1, 2, jw, p2), lambda i: (i, 0, 0, 0)),
            pl.BlockSpec((1, 2, p2, lanes), lambda i: (i, 0, 0, 0)),
            pl.BlockSpec((1, 2, lanes, p2), lambda i: (i, 0, 0, 0)),
            pl.BlockSpec((1, 2, p2, lanes), lambda i: (i, 0, 0, 0)),
            pl.BlockSpec((1, 2, 2 * n_levels, p2), lambda i: (i, 0, 0, 0)),
            pl.BlockSpec((1, 1, lanes), lambda i: (i, 0, 0)),
        ],
        out_specs=pl.BlockSpec((1, n, lanes), lambda i: (i, 0, 0)),
        compiler_params=_cparams(1),
        name="s5_scan",
    )(uc, bbt, cps, win, wout, apw, dsk)


def s5_mix(u_rows, ops, d_skip, *, bsz, seq, ctx):
    toep, win, wout, apw = ops
    g = win.shape[0]
    lanes = win.shape[2]
    tc = S5_CHUNK
    j = lanes // tc
    hgt = seq // GRID_W
    ul = u_rows[:bsz * seq].reshape(bsz, hgt, GRID_W, g, j).transpose(0, 2, 1, 3, 4).reshape(bsz, seq, g, j)
    uc = u_rows[bsz * seq:].reshape(bsz, ctx, g, j)
    useq = jnp.concatenate([uc, ul], axis=1)
    nc = (seq + ctx) // tc
    uch = useq.reshape(bsz, nc, tc, g, j).transpose(3, 1, 0, 2, 4).reshape(g, nc * bsz, lanes)
    n_levels = apw.shape[2] // 2
    dsk = jnp.tile(d_skip.astype(F32).reshape(g, 1, j), (1, tc, 1)).reshape(g, 1, lanes)
    ych = s5_scan(uch.astype(BF16), toep, win, wout, apw, dsk, nb=bsz, ctx_rows=(ctx // tc) * bsz, n_levels=n_levels)
    yseq = ych.reshape(g, nc, bsz, tc, j).transpose(2, 1, 3, 0, 4).reshape(bsz, seq + ctx, g * j)
    yc = yseq[:, :ctx].reshape(bsz * ctx, g * j)
    yl = yseq[:, ctx:].reshape(bsz, GRID_W, hgt, g * j).transpose(0, 2, 1, 3).reshape(bsz * seq, g * j)
    return jnp.concatenate([yl, yc], axis=0)


def _softplus(x):
    return jnp.maximum(x, 0.0) + jnp.log1p(jnp.exp(-jnp.abs(x)))


def _dn_kernel(ql_ref, kl_ref, vl_ref, qc_ref, kc_ref, vc_ref, bal_ref, bac_ref, zl_ref, zc_ref,
               cw_ref, lp_ref, nw_ref, ol_ref, oc_ref,
               nt_s, w2t_s, qp_s, el_s, o_s, tok_s, *, n_heads, ctx):
    c = DN_CHUNK
    seq = ql_ref.shape[0]
    t = seq + ctx
    dk = ql_ref.shape[1]
    nch = t // c
    ncc = ctx // c
    h = pl.program_id(1)
    row = lax.broadcasted_iota(jnp.int32, (t, 1), 0)
    rowc = row % c
    lane = lax.broadcasted_iota(jnp.int32, (1, dk), 1)

    first = (row == 0) | (row == ctx)
    last = (row == ctx - 1) | (row == t - 1)

    def conv_silu(xc_ref, xl_ref, kind):
        x = jnp.concatenate([xc_ref[...], xl_ref[...]], axis=0).astype(F32)
        xp = jnp.where(first, 0.0, pltpu.roll(x, 1, 0))
        xn = jnp.where(last, 0.0, pltpu.roll(x, t - 1, 0))
        w = cw_ref[0, kind]
        return _silu(xp * w[0:1] + x * w[1:2] + xn * w[2:3])

    def l2n(x):
        return x * lax.rsqrt(jnp.sum(x * x, axis=-1, keepdims=True) + EPS)

    q = l2n(conv_silu(qc_ref, ql_ref, 0)) * (dk ** -0.5)
    k = l2n(conv_silu(kc_ref, kl_ref, 1))
    v = conv_silu(vc_ref, vl_ref, 2)

    @pl.when(h == 0)
    def _():
        ba = jnp.concatenate([bac_ref[...], bal_ref[...]], axis=0)
        g_all = -lp_ref[0:1, :] * _softplus(ba + lp_ref[1:2, :])
        pf, sf = g_all, g_all
        s = 1
        while s < c:
            pf = pf + jnp.where(rowc >= s, pltpu.roll(pf, s, 0), 0.0)
            sf = sf + jnp.where(rowc < c - s, pltpu.roll(sf, t - s, 0), 0.0)
            s *= 2
        tok_s[0] = jax.nn.sigmoid(ba)
        tok_s[1] = pf
        tok_s[2] = sf

    beta_all, pf, sf = tok_s[0], tok_s[1], tok_s[2]

    def col(a, idx):
        return jnp.sum(jnp.where(lane == idx, a, 0.0), axis=1, keepdims=True)

    causal_f = (lax.broadcasted_iota(jnp.int32, (c, c), 0) >= lax.broadcasted_iota(jnp.int32, (c, c), 1))[None]
    strict_f = (lax.broadcasted_iota(jnp.int32, (c, c), 0) > lax.broadcasted_iota(jnp.int32, (c, c), 1))[None]
    causal_b = (lax.broadcasted_iota(jnp.int32, (c, c), 0) <= lax.broadcasted_iota(jnp.int32, (c, c), 1))[None]
    strict_b = (lax.broadcasted_iota(jnp.int32, (c, c), 0) < lax.broadcasted_iota(jnp.int32, (c, c), 1))[None]
    eye = (lax.broadcasted_iota(jnp.int32, (c, c), 0) == lax.broadcasted_iota(jnp.int32, (c, c), 1))[None].astype(F32)

    sub_blk = (lax.broadcasted_iota(jnp.int32, (c, c), 0) // DN_SUB
               == lax.broadcasted_iota(jnp.int32, (c, c), 1) // DN_SUB)[None]

    def neg_pow_inverse(x, m, limit):
        pinv = eye + x
        while m < limit:
            xb = x.astype(BF16)
            x = jnp.einsum('cij,cjk->cik', xb, xb, preferred_element_type=F32)
            pinv = pinv + jnp.einsum('cij,cjk->cik', pinv.astype(BF16), x.astype(BF16), preferred_element_type=F32)
            m *= 2
        return pinv

    q3 = q.reshape(nch, c, dk)
    k3 = k.reshape(nch, c, dk)
    v3 = v.reshape(nch, c, dk)
    k3b = k3.astype(BF16)
    q3b = q3.astype(BF16)
    for d in range(2):
        causal, strict = (causal_f, strict_f) if d == 0 else (causal_b, strict_b)
        beta = col(beta_all, d * n_heads + h)
        gc = col(pf if d == 0 else sf, (2 + d) * n_heads + h)
        hi = gc.astype(BF16).astype(F32)
        mid = (gc - hi).astype(BF16).astype(F32)
        lo = gc - hi - mid
        pieces = (hi, mid, lo)
        g1 = jnp.zeros((t, dk), F32)
        g2 = jnp.zeros((t, dk), F32)
        for n_p, piece in enumerate(pieces):
            pb = jnp.broadcast_to(piece, (t, dk))
            g1 = jnp.where(lane == n_p, pb, jnp.where(lane == 3 + n_p, 1.0, g1))
            g2 = jnp.where(lane == n_p, 1.0, jnp.where(lane == 3 + n_p, -pb, g2))
        ldiff = jnp.einsum('cid,cjd->cij', g1.astype(BF16).reshape(nch, c, dk), g2.astype(BF16).reshape(nch, c, dk),
                           preferred_element_type=F32)
        decay = jnp.where(causal, jnp.exp(jnp.where(causal, ldiff, 0.0)), 0.0)
        beta3 = beta.reshape(nch, c, 1)
        gc3 = gc.reshape(nch, c, 1)
        glast3 = gc3[:, c - 1:c, :] if d == 0 else gc3[:, 0:1, :]
        kb = k3 * beta3
        a = jnp.einsum('cid,cjd->cij', kb.astype(BF16), k3b, preferred_element_type=F32)
        a = jnp.where(strict, a * decay, 0.0)
        qk = jnp.einsum('cid,cjd->cij', q3b, k3b, preferred_element_type=F32)
        qk = jnp.where(causal, qk * decay, 0.0)
        a_diag = jnp.where(sub_blk, a, 0.0)
        dinv = neg_pow_inverse(-a_diag, 2, DN_SUB)
        n_off = jnp.einsum('cij,cjk->cik', dinv.astype(BF16), (a - a_diag).astype(BF16), preferred_element_type=F32)
        pinv = neg_pow_inverse(-n_off, 2 * DN_SUB, c)
        pinv = jnp.einsum('cij,cjk->cik', pinv.astype(BF16), dinv.astype(BF16), preferred_element_type=F32)
        rhs = jnp.concatenate([v3 * beta3, kb * jnp.exp(gc3)], axis=-1)
        sol = jnp.einsum('cij,cjd->cid', pinv.astype(BF16), rhs.astype(BF16), preferred_element_type=F32)
        solb = sol.astype(BF16)
        qs = jnp.einsum('cij,cjd->cid', qk.astype(BF16), solb, preferred_element_type=F32)
        o_s[d] = qs[:, :, :dk].reshape(t, dk)
        qp_s[d] = (q3 * jnp.exp(gc3) - qs[:, :, dk:]).reshape(t, dk).astype(BF16)
        ke = (k3 * jnp.exp(glast3 - gc3)).astype(BF16)
        solt = jnp.swapaxes(sol, 1, 2).astype(BF16)
        nw2 = jnp.einsum('cdi,cik->cdk', solt, ke, preferred_element_type=F32)
        nt_s[d] = nw2[:, :dk, :]
        w2t_s[d] = nw2[:, dk:, :].astype(BF16)
        el_s[d] = jnp.broadcast_to(jnp.exp(glast3), (nch, 8, dk))

    def chunk_step(d, ci, st):
        r0 = pl.multiple_of(ci * c, c)
        stb = st.astype(BF16)
        o_s[d, pl.ds(r0, c), :] += lax.dot_general(qp_s[d, pl.ds(r0, c), :], stb, (((1,), (1,)), ((), ())),
                                                   preferred_element_type=F32)
        return (st * el_s[d, ci][0:1, :] + nt_s[d, ci]
                - jnp.dot(stb, w2t_s[d, ci], preferred_element_type=F32))

    def ctx_body(n, carry):
        return chunk_step(0, n, carry[0]), chunk_step(1, ncc - 1 - n, carry[1])

    def lat_body(n, carry):
        return chunk_step(0, ncc + n, carry[0]), chunk_step(1, nch - 1 - n, carry[1])

    zero = jnp.zeros((dk, dk), F32)
    carry = lax.fori_loop(0, ncc, ctx_body, (zero, zero))
    lax.fori_loop(0, nch - ncc, lat_body, carry)

    o = o_s[0] + o_s[1]
    on = o * lax.rsqrt(jnp.mean(o * o, axis=-1, keepdims=True) + EPS) * nw_ref[...]
    z = jnp.concatenate([zc_ref[...], zl_ref[...]], axis=0).astype(F32)
    out = (on * _silu(z)).astype(ol_ref.dtype)
    oc_ref[...] = out[:ctx]
    ol_ref[...] = out[ctx:]


def deltanet_mix(p_main, p_ba, conv_w, a_log, dt_bias, norm_w, *, bsz, seq, ctx, u_width):
    n_heads = a_log.shape[-1]
    dk = norm_w.shape[-1]
    c = DN_CHUNK
    t = seq + ctx
    nch = t // c
    assert seq % c == 0 and ctx % c == 0 and u_width % dk == 0 and 4 * n_heads <= p_ba.shape[1]
    cb = u_width // dk
    lat_rows = bsz * seq
    cw = conv_w.astype(F32).reshape(conv_w.shape[0], 3, n_heads, dk).transpose(2, 1, 0, 3)
    lanes = p_ba.shape[1]
    lp = jnp.zeros((2, lanes), F32)
    lp = lp.at[0, 2 * n_heads:4 * n_heads].set(jnp.exp(a_log.astype(F32)).reshape(-1))
    lp = lp.at[1, 2 * n_heads:4 * n_heads].set(dt_bias.astype(F32).reshape(-1))
    cblk = lat_rows // ctx

    def lat_spec(off):
        return pl.BlockSpec((seq, dk), lambda b, h: (b, off + h))

    def ctx_spec(off):
        return pl.BlockSpec((ctx, dk), lambda b, h: (cblk + b, off + h))

    yl, yc = pl.pallas_call(
        functools.partial(_dn_kernel, n_heads=n_heads, ctx=ctx),
        out_shape=(jax.ShapeDtypeStruct((lat_rows, n_heads * dk), BF16),
                   jax.ShapeDtypeStruct((bsz * ctx, n_heads * dk), BF16)),
        grid=(bsz, n_heads),
        in_specs=[
            lat_spec(cb), lat_spec(cb + n_heads), lat_spec(cb + 2 * n_heads),
            ctx_spec(cb), ctx_spec(cb + n_heads), ctx_spec(cb + 2 * n_heads),
            pl.BlockSpec((seq, lanes), lambda b, h: (b, 0)),
            pl.BlockSpec((ctx, lanes), lambda b, h: (cblk + b, 0)),
            lat_spec(cb + 3 * n_heads), ctx_spec(cb + 3 * n_heads),
            pl.BlockSpec((1, 3, conv_w.shape[0], dk), lambda b, h: (h, 0, 0, 0)),
            pl.BlockSpec((2, lanes), lambda b, h: (0, 0)),
            pl.BlockSpec((1, dk), lambda b, h: (0, 0)),
        ],
        out_specs=(pl.BlockSpec((seq, dk), lambda b, h: (b, h)),
                   pl.BlockSpec((ctx, dk), lambda b, h: (b, h))),
        scratch_shapes=[
            pltpu.VMEM((2, nch, dk, dk), F32),
            pltpu.VMEM((2, nch, dk, dk), BF16),
            pltpu.VMEM((2, t, dk), BF16),
            pltpu.VMEM((2, nch, 8, dk), F32),
            pltpu.VMEM((2, t, dk), F32),
            pltpu.VMEM((3, t, lanes), F32),
        ],
        compiler_params=_cparams(2),
        name="deltanet",
    )(p_main, p_main, p_main, p_main, p_main, p_main, p_ba, p_ba, p_main, p_main,
      cw, lp, norm_w.astype(F32).reshape(1, dk))
    return jnp.concatenate([yl, yc], axis=0)


def _row_gather(table_hbm, idx_ref, buf, sem, n_rows):
    def copy(r, src_row):
        return pltpu.make_async_copy(table_hbm.at[pl.ds(src_row, 1)], buf.at[pl.ds(r, 1)], sem)

    def start():
        def body(r, carry):
            copy(r, idx_ref[0, 0, r]).start()
            return carry
        lax.fori_loop(0, n_rows, body, 0, unroll=8)

    def wait():
        def body(r, carry):
            copy(r, 0).wait()
            return carry
        lax.fori_loop(0, n_rows, body, 0, unroll=8)

    return start, wait


def _experts_kernel(be_ref, nv_ref, tok_ref, tokn_ref, h_hbm, wg_ref, wu_ref, wd_ref, o_ref,
                    wgb, wub, wdb, xbuf, sem):
    i = pl.program_id(0)
    nv = nv_ref[0]
    mb = xbuf.shape[1]
    slot = i % 2
    start_first, _ = _row_gather(h_hbm, tok_ref, xbuf.at[0], sem.at[0], mb)
    start_next, _ = _row_gather(h_hbm, tokn_ref, xbuf.at[1 - slot], sem.at[1 - slot], mb)
    _, wait_cur = _row_gather(h_hbm, tok_ref, xbuf.at[slot], sem.at[slot], mb)

    @pl.when((i == 0) & (nv > 0))
    def _():
        start_first()

    @pl.when(i + 1 < nv)
    def _():
        start_next()

    prev = be_ref[jnp.maximum(i - 1, 0)]

    @pl.when((i == 0) | (be_ref[i] != prev))
    def _():
        wgb[...] = wg_ref[0].astype(BF16)
        wub[...] = wu_ref[0].astype(BF16)
        wdb[...] = wd_ref[0].astype(BF16)

    @pl.when(i < nv)
    def _():
        wait_cur()
        x = xbuf[slot].astype(BF16)
        g = jnp.dot(x, wgb[...], preferred_element_type=F32)
        u = jnp.dot(x, wub[...], preferred_element_type=F32)
        a = (_silu(g) * u).astype(BF16)
        o_ref[...] = jnp.dot(a, wdb[...], preferred_element_type=F32).astype(o_ref.dtype)

    @pl.when(i >= nv)
    def _():
        o_ref[...] = jnp.zeros_like(o_ref)


def routed_experts(h, tok, block_e, n_valid, w_gate, w_up, w_down):
    d = h.shape[1]
    f = w_gate.shape[-1]
    mb = MOE_BLOCK
    n_blocks = tok.shape[0] // mb
    w_gate, w_up = w_gate.reshape(-1, d, f), w_up.reshape(-1, d, f)
    w_down = w_down.reshape(-1, f, d)
    tok3 = tok.reshape(n_blocks, 1, mb)
    return pl.pallas_call(
        _experts_kernel,
        out_shape=jax.ShapeDtypeStruct((n_blocks * mb, d), F32),
        grid_spec=pltpu.PrefetchScalarGridSpec(
            num_scalar_prefetch=2,
            grid=(n_blocks,),
            in_specs=[
                pl.BlockSpec((1, 1, mb), lambda i, be, nv: (i, 0, 0), memory_space=pltpu.SMEM),
                pl.BlockSpec((1, 1, mb), lambda i, be, nv: (jnp.minimum(i + 1, n_blocks - 1), 0, 0),
                             memory_space=pltpu.SMEM),
                pl.BlockSpec(memory_space=pl.ANY),
                pl.BlockSpec((1, d, f), lambda i, be, nv: (be[i], 0, 0)),
                pl.BlockSpec((1, d, f), lambda i, be, nv: (be[i], 0, 0)),
                pl.BlockSpec((1, f, d), lambda i, be, nv: (be[i], 0, 0)),
            ],
            out_specs=pl.BlockSpec((mb, d), lambda i, be, nv: (i, 0)),
            scratch_shapes=[pltpu.VMEM((d, f), BF16), pltpu.VMEM((d, f), BF16), pltpu.VMEM((f, d), BF16),
                            pltpu.VMEM((2, mb, d), F32), pltpu.SemaphoreType.DMA((2,))],
        ),
        compiler_params=_cparams(1),
        name="routed_experts",
    )(block_e, n_valid, tok3, tok3, h, w_gate, w_up, w_down)


def _swiglu_kernel(x_ref, wg_ref, wu_ref, wd_ref, o_ref, wgb, wub, wdb):
    @pl.when(pl.program_id(0) == 0)
    def _():
        wgb[...] = wg_ref[...].astype(BF16)
        wub[...] = wu_ref[...].astype(BF16)
        wdb[...] = wd_ref[...].astype(BF16)

    x = x_ref[...].astype(BF16)
    g = jnp.dot(x, wgb[...], preferred_element_type=F32)
    u = jnp.dot(x, wub[...], preferred_element_type=F32)
    a = (_silu(g) * u).astype(BF16)
    o_ref[...] = jnp.dot(a, wdb[...], preferred_element_type=F32).astype(o_ref.dtype)


def shared_expert(x, wg, wu, wd, *, layer, rows, tm=None):
    d = x.shape[1]
    f = wg.shape[-1]
    tm = _row_tile(rows, tm)
    const = lambda i: (0, 0)
    return pl.pallas_call(
        _swiglu_kernel,
        out_shape=jax.ShapeDtypeStruct((rows, d), F32),
        grid=(rows // tm,),
        in_specs=[
            pl.BlockSpec((tm, d), lambda i: (i, 0)),
            _wspec(wg, layer, (d, f), const, pipeline_mode=pl.Buffered(1)),
            _wspec(wu, layer, (d, f), const, pipeline_mode=pl.Buffered(1)),
            _wspec(wd, layer, (f, d), const, pipeline_mode=pl.Buffered(1)),
        ],
        out_specs=pl.BlockSpec((tm, d), lambda i: (i, 0)),
        scratch_shapes=[pltpu.VMEM((d, f), BF16), pltpu.VMEM((d, f), BF16), pltpu.VMEM((f, d), BF16)],
        compiler_params=_cparams(1),
        name="shared_expert",
    )(x, wg, wu, wd)


def _ffn_res_kernel(dst_ref, dstn_ref, y_hbm, w_ref, s_ref, x_ref, nw_ref, g_ref, o_ref, buf, sem, *, top_k):
    i = pl.program_id(0)
    tm = x_ref.shape[0]
    nr = top_k * tm
    slot = i % 2
    start_first, _ = _row_gather(y_hbm, dst_ref, buf.at[0], sem.at[0], nr)
    start_next, _ = _row_gather(y_hbm, dstn_ref, buf.at[1 - slot], sem.at[1 - slot], nr)
    _, wait_cur = _row_gather(y_hbm, dst_ref, buf.at[slot], sem.at[slot], nr)

    @pl.when(i == 0)
    def _():
        start_first()

    @pl.when(i + 1 < pl.num_programs(0))
    def _():
        start_next()

    wait_cur()
    w = w_ref[...]
    f = s_ref[...]
    for k in range(top_k):
        f = f + buf[slot, k * tm:(k + 1) * tm, :] * w[:, k:k + 1]
    fn = f * lax.rsqrt(jnp.mean(f * f, axis=-1, keepdims=True) + EPS) * nw_ref[...]
    o_ref[...] = x_ref[...] + g_ref[0] * fn


def ffn_residual(y, dest, wts, shared, x, nw, gate, set_of_tile, *, rows, tm=128):
    d = x.shape[1]
    top_k = wts.shape[1]
    tm = _row_tile(rows, tm)
    n_tiles = rows // tm
    dst3 = dest.reshape(top_k, n_tiles, tm).transpose(1, 0, 2).reshape(n_tiles, 1, top_k * tm)
    return pl.pallas_call(
        functools.partial(_ffn_res_kernel, top_k=top_k),
        out_shape=jax.ShapeDtypeStruct((rows, d), F32),
        grid=(n_tiles,),
        in_specs=[
            pl.BlockSpec((1, 1, top_k * tm), lambda i: (i, 0, 0), memory_space=pltpu.SMEM),
            pl.BlockSpec((1, 1, top_k * tm), lambda i: (jnp.minimum(i + 1, n_tiles - 1), 0, 0),
                         memory_space=pltpu.SMEM),
            pl.BlockSpec(memory_space=pl.ANY),
            pl.BlockSpec((tm, top_k), lambda i: (i, 0)),
            pl.BlockSpec((tm, d), lambda i: (i, 0)),
            pl.BlockSpec((tm, d), lambda i: (i, 0)),
            pl.BlockSpec((1, d), lambda i: (0, 0)),
            pl.BlockSpec((1, 1, d), lambda i: (set_of_tile(i, tm), 0, 0)),
        ],
        out_specs=pl.BlockSpec((tm, d), lambda i: (i, 0)),
        scratch_shapes=[pltpu.VMEM((2, top_k * tm, d), F32), pltpu.SemaphoreType.DMA((2,))],
        compiler_params=_cparams(1),
        name="ffn_residual",
    )(dst3, dst3, y, wts, shared, x, nw.reshape(1, d), gate)


def _route_kernel(h_ref, wr_ref, b_ref, eidx_ref, wts_ref, rank_ref, cnt_ref, wrb, tri, cnt_s, *,
                  n_groups, topk_groups, top_k):
    n_exp, tm = wr_ref.shape[0], h_ref.shape[0]
    gs = n_exp // n_groups
    ninf = -jnp.inf

    @pl.when(pl.program_id(0) == 0)
    def _():
        wrb[...] = wr_ref[...].astype(BF16)
        tri[...] = (lax.broadcasted_iota(jnp.int32, (tm, tm), 0)
                    < lax.broadcasted_iota(jnp.int32, (tm, tm), 1)).astype(BF16)
        cnt_s[...] = jnp.zeros_like(cnt_s)

    logits = lax.dot_general(wrb[...], h_ref[...].astype(BF16), (((1,), (1,)), ((), ())),
                             preferred_element_type=F32)
    scores = jax.nn.sigmoid(logits)
    sel = scores + b_ref[:, 0:1]
    s3 = sel.reshape(n_groups, gs, tm)
    io3 = lax.broadcasted_iota(jnp.int32, (n_groups, gs, tm), 1)
    m1 = jnp.max(s3, axis=1, keepdims=True)
    i1 = jnp.min(jnp.where(s3 == m1, io3, gs), axis=1, keepdims=True)
    m2 = jnp.max(jnp.where(io3 == i1, ninf, s3), axis=1, keepdims=True)
    gscore = (m1 + m2).reshape(n_groups, tm)
    iog = lax.broadcasted_iota(jnp.int32, (n_groups, tm), 0)
    gsel = jnp.zeros((n_groups, tm), jnp.bool_)
    for _ in range(topk_groups):
        gm = jnp.max(gscore, axis=0, keepdims=True)
        gi = jnp.min(jnp.where(gscore == gm, iog, n_groups), axis=0, keepdims=True)
        hit = iog == gi
        gsel = gsel | hit
        gscore = jnp.where(hit, ninf, gscore)
    x = jnp.where(gsel.reshape(n_groups, 1, tm), s3, ninf).reshape(n_exp, tm)
    ioe = lax.broadcasted_iota(jnp.int32, (n_exp, tm), 0)
    hits = []
    chosen = jnp.zeros((n_exp, tm), jnp.bool_)
    for k in range(top_k):
        m = jnp.max(x, axis=0, keepdims=True)
        idx = jnp.min(jnp.where(x == m, ioe, n_exp), axis=0, keepdims=True)
        hit = ioe == idx
        x = jnp.where(hit, ninf, x)
        chosen = chosen | hit
        hits.append(hit)
        eidx_ref[k:k + 1, :] = idx
    wsel = jnp.where(chosen, scores, 0.0)
    wd = wsel / jnp.sum(wsel, axis=0, keepdims=True) * ROUTE_SCALE
    cm = jnp.where(chosen, 1.0, 0.0)
    rank = jnp.dot(cm.astype(BF16), tri[...], preferred_element_type=F32) + cnt_s[:, 0:1]
    for k in range(top_k):
        wts_ref[k:k + 1, :] = jnp.sum(jnp.where(hits[k], wd, 0.0), axis=0, keepdims=True)
        rank_ref[k:k + 1, :] = jnp.sum(jnp.where(hits[k], rank, 0.0), axis=0, keepdims=True).astype(jnp.int32)
    cnt_s[...] = cnt_s[...] + jnp.sum(cm, axis=1, keepdims=True)
    cnt_ref[...] = cnt_s[...].astype(jnp.int32)


def route(h, w_router, e_bias, *, rows, tm=None):
    d = h.shape[1]
    n_exp = w_router.shape[1]
    tm = _row_tile(rows, tm)
    bias = jnp.broadcast_to(e_bias.astype(F32).reshape(n_exp, 1), (n_exp, LANES))
    kt = lambda i: (0, i)
    eidx, wts, rank, cnt = pl.pallas_call(
        functools.partial(_route_kernel, n_groups=N_GROUPS, topk_groups=TOPK_GROUPS, top_k=TOP_K),
        out_shape=(jax.ShapeDtypeStruct((TOP_K, rows), jnp.int32), jax.ShapeDtypeStruct((TOP_K, rows), F32),
                   jax.ShapeDtypeStruct((TOP_K, rows), jnp.int32), jax.ShapeDtypeStruct((n_exp, LANES), jnp.int32)),
        grid=(rows // tm,),
        in_specs=[
            pl.BlockSpec((tm, d), lambda i: (i, 0)),
            pl.BlockSpec((n_exp, d), lambda i: (0, 0)),
            pl.BlockSpec((n_exp, LANES), lambda i: (0, 0)),
        ],
        out_specs=(pl.BlockSpec((TOP_K, tm), kt), pl.BlockSpec((TOP_K, tm), kt), pl.BlockSpec((TOP_K, tm), kt),
                   pl.BlockSpec((n_exp, LANES), lambda i: (0, 0))),
        scratch_shapes=[pltpu.VMEM((n_exp, d), BF16), pltpu.VMEM((tm, tm), BF16), pltpu.VMEM((n_exp, LANES), F32)],
        compiler_params=_cparams(1),
        name="route",
    )(h, w_router.T, bias)
    return eidx, wts, rank, cnt[:, 0]


def moe_ffn(h, w_router, e_bias, w_gate, w_up, w_down, ws_gate, ws_up, ws_down, *, layer, rows):
    n_exp = w_router.shape[1]
    eidx, wts, rank, counts = route(h, w_router, e_bias, rows=rows)
    mb = MOE_BLOCK
    padded = (counts + mb - 1) // mb * mb
    pad_end = jnp.cumsum(padded)
    pad_start = pad_end - padded
    dest = pad_start[eidx] + rank
    n_blocks = (rows * TOP_K + n_exp * (mb - 1) + mb - 1) // mb
    tok = jnp.zeros((n_blocks * mb,), jnp.int32).at[dest.reshape(-1)].set(
        jnp.tile(jnp.arange(rows, dtype=jnp.int32), TOP_K))
    n_valid = (pad_end[-1] // mb).astype(jnp.int32)
    starts = jnp.arange(n_blocks, dtype=jnp.int32) * mb
    block_e = jnp.minimum(jnp.sum(starts[:, None] >= pad_end[None, :], axis=1), n_exp - 1).astype(jnp.int32)
    last_e = block_e[jnp.maximum(n_valid - 1, 0)]
    block_e = jnp.where(jnp.arange(n_blocks) < n_valid, block_e, last_e)
    y = routed_experts(h, tok, block_e + layer * n_exp, n_valid.reshape(1), w_gate, w_up, w_down)
    shared = shared_expert(h, ws_gate, ws_up, ws_down, layer=layer, rows=rows)
    return y, dest, wts.T, shared


def kernel(x, c, ctx, c_ctx, w_mod, b_mod, norm_mix_pre, norm_mix_post, norm_ffn_pre, norm_ffn_post,
           w_in, s5_lam_re, s5_lam_im, s5_log_step, s5_b_re, s5_b_im, s5_c_re, s5_c_im, s5_d, s5_w_glu,
           dn_conv, dn_a_log, dn_dt_bias, dn_norm, w_br_s5, w_br_dn, w_out,
           moe_router, moe_bias, moe_w_gate, moe_w_up, moe_w_down, sh_w_gate, sh_w_up, sh_w_down):
    bsz, seq, d = x.shape
    n_ctx = ctx.shape[1]
    depth = w_mod.shape[0]
    lat_rows, ctx_rows = bsz * seq, bsz * n_ctx
    all_rows = lat_rows + ctx_rows
    s5_width = s5_d.shape[1]
    dn_width = w_br_dn.shape[1]
    n_heads = dn_a_log.shape[-1]
    main_cols = s5_width + 4 * dn_width
    ba_cols = 4 * n_heads
    lanes = 128
    nc = (seq + n_ctx) // S5_CHUNK
    n_levels = max(1, (nc - 1).bit_length())

    def set_of_tile(i, tm):
        return jnp.minimum((i * tm) // seq, bsz)

    xs = jnp.concatenate([x.reshape(lat_rows, d), ctx.reshape(ctx_rows, d)], axis=0)
    n_sets = 8
    cin = jnp.zeros((n_sets, d), F32).at[:bsz].set(_silu(c)).at[bsz].set(_silu(c_ctx))
    for i in range(depth):
        last = i == depth - 1
        rows = lat_rows if last else all_rows
        mods = matmul(cin, w_mod, b_mod[i].reshape(1, -1), layer=i, tm=n_sets, name="mods").reshape(n_sets, 6, 1, d)
        mod = [mods[:, k] for k in range(6)]
        hmix = prenorm(xs, norm_mix_pre[i], mod[0], mod[1], set_of_tile)
        p_main = matmul(hmix, w_in, layer=i, n_cols=main_cols, out_dtype=BF16, name="in_proj")
        w_tail = w_in[i, :, main_cols:]
        p_ba = matmul(hmix, jnp.pad(w_tail[:, :ba_cols], ((0, 0), (0, lanes - ba_cols))), tn=lanes, name="in_proj_ba")
        gates = matmul(hmix, w_tail[:, ba_cols:], rows=rows, out_dtype=BF16, name="in_proj_gates")
        ops = s5_operators(s5_lam_re[i], s5_lam_im[i], s5_log_step[i], s5_b_re[i], s5_b_im[i],
                           s5_c_re[i], s5_c_im[i], n_levels)
        y_s5 = s5_mix(p_main[:, :s5_width], ops, s5_d[i], bsz=bsz, seq=seq, ctx=n_ctx)
        y_s5 = s5_glu(y_s5, s5_w_glu, layer=i, rows=rows)
        y_dn = deltanet_mix(p_main, p_ba, dn_conv[i], dn_a_log[i], dn_dt_bias[i], dn_norm[i],
                            bsz=bsz, seq=seq, ctx=n_ctx, u_width=s5_width)
        m = branch_merge(y_s5, y_dn, gates, w_br_s5, w_br_dn, layer=i, rows=rows)
        xs = outproj_residual(m, w_out, xs, norm_mix_post[i], mod[2], set_of_tile, layer=i, rows=rows)
        hffn = prenorm(xs, norm_ffn_pre[i], mod[3], mod[4], set_of_tile, rows=rows, out_dtype=F32)
        y, dest, wts, shared = moe_ffn(hffn, moe_router[i], moe_bias[i], moe_w_gate, moe_w_up, moe_w_down,
                                       sh_w_gate, sh_w_up, sh_w_down, layer=i, rows=rows)
        xs = ffn_residual(y, dest, wts, shared, xs, norm_ffn_post[i], mod[5], set_of_tile, rows=rows)
    return xs[:lat_rows].reshape(bsz, seq, d)
```

```python
import functools

import jax
import jax.numpy as jnp
from jax import lax
from jax.experimental import pallas as pl
from jax.experimental.pallas import tpu as pltpu

F32 = jnp.float32
BF16 = jnp.bfloat16

EPS = 1e-6
GRID_W = 64
S5_CHUNK = 16
DN_CHUNK = 64
DN_SUB = 16
TOP_K = 8
N_GROUPS = 8
TOPK_GROUPS = 4
ROUTE_SCALE = 2.5
MOE_BLOCK = 256

V7X_VMEM_LIMIT = 56 * 1024 * 1024
LANES = 128
ROW_TILE = 1024


def _cparams(n_axes, vmem=V7X_VMEM_LIMIT):
    return pltpu.CompilerParams(dimension_semantics=("arbitrary",) * n_axes, vmem_limit_bytes=vmem)


def _silu(x):
    return x * jax.nn.sigmoid(x)


def _row_tile(rows, tm=None):
    tm = min(tm or ROW_TILE, ROW_TILE, rows)
    assert rows % tm == 0, (rows, tm)
    return tm


def _col_tile(n, tn, col0=0):
    tn = min(tn, n)
    while n % tn or col0 % tn:
        tn -= LANES
    return tn


def _wspec(w, layer, block, index_map, **kw):
    if w.ndim == len(block):
        return pl.BlockSpec(block, index_map, **kw)
    return pl.BlockSpec((None,) + tuple(block), lambda *a: (layer,) + tuple(index_map(*a)), **kw)


def _mm_kernel(x_ref, w_ref, b_ref, o_ref, wbf_ref):
    @pl.when(pl.program_id(1) == 0)
    def _():
        wbf_ref[...] = w_ref[...].astype(BF16)

    acc = jnp.dot(x_ref[...].astype(BF16), wbf_ref[...], preferred_element_type=F32)
    o_ref[...] = (acc + b_ref[...]).astype(o_ref.dtype)


def matmul(x, w, bias=None, *, layer=0, n_cols=None, col0=0, out_dtype=F32, tm=None, tn=1024, rows=None,
           name="matmul"):
    m, k = x.shape
    rows = m if rows is None else rows
    n_cols = w.shape[-1] - col0 if n_cols is None else n_cols
    tm = _row_tile(rows, tm)
    tn = _col_tile(n_cols, tn, col0)
    assert rows % tm == 0 and n_cols % tn == 0 and col0 % tn == 0, (rows, tm, n_cols, tn, col0)
    if bias is None:
        bias = jnp.zeros((1, n_cols), F32)
    cb0 = col0 // tn
    return pl.pallas_call(
        _mm_kernel,
        out_shape=jax.ShapeDtypeStruct((rows, n_cols), out_dtype),
        grid=(n_cols // tn, rows // tm),
        in_specs=[
            pl.BlockSpec((tm, k), lambda j, i: (i, 0)),
            _wspec(w, layer, (k, tn), lambda j, i: (0, j + cb0)),
            pl.BlockSpec((1, tn), lambda j, i: (0, j)),
        ],
        out_specs=pl.BlockSpec((tm, tn), lambda j, i: (i, j)),
        scratch_shapes=[pltpu.VMEM((k, tn), BF16)],
        compiler_params=_cparams(2),
        name=name,
    )(x, w, bias)


def _prenorm_kernel(x_ref, w_ref, sh_ref, sc_ref, o_ref):
    x = x_ref[...]
    y = x * lax.rsqrt(jnp.mean(x * x, axis=-1, keepdims=True) + EPS) * w_ref[...]
    o_ref[...] = (y * (1.0 + sc_ref[0]) + sh_ref[0]).astype(o_ref.dtype)


def prenorm(x, w, shift, scale, set_of_tile, *, rows=None, tm=None, out_dtype=BF16):
    m, d = x.shape
    rows = m if rows is None else rows
    tm = _row_tile(rows, tm)
    return pl.pallas_call(
        _prenorm_kernel,
        out_shape=jax.ShapeDtypeStruct((rows, d), out_dtype),
        grid=(rows // tm,),
        in_specs=[
            pl.BlockSpec((tm, d), lambda i: (i, 0)),
            pl.BlockSpec((1, d), lambda i: (0, 0)),
            pl.BlockSpec((1, 1, d), lambda i: (set_of_tile(i, tm), 0, 0)),
            pl.BlockSpec((1, 1, d), lambda i: (set_of_tile(i, tm), 0, 0)),
        ],
        out_specs=pl.BlockSpec((tm, d), lambda i: (i, 0)),
        compiler_params=_cparams(1),
        name="prenorm",
    )(x, w.reshape(1, d), shift, scale)


def _glu_kernel(y_ref, w_ref, o_ref, wbf_ref, *, tn):
    @pl.when(pl.program_id(1) == 0)
    def _():
        wbf_ref[...] = w_ref[...].astype(BF16)

    j = pl.program_id(0)
    g = jax.nn.gelu(y_ref[...].astype(F32))
    acc = jnp.dot(g.astype(BF16), wbf_ref[...], preferred_element_type=F32)
    gj = jax.nn.gelu(y_ref[:, pl.ds(pl.multiple_of(j * tn, tn), tn)].astype(F32))
    o_ref[...] = (gj * jax.nn.sigmoid(acc)).astype(o_ref.dtype)


def s5_glu(y, w, *, layer, rows, tm=None, tn=512):
    m, k = y.shape
    tm = _row_tile(rows, tm)
    tn = _col_tile(k, tn)
    return pl.pallas_call(
        functools.partial(_glu_kernel, tn=tn),
        out_shape=jax.ShapeDtypeStruct((rows, k), BF16),
        grid=(k // tn, rows // tm),
        in_specs=[
            pl.BlockSpec((tm, k), lambda j, i: (i, 0)),
            _wspec(w, layer, (k, tn), lambda j, i: (0, j)),
        ],
        out_specs=pl.BlockSpec((tm, tn), lambda j, i: (i, j)),
        scratch_shapes=[pltpu.VMEM((k, tn), BF16)],
        compiler_params=_cparams(2),
        name="s5_glu",
    )(y, w)


def _merge_kernel(a_ref, b_ref, ga_ref, gb_ref, wa_ref, wb_ref, o_ref, wabf_ref, wbbf_ref):
    @pl.when(pl.program_id(1) == 0)
    def _():
        wabf_ref[...] = wa_ref[...].astype(BF16)
        wbbf_ref[...] = wb_ref[...].astype(BF16)

    ya = jnp.dot(a_ref[...], wabf_ref[...], preferred_element_type=F32)
    yb = jnp.dot(b_ref[...], wbbf_ref[...], preferred_element_type=F32)
    m = jax.nn.sigmoid(ga_ref[...].astype(F32)) * ya + jax.nn.sigmoid(gb_ref[...].astype(F32)) * yb
    o_ref[...] = m.astype(o_ref.dtype)


def branch_merge(ya, yb, gates, wa, wb, *, layer, rows, tm=None, tn=512):
    ka, kb = ya.shape[1], yb.shape[1]
    d = wa.shape[-1]
    tm = _row_tile(rows, tm)
    tn = _col_tile(d, tn)
    nb = d // tn
    return pl.pallas_call(
        _merge_kernel,
        out_shape=jax.ShapeDtypeStruct((rows, d), BF16),
        grid=(nb, rows // tm),
        in_specs=[
            pl.BlockSpec((tm, ka), lambda j, i: (i, 0)),
            pl.BlockSpec((tm, kb), lambda j, i: (i, 0)),
            pl.BlockSpec((tm, tn), lambda j, i: (i, j)),
            pl.BlockSpec((tm, tn), lambda j, i: (i, j + nb)),
            _wspec(wa, layer, (ka, tn), lambda j, i: (0, j)),
            _wspec(wb, layer, (kb, tn), lambda j, i: (0, j)),
        ],
        out_specs=pl.BlockSpec((tm, tn), lambda j, i: (i, j)),
        scratch_shapes=[pltpu.VMEM((ka, tn), BF16), pltpu.VMEM((kb, tn), BF16)],
        compiler_params=_cparams(2),
        name="branch_merge",
    )(ya, yb, gates, gates, wa, wb)


def _outproj_kernel(m_ref, w_ref, x_ref, nw_ref, g_ref, o_ref, wbf_ref):
    @pl.when(pl.program_id(0) == 0)
    def _():
        wbf_ref[...] = w_ref[...].astype(BF16)

    y = jnp.dot(m_ref[...], wbf_ref[...], preferred_element_type=F32)
    yn = y * lax.rsqrt(jnp.mean(y * y, axis=-1, keepdims=True) + EPS) * nw_ref[...]
    o_ref[...] = x_ref[...] + g_ref[0] * yn


def outproj_residual(m, w, x, nw, gate, set_of_tile, *, layer, rows, tm=512):
    d = w.shape[-1]
    k = w.shape[-2]
    tm = _row_tile(rows, tm)
    return pl.pallas_call(
        _outproj_kernel,
        out_shape=jax.ShapeDtypeStruct((rows, d), F32),
        grid=(rows // tm,),
        in_specs=[
            pl.BlockSpec((tm, k), lambda i: (i, 0)),
            _wspec(w, layer, (k, d), lambda i: (0, 0), pipeline_mode=pl.Buffered(1)),
            pl.BlockSpec((tm, d), lambda i: (i, 0)),
            pl.BlockSpec((1, d), lambda i: (0, 0)),
            pl.BlockSpec((1, 1, d), lambda i: (set_of_tile(i, tm), 0, 0)),
        ],
        out_specs=pl.BlockSpec((tm, d), lambda i: (i, 0)),
        scratch_shapes=[pltpu.VMEM((k, d), BF16)],
        compiler_params=_cparams(1),
        name="outproj_residual",
    )(m, w, x, nw.reshape(1, d), gate)


def s5_operators(lam_re, lam_im, log_step, b_re, b_im, c_re, c_im, n_levels):
    tc = S5_CHUNK
    hp = lax.Precision.HIGHEST
    lr, li = lam_re.astype(F32), lam_im.astype(F32)
    step = jnp.exp(log_step.astype(F32))[..., None]

    def apow(l):
        mag = jnp.exp(lr * step * l)
        return mag * jnp.cos(li * step * l), mag * jnp.sin(li * step * l)

    ab_re, ab_im = apow(1.0)
    den = lr * lr + li * li
    nr = ab_re - 1.0
    cr = (nr * lr + ab_im * li) / den
    ci = (ab_im * lr - nr * li) / den
    br, bi = b_re.astype(F32), b_im.astype(F32)
    bb_re = cr[..., None] * br - ci[..., None] * bi
    bb_im = cr[..., None] * bi + ci[..., None] * br
    lags = jnp.arange(tc + 1, dtype=F32)[:, None, None, None]
    pw_re, pw_im = apow(lags)
    pw_re, pw_im = jnp.moveaxis(pw_re, 0, 2), jnp.moveaxis(pw_im, 0, 2)
    cre, cim = c_re.astype(F32), c_im.astype(F32)
    cp_re = cre[:, :, None] * pw_re[:, :, :, None] - cim[:, :, None] * pw_im[:, :, :, None]
    cp_im = cre[:, :, None] * pw_im[:, :, :, None] + cim[:, :, None] * pw_re[:, :, :, None]
    bbt = jnp.concatenate([bb_re.transpose(0, 1, 3, 2), -bb_im.transpose(0, 1, 3, 2)], axis=-1)
    cps = jnp.concatenate([cp_re[:, :, :tc], cp_im[:, :, :tc]], axis=-1)
    cps = jnp.stack([cps[0], cps[1, :, ::-1]])
    cps = cps.transpose(0, 1, 4, 2, 3)
    pr, pi = pw_re[:, :, tc - 1 - jnp.arange(tc)], pw_im[:, :, tc - 1 - jnp.arange(tc)]
    win_re = pr[:, :, :, None, :] * bb_re.transpose(0, 1, 3, 2)[:, :, None] - pi[:, :, :, None, :] * bb_im.transpose(0, 1, 3, 2)[:, :, None]
    win_im = pr[:, :, :, None, :] * bb_im.transpose(0, 1, 3, 2)[:, :, None] + pi[:, :, :, None, :] * bb_re.transpose(0, 1, 3, 2)[:, :, None]
    win = jnp.concatenate([win_re, win_im], axis=-1)
    wo_re = cp_re[:, :, 1:].transpose(0, 1, 4, 2, 3)
    wo_im = -cp_im[:, :, 1:].transpose(0, 1, 4, 2, 3)
    wout = jnp.concatenate([wo_re, wo_im], axis=2)
    win = jnp.stack([win[0], win[1, :, ::-1]])
    wout = jnp.stack([wout[0], wout[1, :, :, ::-1]])
    g = lr.shape[1]
    j = br.shape[-1]
    p = lr.shape[-1]
    rows = []
    for k in range(n_levels):
        ar, ai = apow(float(tc * 2 ** k))
        rows.append(jnp.concatenate([ar, ar], axis=-1))
        rows.append(jnp.concatenate([-ai, ai], axis=-1))
    apw = jnp.stack(rows, axis=2)
    return ((bbt.transpose(1, 0, 2, 3), cps.reshape(2, g, 2 * p, tc * j).transpose(1, 0, 2, 3)),
            win.reshape(2, g, tc * j, 2 * p).transpose(1, 0, 2, 3).astype(BF16),
            wout.reshape(2, g, 2 * p, tc * j).transpose(1, 0, 2, 3).astype(BF16),
            apw.transpose(1, 0, 2, 3))


def _s5_kernel(u_ref, bbt_ref, cps_ref, win_ref, wout_ref, apw_ref, dsk_ref, y_ref, *, nb, ctx_rows, n_levels, p):
    u = u_ref[0]
    n, lanes = u.shape
    jw = bbt_ref.shape[2]
    tc = lanes // jw
    row = lax.broadcasted_iota(jnp.int32, (n, 1), 0)
    lane = lax.broadcasted_iota(jnp.int32, (jw, lanes), 1)
    y = u.astype(F32) * dsk_ref[0]
    for d in range(2):
        kt = jnp.dot(bbt_ref[0, d], cps_ref[0, d], precision=lax.Precision.HIGHEST, preferred_element_type=F32)
        blocks = []
        for s in range(tc):
            sh = jw * s if d == 0 else jw * (tc - 1 - s)
            if sh == 0:
                blocks.append(kt)
            elif d == 0:
                blocks.append(jnp.where(lane >= sh, pltpu.roll(kt, sh, 1), 0.0))
            else:
                blocks.append(jnp.where(lane < lanes - sh, pltpu.roll(kt, lanes - sh, 1), 0.0))
        tmat = jnp.concatenate(blocks, axis=0).astype(BF16)
        y = y + jnp.dot(u, tmat, preferred_element_type=F32)
        x = jnp.dot(u, win_ref[0, d], preferred_element_type=F32)
        if d == 0:
            def shift(a, s):
                return jnp.where(row >= s, pltpu.roll(a, s, 0), 0.0)
        else:
            if ctx_rows:
                x = pltpu.roll(x, n - ctx_rows, 0)

            def shift(a, s):
                return jnp.where(row < n - s, pltpu.roll(a, n - s, 0), 0.0)
        x = shift(x, nb)
        for k in range(n_levels):
            sh = shift(x, nb * 2 ** k)
            a1 = apw_ref[0, d, 2 * k:2 * k + 1, :]
            a2 = apw_ref[0, d, 2 * k + 1:2 * k + 2, :]
            x = x + a1 * sh + a2 * pltpu.roll(sh, p, 1)
        if d == 1 and ctx_rows:
            x = pltpu.roll(x, ctx_rows, 0)
        y = y + jnp.dot(x.astype(BF16), wout_ref[0, d], preferred_element_type=F32)
    y_ref[0] = y.astype(y_ref.dtype)


def s5_scan(uc, toep, win, wout, apw, dsk, *, nb, ctx_rows, n_levels):
    g, n, lanes = uc.shape
    p2 = win.shape[-1]
    bbt, cps = toep
    jw = bbt.shape[2]
    return pl.pallas_call(
        functools.partial(_s5_kernel, nb=nb, ctx_rows=ctx_rows, n_levels=n_levels, p=p2 // 2),
        out_shape=jax.ShapeDtypeStruct((g, n, lanes), BF16),
        grid=(g,),
        in_specs=[
            pl.BlockSpec((1, n, lanes), lambda i: (i, 0, 0)),
            pl.BlockSpec((1, 2, jw, p2), lambda i: (i, 0, 0, 0)),
            pl.BlockSpec((1, 2, p2, lanes), lambda i: (i, 0, 0, 0)),
            pl.BlockSpec((1, 2, lanes, p2), lambda i: (i, 0, 0, 0)),
            pl.BlockSpec((1, 2, p2, lanes), lambda i: (i, 0, 0, 0)),
            pl.BlockSpec((1, 2, 2 * n_levels, p2), lambda i: (i, 0, 0, 0)),
            pl.BlockSpec((1, 1, lanes), lambda i: (i, 0, 0)),
        ],
        out_specs=pl.BlockSpec((1, n, lanes), lambda i: (i, 0, 0)),
        compiler_params=_cparams(1),
        name="s5_scan",
    )(uc, bbt, cps, win, wout, apw, dsk)


def s5_mix(u_rows, ops, d_skip, *, bsz, seq, ctx):
    toep, win, wout, apw = ops
    g = win.shape[0]
    lanes = win.shape[2]
    tc = S5_CHUNK
    j = lanes // tc
    hgt = seq // GRID_W
    ul = u_rows[:bsz * seq].reshape(bsz, hgt, GRID_W, g, j).transpose(0, 2, 1, 3, 4).reshape(bsz, seq, g, j)
    uc = u_rows[bsz * seq:].reshape(bsz, ctx, g, j)
    useq = jnp.concatenate([uc, ul], axis=1)
    nc = (seq + ctx) // tc
    uch = useq.reshape(bsz, nc, tc, g, j).transpose(3, 1, 0, 2, 4).reshape(g, nc * bsz, lanes)
    n_levels = apw.shape[2] // 2
    dsk = jnp.tile(d_skip.astype(F32).reshape(g, 1, j), (1, tc, 1)).reshape(g, 1, lanes)
    ych = s5_scan(uch.astype(BF16), toep, win, wout, apw, dsk, nb=bsz, ctx_rows=(ctx // tc) * bsz, n_levels=n_levels)
    yseq = ych.reshape(g, nc, bsz, tc, j).transpose(2, 1, 3, 0, 4).reshape(bsz, seq + ctx, g * j)
    yc = yseq[:, :ctx].reshape(bsz * ctx, g * j)
    yl = yseq[:, ctx:].reshape(bsz, GRID_W, hgt, g * j).transpose(0, 2, 1, 3).reshape(bsz * seq, g * j)
    return jnp.concatenate([yl, yc], axis=0)


def _softplus(x):
    return jnp.maximum(x, 0.0) + jnp.log1p(jnp.exp(-jnp.abs(x)))


def _dn_kernel(ql_ref, kl_ref, vl_ref, qc_ref, kc_ref, vc_ref, bal_ref, bac_ref, zl_ref, zc_ref,
               cw_ref, lp_ref, nw_ref, ol_ref, oc_ref,
               nt_s, w2t_s, qp_s, el_s, o_s, tok_s, *, n_heads, ctx):
    c = DN_CHUNK
    seq = ql_ref.shape[0]
    t = seq + ctx
    dk = ql_ref.shape[1]
    nch = t // c
    ncc = ctx // c
    h = pl.program_id(1)
    row = lax.broadcasted_iota(jnp.int32, (t, 1), 0)
    rowc = row % c
    lane = lax.broadcasted_iota(jnp.int32, (1, dk), 1)

    first = (row == 0) | (row == ctx)
    last = (row == ctx - 1) | (row == t - 1)

    def conv_silu(xc_ref, xl_ref, kind):
        x = jnp.concatenate([xc_ref[...], xl_ref[...]], axis=0).astype(F32)
        xp = jnp.where(first, 0.0, pltpu.roll(x, 1, 0))
        xn = jnp.where(last, 0.0, pltpu.roll(x, t - 1, 0))
        w = cw_ref[0, kind]
        return _silu(xp * w[0:1] + x * w[1:2] + xn * w[2:3])

    def l2n(x):
        return x * lax.rsqrt(jnp.sum(x * x, axis=-1, keepdims=True) + EPS)

    q = l2n(conv_silu(qc_ref, ql_ref, 0)) * (dk ** -0.5)
    k = l2n(conv_silu(kc_ref, kl_ref, 1))
    v = conv_silu(vc_ref, vl_ref, 2)

    @pl.when(h == 0)
    def _():
        ba = jnp.concatenate([bac_ref[...], bal_ref[...]], axis=0)
        g_all = -lp_ref[0:1, :] * _softplus(ba + lp_ref[1:2, :])
        pf, sf = g_all, g_all
        s = 1
        while s < c:
            pf = pf + jnp.where(rowc >= s, pltpu.roll(pf, s, 0), 0.0)
            sf = sf + jnp.where(rowc < c - s, pltpu.roll(sf, t - s, 0), 0.0)
            s *= 2
        tok_s[0] = jax.nn.sigmoid(ba)
        tok_s[1] = pf
        tok_s[2] = sf

    beta_all, pf, sf = tok_s[0], tok_s[1], tok_s[2]

    def col(a, idx):
        return jnp.sum(jnp.where(lane == idx, a, 0.0), axis=1, keepdims=True)

    causal_f = (lax.broadcasted_iota(jnp.int32, (c, c), 0) >= lax.broadcasted_iota(jnp.int32, (c, c), 1))[None]
    strict_f = (lax.broadcasted_iota(jnp.int32, (c, c), 0) > lax.broadcasted_iota(jnp.int32, (c, c), 1))[None]
    causal_b = (lax.broadcasted_iota(jnp.int32, (c, c), 0) <= lax.broadcasted_iota(jnp.int32, (c, c), 1))[None]
    strict_b = (lax.broadcasted_iota(jnp.int32, (c, c), 0) < lax.broadcasted_iota(jnp.int32, (c, c), 1))[None]
    eye = (lax.broadcasted_iota(jnp.int32, (c, c), 0) == lax.broadcasted_iota(jnp.int32, (c, c), 1))[None].astype(F32)

    sub_blk = (lax.broadcasted_iota(jnp.int32, (c, c), 0) // DN_SUB
               == lax.broadcasted_iota(jnp.int32, (c, c), 1) // DN_SUB)[None]

    def neg_pow_inverse(x, m, limit):
        pinv = eye + x
        while m < limit:
            xb = x.astype(BF16)
            x = jnp.einsum('cij,cjk->cik', xb, xb, preferred_element_type=F32)
            pinv = pinv + jnp.einsum('cij,cjk->cik', pinv.astype(BF16), x.astype(BF16), preferred_element_type=F32)
            m *= 2
        return pinv

    q3 = q.reshape(nch, c, dk)
    k3 = k.reshape(nch, c, dk)
    v3 = v.reshape(nch, c, dk)
    k3b = k3.astype(BF16)
    q3b = q3.astype(BF16)
    for d in range(2):
        causal, strict = (causal_f, strict_f) if d == 0 else (causal_b, strict_b)
        beta = col(beta_all, d * n_heads + h)
        gc = col(pf if d == 0 else sf, (2 + d) * n_heads + h)
        hi = gc.astype(BF16).astype(F32)
        mid = (gc - hi).astype(BF16).astype(F32)
        lo = gc - hi - mid
        pieces = (hi, mid, lo)
        g1 = jnp.zeros((t, dk), F32)
        g2 = jnp.zeros((t, dk), F32)
        for n_p, piece in enumerate(pieces):
            pb = jnp.broadcast_to(piece, (t, dk))
            g1 = jnp.where(lane == n_p, pb, jnp.where(lane == 3 + n_p, 1.0, g1))
            g2 = jnp.where(lane == n_p, 1.0, jnp.where(lane == 3 + n_p, -pb, g2))
        ldiff = jnp.einsum('cid,cjd->cij', g1.astype(BF16).reshape(nch, c, dk), g2.astype(BF16).reshape(nch, c, dk),
                           preferred_element_type=F32)
        decay = jnp.where(causal, jnp.exp(jnp.where(causal, ldiff, 0.0)), 0.0)
        beta3 = beta.reshape(nch, c, 1)
        gc3 = gc.reshape(nch, c, 1)
        glast3 = gc3[:, c - 1:c, :] if d == 0 else gc3[:, 0:1, :]
        kb = k3 * beta3
        a = jnp.einsum('cid,cjd->cij', kb.astype(BF16), k3b, preferred_element_type=F32)
        a = jnp.where(strict, a * decay, 0.0)
        qk = jnp.einsum('cid,cjd->cij', q3b, k3b, preferred_element_type=F32)
        qk = jnp.where(causal, qk * decay, 0.0)
        a_diag = jnp.where(sub_blk, a, 0.0)
        dinv = neg_pow_inverse(-a_diag, 2, DN_SUB)
        n_off = jnp.einsum('cij,cjk->cik', dinv.astype(BF16), (a - a_diag).astype(BF16), preferred_element_type=F32)
        pinv = neg_pow_inverse(-n_off, 2 * DN_SUB, c)
        pinv = jnp.einsum('cij,cjk->cik', pinv.astype(BF16), dinv.astype(BF16), preferred_element_type=F32)
        rhs = jnp.concatenate([v3 * beta3, kb * jnp.exp(gc3)], axis=-1)
        sol = jnp.einsum('cij,cjd->cid', pinv.astype(BF16), rhs.astype(BF16), preferred_element_type=F32)
        solb = sol.astype(BF16)
        qs = jnp.einsum('cij,cjd->cid', qk.astype(BF16), solb, preferred_element_type=F32)
        o_s[d] = qs[:, :, :dk].reshape(t, dk)
        qp_s[d] = (q3 * jnp.exp(gc3) - qs[:, :, dk:]).reshape(t, dk).astype(BF16)
        ke = (k3 * jnp.exp(glast3 - gc3)).astype(BF16)
        solt = jnp.swapaxes(sol, 1, 2).astype(BF16)
        nw2 = jnp.einsum('cdi,cik->cdk', solt, ke, preferred_element_type=F32)
        nt_s[d] = nw2[:, :dk, :]
        w2t_s[d] = nw2[:, dk:, :].astype(BF16)
        el_s[d] = jnp.broadcast_to(jnp.exp(glast3), (nch, 8, dk))

    def chunk_step(d, ci, st):
        r0 = pl.multiple_of(ci * c, c)
        stb = st.astype(BF16)
        o_s[d, pl.ds(r0, c), :] += lax.dot_general(qp_s[d, pl.ds(r0, c), :], stb, (((1,), (1,)), ((), ())),
                                                   preferred_element_type=F32)
        return (st * el_s[d, ci][0:1, :] + nt_s[d, ci]
                - jnp.dot(stb, w2t_s[d, ci], preferred_element_type=F32))

    def ctx_body(n, carry):
        return chunk_step(0, n, carry[0]), chunk_step(1, ncc - 1 - n, carry[1])

    def lat_body(n, carry):
        return chunk_step(0, ncc + n, carry[0]), chunk_step(1, nch - 1 - n, carry[1])

    zero = jnp.zeros((dk, dk), F32)
    carry = lax.fori_loop(0, ncc, ctx_body, (zero, zero))
    lax.fori_loop(0, nch - ncc, lat_body, carry)

    o = o_s[0] + o_s[1]
    on = o * lax.rsqrt(jnp.mean(o * o, axis=-1, keepdims=True) + EPS) * nw_ref[...]
    z = jnp.concatenate([zc_ref[...], zl_ref[...]], axis=0).astype(F32)
    out = (on * _silu(z)).astype(ol_ref.dtype)
    oc_ref[...] = out[:ctx]
    ol_ref[...] = out[ctx:]


def deltanet_mix(p_main, p_ba, conv_w, a_log, dt_bias, norm_w, *, bsz, seq, ctx, u_width):
    n_heads = a_log.shape[-1]
    dk = norm_w.shape[-1]
    c = DN_CHUNK
    t = seq + ctx
    nch = t // c
    assert seq % c == 0 and ctx % c == 0 and u_width % dk == 0 and 4 * n_heads <= p_ba.shape[1]
    cb = u_width // dk
    lat_rows = bsz * seq
    cw = conv_w.astype(F32).reshape(conv_w.shape[0], 3, n_heads, dk).transpose(2, 1, 0, 3)
    lanes = p_ba.shape[1]
    lp = jnp.zeros((2, lanes), F32)
    lp = lp.at[0, 2 * n_heads:4 * n_heads].set(jnp.exp(a_log.astype(F32)).reshape(-1))
    lp = lp.at[1, 2 * n_heads:4 * n_heads].set(dt_bias.astype(F32).reshape(-1))
    cblk = lat_rows // ctx

    def lat_spec(off):
        return pl.BlockSpec((seq, dk), lambda b, h: (b, off + h))

    def ctx_spec(off):
        return pl.BlockSpec((ctx, dk), lambda b, h: (cblk + b, off + h))

    yl, yc = pl.pallas_call(
        functools.partial(_dn_kernel, n_heads=n_heads, ctx=ctx),
        out_shape=(jax.ShapeDtypeStruct((lat_rows, n_heads * dk), BF16),
                   jax.ShapeDtypeStruct((bsz * ctx, n_heads * dk), BF16)),
        grid=(bsz, n_heads),
        in_specs=[
            lat_spec(cb), lat_spec(cb + n_heads), lat_spec(cb + 2 * n_heads),
            ctx_spec(cb), ctx_spec(cb + n_heads), ctx_spec(cb + 2 * n_heads),
            pl.BlockSpec((seq, lanes), lambda b, h: (b, 0)),
            pl.BlockSpec((ctx, lanes), lambda b, h: (cblk + b, 0)),
            lat_spec(cb + 3 * n_heads), ctx_spec(cb + 3 * n_heads),
            pl.BlockSpec((1, 3, conv_w.shape[0], dk), lambda b, h: (h, 0, 0, 0)),
            pl.BlockSpec((2, lanes), lambda b, h: (0, 0)),
            pl.BlockSpec((1, dk), lambda b, h: (0, 0)),
        ],
        out_specs=(pl.BlockSpec((seq, dk), lambda b, h: (b, h)),
                   pl.BlockSpec((ctx, dk), lambda b, h: (b, h))),
        scratch_shapes=[
            pltpu.VMEM((2, nch, dk, dk), F32),
            pltpu.VMEM((2, nch, dk, dk), BF16),
            pltpu.VMEM((2, t, dk), BF16),
            pltpu.VMEM((2, nch, 8, dk), F32),
            pltpu.VMEM((2, t, dk), F32),
            pltpu.VMEM((3, t, lanes), F32),
        ],
        compiler_params=_cparams(2),
        name="deltanet",
    )(p_main, p_main, p_main, p_main, p_main, p_main, p_ba, p_ba, p_main, p_main,
      cw, lp, norm_w.astype(F32).reshape(1, dk))
    return jnp.concatenate([yl, yc], axis=0)


ROW_SUB = 16


def _tiled_rows(x2d):
    n, d = x2d.shape
    assert d == ROW_SUB * LANES, (d, ROW_SUB)
    return x2d.reshape(n * ROW_SUB, LANES)


def _load_token_rows(ref2d, first_token, n_tokens):
    return jnp.concatenate(
        [ref2d[pl.ds(first_token * ROW_SUB + j, n_tokens, stride=ROW_SUB), :] for j in range(ROW_SUB)], axis=1)


def _row_gather(table_hbm, idx_ref, buf, sem, n_rows):
    def copy(r, src_row):
        return pltpu.make_async_copy(table_hbm.at[pl.ds(pl.multiple_of(src_row * ROW_SUB, ROW_SUB), ROW_SUB)],
                                     buf.at[pl.ds(pl.multiple_of(r * ROW_SUB, ROW_SUB), ROW_SUB)], sem)

    def start():
        def body(r, carry):
            copy(r, idx_ref[0, 0, r]).start()
            return carry
        lax.fori_loop(0, n_rows, body, 0, unroll=8)

    def wait():
        def body(r, carry):
            copy(r, 0).wait()
            return carry
        lax.fori_loop(0, n_rows, body, 0, unroll=8)

    return start, wait


def _experts_kernel(be_ref, nv_ref, tok_ref, tokn_ref, h_hbm, wg_ref, wu_ref, wd_ref, o_ref,
                    wgb, wub, wdb, xbuf, sem):
    i = pl.program_id(0)
    nv = nv_ref[0]
    mb = xbuf.shape[1] // ROW_SUB
    lw = xbuf.shape[2]
    slot = i % 2
    start_first, _ = _row_gather(h_hbm, tok_ref, xbuf.at[0], sem.at[0], mb)
    start_next, _ = _row_gather(h_hbm, tokn_ref, xbuf.at[1 - slot], sem.at[1 - slot], mb)
    _, wait_cur = _row_gather(h_hbm, tok_ref, xbuf.at[slot], sem.at[slot], mb)

    @pl.when((i == 0) & (nv > 0))
    def _():
        start_first()

    @pl.when(i + 1 < nv)
    def _():
        start_next()

    prev = be_ref[jnp.maximum(i - 1, 0)]

    @pl.when((i == 0) | (be_ref[i] != prev))
    def _():
        wgb[...] = wg_ref[0].astype(BF16)
        wub[...] = wu_ref[0].astype(BF16)
        wdb[...] = wd_ref[0].astype(BF16)

    @pl.when(i < nv)
    def _():
        wait_cur()
        x = _load_token_rows(xbuf.at[slot], 0, mb).astype(BF16)
        g = jnp.dot(x, wgb[...], preferred_element_type=F32)
        u = jnp.dot(x, wub[...], preferred_element_type=F32)
        a = (_silu(g) * u).astype(BF16)
        y = jnp.dot(a, wdb[...], preferred_element_type=F32)
        for j in range(ROW_SUB):
            o_ref[pl.ds(j, mb, stride=ROW_SUB), :] = y[:, j * lw:(j + 1) * lw]

    @pl.when(i >= nv)
    def _():
        o_ref[...] = jnp.zeros_like(o_ref)


def routed_experts(h, tok, block_e, n_valid, w_gate, w_up, w_down):
    d = h.shape[1]
    f = w_gate.shape[-1]
    mb = MOE_BLOCK
    n_blocks = tok.shape[0] // mb
    lw = d // ROW_SUB
    w_gate, w_up = w_gate.reshape(-1, d, f), w_up.reshape(-1, d, f)
    w_down = w_down.reshape(-1, f, d)
    tok3 = tok.reshape(n_blocks, 1, mb)
    return pl.pallas_call(
        _experts_kernel,
        out_shape=jax.ShapeDtypeStruct((n_blocks * mb * ROW_SUB, lw), F32),
        grid_spec=pltpu.PrefetchScalarGridSpec(
            num_scalar_prefetch=2,
            grid=(n_blocks,),
            in_specs=[
                pl.BlockSpec((1, 1, mb), lambda i, be, nv: (i, 0, 0), memory_space=pltpu.SMEM),
                pl.BlockSpec((1, 1, mb), lambda i, be, nv: (jnp.minimum(i + 1, n_blocks - 1), 0, 0),
                             memory_space=pltpu.SMEM),
                pl.BlockSpec(memory_space=pl.ANY),
                pl.BlockSpec((1, d, f), lambda i, be, nv: (be[i], 0, 0)),
                pl.BlockSpec((1, d, f), lambda i, be, nv: (be[i], 0, 0)),
                pl.BlockSpec((1, f, d), lambda i, be, nv: (be[i], 0, 0)),
            ],
            out_specs=pl.BlockSpec((mb * ROW_SUB, lw), lambda i, be, nv: (i, 0)),
            scratch_shapes=[pltpu.VMEM((d, f), BF16), pltpu.VMEM((d, f), BF16), pltpu.VMEM((f, d), BF16),
                            pltpu.VMEM((2, mb * ROW_SUB, lw), F32), pltpu.SemaphoreType.DMA((2,))],
        ),
        compiler_params=_cparams(1),
        name="routed_experts",
    )(block_e, n_valid, tok3, tok3, _tiled_rows(h), w_gate, w_up, w_down)


def _swiglu_kernel(x_ref, wg_ref, wu_ref, wd_ref, o_ref, wgb, wub, wdb):
    @pl.when(pl.program_id(0) == 0)
    def _():
        wgb[...] = wg_ref[...].astype(BF16)
        wub[...] = wu_ref[...].astype(BF16)
        wdb[...] = wd_ref[...].astype(BF16)

    x = x_ref[...].astype(BF16)
    g = jnp.dot(x, wgb[...], preferred_element_type=F32)
    u = jnp.dot(x, wub[...], preferred_element_type=F32)
    a = (_silu(g) * u).astype(BF16)
    o_ref[...] = jnp.dot(a, wdb[...], preferred_element_type=F32).astype(o_ref.dtype)


def shared_expert(x, wg, wu, wd, *, layer, rows, tm=None):
    d = x.shape[1]
    f = wg.shape[-1]
    tm = _row_tile(rows, tm)
    const = lambda i: (0, 0)
    return pl.pallas_call(
        _swiglu_kernel,
        out_shape=jax.ShapeDtypeStruct((rows, d), F32),
        grid=(rows // tm,),
        in_specs=[
            pl.BlockSpec((tm, d), lambda i: (i, 0)),
            _wspec(wg, layer, (d, f), const, pipeline_mode=pl.Buffered(1)),
            _wspec(wu, layer, (d, f), const, pipeline_mode=pl.Buffered(1)),
            _wspec(wd, layer, (f, d), const, pipeline_mode=pl.Buffered(1)),
        ],
        out_specs=pl.BlockSpec((tm, d), lambda i: (i, 0)),
        scratch_shapes=[pltpu.VMEM((d, f), BF16), pltpu.VMEM((d, f), BF16), pltpu.VMEM((f, d), BF16)],
        compiler_params=_cparams(1),
        name="shared_expert",
    )(x, wg, wu, wd)


def _ffn_res_kernel(dst_ref, dstn_ref, y_hbm, w_ref, s_ref, x_ref, nw_ref, g_ref, o_ref, buf, sem, *, top_k):
    i = pl.program_id(0)
    tm = x_ref.shape[0]
    nr = top_k * tm
    slot = i % 2
    start_first, _ = _row_gather(y_hbm, dst_ref, buf.at[0], sem.at[0], nr)
    start_next, _ = _row_gather(y_hbm, dstn_ref, buf.at[1 - slot], sem.at[1 - slot], nr)
    _, wait_cur = _row_gather(y_hbm, dst_ref, buf.at[slot], sem.at[slot], nr)

    @pl.when(i == 0)
    def _():
        start_first()

    @pl.when(i + 1 < pl.num_programs(0))
    def _():
        start_next()

    wait_cur()
    w = w_ref[...]
    f = s_ref[...]
    for k in range(top_k):
        f = f + _load_token_rows(buf.at[slot], k * tm, tm) * w[:, k:k + 1]
    fn = f * lax.rsqrt(jnp.mean(f * f, axis=-1, keepdims=True) + EPS) * nw_ref[...]
    o_ref[...] = x_ref[...] + g_ref[0] * fn


def ffn_residual(y, dest, wts, shared, x, nw, gate, set_of_tile, *, rows, tm=128):
    d = x.shape[1]
    lw = d // ROW_SUB
    top_k = wts.shape[1]
    tm = _row_tile(rows, tm)
    n_tiles = rows // tm
    dst3 = dest.reshape(top_k, n_tiles, tm).transpose(1, 0, 2).reshape(n_tiles, 1, top_k * tm)
    return pl.pallas_call(
        functools.partial(_ffn_res_kernel, top_k=top_k),
        out_shape=jax.ShapeDtypeStruct((rows, d), F32),
        grid=(n_tiles,),
        in_specs=[
            pl.BlockSpec((1, 1, top_k * tm), lambda i: (i, 0, 0), memory_space=pltpu.SMEM),
            pl.BlockSpec((1, 1, top_k * tm), lambda i: (jnp.minimum(i + 1, n_tiles - 1), 0, 0),
                         memory_space=pltpu.SMEM),
            pl.BlockSpec(memory_space=pl.ANY),
            pl.BlockSpec((tm, top_k), lambda i: (i, 0)),
            pl.BlockSpec((tm, d), lambda i: (i, 0)),
            pl.BlockSpec((tm, d), lambda i: (i, 0)),
            pl.BlockSpec((1, d), lambda i: (0, 0)),
            pl.BlockSpec((1, 1, d), lambda i: (set_of_tile(i, tm), 0, 0)),
        ],
        out_specs=pl.BlockSpec((tm, d), lambda i: (i, 0)),
        scratch_shapes=[pltpu.VMEM((2, top_k * tm * ROW_SUB, lw), F32), pltpu.SemaphoreType.DMA((2,))],
        compiler_params=_cparams(1),
        name="ffn_residual",
    )(dst3, dst3, y, wts, shared, x, nw.reshape(1, d), gate)


def _route_kernel(h_ref, wr_ref, b_ref, eidx_ref, wts_ref, rank_ref, cnt_ref, wrb, tri, cnt_s, *,
                  n_groups, topk_groups, top_k):
    n_exp, tm = wr_ref.shape[0], h_ref.shape[0]
    gs = n_exp // n_groups
    ninf = -jnp.inf

    @pl.when(pl.program_id(0) == 0)
    def _():
        wrb[...] = wr_ref[...].astype(BF16)
        tri[...] = (lax.broadcasted_iota(jnp.int32, (tm, tm), 0)
                    < lax.broadcasted_iota(jnp.int32, (tm, tm), 1)).astype(BF16)
        cnt_s[...] = jnp.zeros_like(cnt_s)

    logits = lax.dot_general(wrb[...], h_ref[...].astype(BF16), (((1,), (1,)), ((), ())),
                             preferred_element_type=F32)
    scores = jax.nn.sigmoid(logits)
    sel = scores + b_ref[:, 0:1]
    s3 = sel.reshape(n_groups, gs, tm)
    io3 = lax.broadcasted_iota(jnp.int32, (n_groups, gs, tm), 1)
    m1 = jnp.max(s3, axis=1, keepdims=True)
    i1 = jnp.min(jnp.where(s3 == m1, io3, gs), axis=1, keepdims=True)
    m2 = jnp.max(jnp.where(io3 == i1, ninf, s3), axis=1, keepdims=True)
    gscore = (m1 + m2).reshape(n_groups, tm)
    iog = lax.broadcasted_iota(jnp.int32, (n_groups, tm), 0)
    gsel = jnp.zeros((n_groups, tm), jnp.bool_)
    for _ in range(topk_groups):
        gm = jnp.max(gscore, axis=0, keepdims=True)
        gi = jnp.min(jnp.where(gscore == gm, iog, n_groups), axis=0, keepdims=True)
        hit = iog == gi
        gsel = gsel | hit
        gscore = jnp.where(hit, ninf, gscore)
    x = jnp.where(gsel.reshape(n_groups, 1, tm), s3, ninf).reshape(n_exp, tm)
    ioe = lax.broadcasted_iota(jnp.int32, (n_exp, tm), 0)
    hits = []
    chosen = jnp.zeros((n_exp, tm), jnp.bool_)
    for k in range(top_k):
        m = jnp.max(x, axis=0, keepdims=True)
        idx = jnp.min(jnp.where(x == m, ioe, n_exp), axis=0, keepdims=True)
        hit = ioe == idx
        x = jnp.where(hit, ninf, x)
        chosen = chosen | hit
        hits.append(hit)
        eidx_ref[k:k + 1, :] = idx
    wsel = jnp.where(chosen, scores, 0.0)
    wd = wsel / jnp.sum(wsel, axis=0, keepdims=True) * ROUTE_SCALE
    cm = jnp.where(chosen, 1.0, 0.0)
    rank = jnp.dot(cm.astype(BF16), tri[...], preferred_element_type=F32) + cnt_s[:, 0:1]
    for k in range(top_k):
        wts_ref[k:k + 1, :] = jnp.sum(jnp.where(hits[k], wd, 0.0), axis=0, keepdims=True)
        rank_ref[k:k + 1, :] = jnp.sum(jnp.where(hits[k], rank, 0.0), axis=0, keepdims=True).astype(jnp.int32)
    cnt_s[...] = cnt_s[...] + jnp.sum(cm, axis=1, keepdims=True)
    cnt_ref[...] = cnt_s[...].astype(jnp.int32)


def route(h, w_router, e_bias, *, rows, tm=None):
    d = h.shape[1]
    n_exp = w_router.shape[1]
    tm = _row_tile(rows, tm)
    bias = jnp.broadcast_to(e_bias.astype(F32).reshape(n_exp, 1), (n_exp, LANES))
    kt = lambda i: (0, i)
    eidx, wts, rank, cnt = pl.pallas_call(
        functools.partial(_route_kernel, n_groups=N_GROUPS, topk_groups=TOPK_GROUPS, top_k=TOP_K),
        out_shape=(jax.ShapeDtypeStruct((TOP_K, rows), jnp.int32), jax.ShapeDtypeStruct((TOP_K, rows), F32),
                   jax.ShapeDtypeStruct((TOP_K, rows), jnp.int32), jax.ShapeDtypeStruct((n_exp, LANES), jnp.int32)),
        grid=(rows // tm,),
        in_specs=[
            pl.BlockSpec((tm, d), lambda i: (i, 0)),
            pl.BlockSpec((n_exp, d), lambda i: (0, 0)),
            pl.BlockSpec((n_exp, LANES), lambda i: (0, 0)),
        ],
        out_specs=(pl.BlockSpec((TOP_K, tm), kt), pl.BlockSpec((TOP_K, tm), kt), pl.BlockSpec((TOP_K, tm), kt),
                   pl.BlockSpec((n_exp, LANES), lambda i: (0, 0))),
        scratch_shapes=[pltpu.VMEM((n_exp, d), BF16), pltpu.VMEM((tm, tm), BF16), pltpu.VMEM((n_exp, LANES), F32)],
        compiler_params=_cparams(1),
        name="route",
    )(h, w_router.T, bias)
    return eidx, wts, rank, cnt[:, 0]


def moe_ffn(h, w_router, e_bias, w_gate, w_up, w_down, ws_gate, ws_up, ws_down, *, layer, rows):
    n_exp = w_router.shape[1]
    eidx, wts, rank, counts = route(h, w_router, e_bias, rows=rows)
    mb = MOE_BLOCK
    padded = (counts + mb - 1) // mb * mb
    pad_end = jnp.cumsum(padded)
    pad_start = pad_end - padded
    start_of = jnp.sum(jnp.where(eidx[:, :, None] == jnp.arange(n_exp)[None, None, :], pad_start, 0), axis=-1)
    dest = start_of + rank
    n_blocks = (rows * TOP_K + n_exp * (mb - 1) + mb - 1) // mb
    tok = jnp.zeros((n_blocks * mb,), jnp.int32).at[dest.reshape(-1)].set(
        jnp.tile(jnp.arange(rows, dtype=jnp.int32), TOP_K), unique_indices=True)
    n_valid = (pad_end[-1] // mb).astype(jnp.int32)
    starts = jnp.arange(n_blocks, dtype=jnp.int32) * mb
    block_e = jnp.minimum(jnp.sum(starts[:, None] >= pad_end[None, :], axis=1), n_exp - 1).astype(jnp.int32)
    last_e = block_e[jnp.maximum(n_valid - 1, 0)]
    block_e = jnp.where(jnp.arange(n_blocks) < n_valid, block_e, last_e)
    y = routed_experts(h, tok, block_e + layer * n_exp, n_valid.reshape(1), w_gate, w_up, w_down)
    shared = shared_expert(h, ws_gate, ws_up, ws_down, layer=layer, rows=rows)
    return y, dest, wts.T, shared


def kernel(x, c, ctx, c_ctx, w_mod, b_mod, norm_mix_pre, norm_mix_post, norm_ffn_pre, norm_ffn_post,
           w_in, s5_lam_re, s5_lam_im, s5_log_step, s5_b_re, s5_b_im, s5_c_re, s5_c_im, s5_d, s5_w_glu,
           dn_conv, dn_a_log, dn_dt_bias, dn_norm, w_br_s5, w_br_dn, w_out,
           moe_router, moe_bias, moe_w_gate, moe_w_up, moe_w_down, sh_w_gate, sh_w_up, sh_w_down):
    bsz, seq, d = x.shape
    n_ctx = ctx.shape[1]
    depth = w_mod.shape[0]
    lat_rows, ctx_rows = bsz * seq, bsz * n_ctx
    all_rows = lat_rows + ctx_rows
    s5_width = s5_d.shape[1]
    dn_width = w_br_dn.shape[1]
    n_heads = dn_a_log.shape[-1]
    main_cols = s5_width + 4 * dn_width
    ba_cols = 4 * n_heads
    lanes = 128
    nc = (seq + n_ctx) // S5_CHUNK
    n_levels = max(1, (nc - 1).bit_length())

    def set_of_tile(i, tm):
        return jnp.minimum((i * tm) // seq, bsz)

    xs = jnp.concatenate([x.reshape(lat_rows, d), ctx.reshape(ctx_rows, d)], axis=0)
    n_sets = 8
    cin = jnp.zeros((n_sets, d), F32).at[:bsz].set(_silu(c)).at[bsz].set(_silu(c_ctx))
    for i in range(depth):
        last = i == depth - 1
        rows = lat_rows if last else all_rows
        mods = matmul(cin, w_mod, b_mod[i].reshape(1, -1), layer=i, tm=n_sets, name="mods").reshape(n_sets, 6, 1, d)
        mod = [mods[:, k] for k in range(6)]
        hmix = prenorm(xs, norm_mix_pre[i], mod[0], mod[1], set_of_tile)
        p_main = matmul(hmix, w_in, layer=i, n_cols=main_cols, out_dtype=BF16, name="in_proj")
        w_tail = w_in[i, :, main_cols:]
        p_ba = matmul(hmix, jnp.pad(w_tail[:, :ba_cols], ((0, 0), (0, lanes - ba_cols))), tn=lanes, name="in_proj_ba")
        gates = matmul(hmix, w_tail[:, ba_cols:], rows=rows, out_dtype=BF16, name="in_proj_gates")
        ops = s5_operators(s5_lam_re[i], s5_lam_im[i], s5_log_step[i], s5_b_re[i], s5_b_im[i],
                           s5_c_re[i], s5_c_im[i], n_levels)
        y_s5 = s5_mix(p_main[:, :s5_width], ops, s5_d[i], bsz=bsz, seq=seq, ctx=n_ctx)
        y_s5 = s5_glu(y_s5, s5_w_glu, layer=i, rows=rows)
        y_dn = deltanet_mix(p_main, p_ba, dn_conv[i], dn_a_log[i], dn_dt_bias[i], dn_norm[i],
                            bsz=bsz, seq=seq, ctx=n_ctx, u_width=s5_width)
        m = branch_merge(y_s5, y_dn, gates, w_br_s5, w_br_dn, layer=i, rows=rows)
        xs = outproj_residual(m, w_out, xs, norm_mix_post[i], mod[2], set_of_tile, layer=i, rows=rows)
        hffn = prenorm(xs, norm_ffn_pre[i], mod[3], mod[4], set_of_tile, rows=rows, out_dtype=F32)
        y, dest, wts, shared = moe_ffn(hffn, moe_router[i], moe_bias[i], moe_w_gate, moe_w_up, moe_w_down,
                                       sh_w_gate, sh_w_up, sh_w_down, layer=i, rows=rows)
        xs = ffn_residual(y, dest, wts, shared, xs, norm_ffn_post[i], mod[5], set_of_tile, rows=rows)
    return xs[:lat_rows].reshape(bsz, seq, d)
```

```python
import functools

import jax
import jax.numpy as jnp
from jax import lax
from jax.experimental import pallas as pl
from jax.experimental.pallas import tpu as pltpu

F32 = jnp.float32
BF16 = jnp.bfloat16

EPS = 1e-6
GRID_W = 64
S5_CHUNK = 16
DN_CHUNK = 64
DN_SUB = 16
TOP_K = 8
N_GROUPS = 8
TOPK_GROUPS = 4
ROUTE_SCALE = 2.5
MOE_BLOCK = 256

V7X_VMEM_LIMIT = 56 * 1024 * 1024
LANES = 128
ROW_TILE = 1024
ROW_SUB = 8
U32 = jnp.uint32


def _cparams(n_axes, vmem=V7X_VMEM_LIMIT):
    return pltpu.CompilerParams(dimension_semantics=("arbitrary",) * n_axes, vmem_limit_bytes=vmem)


def _silu(x):
    return x * jax.nn.sigmoid(x)


def _row_tile(rows, tm=None):
    tm = min(tm or ROW_TILE, ROW_TILE, rows)
    assert rows % tm == 0, (rows, tm)
    return tm


def _col_tile(n, tn, col0=0):
    tn = min(tn, n)
    while n % tn or col0 % tn:
        tn -= LANES
    return tn


def _wspec(w, layer, block, index_map, **kw):
    if w.ndim == len(block):
        return pl.BlockSpec(block, index_map, **kw)
    return pl.BlockSpec((None,) + tuple(block), lambda *a: (layer,) + tuple(index_map(*a)), **kw)


def _mm_kernel(x_ref, w_ref, b_ref, o_ref, wbf_ref):
    @pl.when(pl.program_id(1) == 0)
    def _():
        wbf_ref[...] = w_ref[...].astype(BF16)

    acc = jnp.dot(x_ref[...].astype(BF16), wbf_ref[...], preferred_element_type=F32)
    o_ref[...] = (acc + b_ref[...]).astype(o_ref.dtype)


def matmul(x, w, bias=None, *, layer=0, n_cols=None, col0=0, out_dtype=F32, tm=None, tn=1024, rows=None,
           name="matmul"):
    m, k = x.shape
    rows = m if rows is None else rows
    n_cols = w.shape[-1] - col0 if n_cols is None else n_cols
    tm = _row_tile(rows, tm)
    tn = _col_tile(n_cols, tn, col0)
    assert rows % tm == 0 and n_cols % tn == 0 and col0 % tn == 0, (rows, tm, n_cols, tn, col0)
    if bias is None:
        bias = jnp.zeros((1, n_cols), F32)
    cb0 = col0 // tn
    return pl.pallas_call(
        _mm_kernel,
        out_shape=jax.ShapeDtypeStruct((rows, n_cols), out_dtype),
        grid=(n_cols // tn, rows // tm),
        in_specs=[
            pl.BlockSpec((tm, k), lambda j, i: (i, 0)),
            _wspec(w, layer, (k, tn), lambda j, i: (0, j + cb0)),
            pl.BlockSpec((1, tn), lambda j, i: (0, j)),
        ],
        out_specs=pl.BlockSpec((tm, tn), lambda j, i: (i, j)),
        scratch_shapes=[pltpu.VMEM((k, tn), BF16)],
        compiler_params=_cparams(2),
        name=name,
    )(x, w, bias)


def _mm_shifted_kernel(x_ref, wa_ref, wb_ref, o_ref, wbf_ref, *, shift):
    @pl.when(pl.program_id(1) == 0)
    def _():
        tn = wbf_ref.shape[1]
        w = jnp.concatenate([wa_ref[...], wb_ref[...]], axis=1)
        wbf_ref[...] = w[:, shift:shift + tn].astype(BF16)

    o_ref[...] = jnp.dot(x_ref[...], wbf_ref[...], preferred_element_type=F32).astype(o_ref.dtype)


def matmul_unaligned(x, w, *, layer, col0, n_cols, out_dtype=BF16, rows=None, tm=None, tn=512, name="matmul_unaligned"):
    m, k = x.shape
    rows = m if rows is None else rows
    tm = _row_tile(rows, tm)
    tn = _col_tile(n_cols, tn)
    base, shift = col0 // tn, col0 % tn
    return pl.pallas_call(
        functools.partial(_mm_shifted_kernel, shift=shift),
        out_shape=jax.ShapeDtypeStruct((rows, n_cols), out_dtype),
        grid=(n_cols // tn, rows // tm),
        in_specs=[
            pl.BlockSpec((tm, k), lambda j, i: (i, 0)),
            _wspec(w, layer, (k, tn), lambda j, i: (0, base + j)),
            _wspec(w, layer, (k, tn), lambda j, i: (0, base + j + 1)),
        ],
        out_specs=pl.BlockSpec((tm, tn), lambda j, i: (i, j)),
        scratch_shapes=[pltpu.VMEM((k, tn), BF16)],
        compiler_params=_cparams(2),
        name=name,
    )(x, w, w)


def _prenorm_kernel(x_ref, w_ref, sh_ref, sc_ref, o_ref, *packed_ref):
    x = x_ref[...]
    y = x * lax.rsqrt(jnp.mean(x * x, axis=-1, keepdims=True) + EPS) * w_ref[...]
    h = y * (1.0 + sc_ref[0]) + sh_ref[0]
    o_ref[...] = h.astype(o_ref.dtype)
    if packed_ref:
        _store_token_rows(packed_ref[0], h)


def prenorm(x, w, shift, scale, set_of_tile, *, rows=None, tm=None, packed=False):
    m, d = x.shape
    rows = m if rows is None else rows
    tm = _row_tile(rows, tm)
    out_shape = [jax.ShapeDtypeStruct((rows, d), BF16)]
    out_specs = [pl.BlockSpec((tm, d), lambda i: (i, 0))]
    if packed:
        out_shape.append(jax.ShapeDtypeStruct((rows * ROW_SUB, LANES), U32))
        out_specs.append(pl.BlockSpec((tm * ROW_SUB, LANES), lambda i: (i, 0)))
    out = pl.pallas_call(
        _prenorm_kernel,
        out_shape=out_shape,
        grid=(rows // tm,),
        in_specs=[
            pl.BlockSpec((tm, d), lambda i: (i, 0)),
            pl.BlockSpec((1, d), lambda i: (0, 0)),
            pl.BlockSpec((1, 1, d), lambda i: (set_of_tile(i, tm), 0, 0)),
            pl.BlockSpec((1, 1, d), lambda i: (set_of_tile(i, tm), 0, 0)),
        ],
        out_specs=out_specs,
        compiler_params=_cparams(1),
        name="prenorm",
    )(x, w.reshape(1, d), shift, scale)
    return out if packed else out[0]


def _glu_kernel(y_ref, w_ref, o_ref, wbf_ref, *, tn):
    @pl.when(pl.program_id(1) == 0)
    def _():
        wbf_ref[...] = w_ref[...].astype(BF16)

    j = pl.program_id(0)
    g = jax.nn.gelu(y_ref[...].astype(F32))
    acc = jnp.dot(g.astype(BF16), wbf_ref[...], preferred_element_type=F32)
    gj = jax.nn.gelu(y_ref[:, pl.ds(pl.multiple_of(j * tn, tn), tn)].astype(F32))
    o_ref[...] = (gj * jax.nn.sigmoid(acc)).astype(o_ref.dtype)


def s5_glu(y, w, *, layer, rows, tm=None, tn=512):
    m, k = y.shape
    tm = _row_tile(rows, tm)
    tn = _col_tile(k, tn)
    return pl.pallas_call(
        functools.partial(_glu_kernel, tn=tn),
        out_shape=jax.ShapeDtypeStruct((rows, k), BF16),
        grid=(k // tn, rows // tm),
        in_specs=[
            pl.BlockSpec((tm, k), lambda j, i: (i, 0)),
            _wspec(w, layer, (k, tn), lambda j, i: (0, j)),
        ],
        out_specs=pl.BlockSpec((tm, tn), lambda j, i: (i, j)),
        scratch_shapes=[pltpu.VMEM((k, tn), BF16)],
        compiler_params=_cparams(2),
        name="s5_glu",
    )(y, w)


def _merge_kernel(a_ref, b_ref, ga_ref, gb_ref, wa_ref, wb_ref, o_ref, wabf_ref, wbbf_ref):
    @pl.when(pl.program_id(1) == 0)
    def _():
        wabf_ref[...] = wa_ref[...].astype(BF16)
        wbbf_ref[...] = wb_ref[...].astype(BF16)

    ya = jnp.dot(a_ref[...], wabf_ref[...], preferred_element_type=F32)
    yb = jnp.dot(b_ref[...], wbbf_ref[...], preferred_element_type=F32)
    m = jax.nn.sigmoid(ga_ref[...].astype(F32)) * ya + jax.nn.sigmoid(gb_ref[...].astype(F32)) * yb
    o_ref[...] = m.astype(o_ref.dtype)


def branch_merge(ya, yb, gates, wa, wb, *, layer, rows, tm=None, tn=512):
    ka, kb = ya.shape[1], yb.shape[1]
    d = wa.shape[-1]
    tm = _row_tile(rows, tm)
    tn = _col_tile(d, tn)
    nb = d // tn
    return pl.pallas_call(
        _merge_kernel,
        out_shape=jax.ShapeDtypeStruct((rows, d), BF16),
        grid=(nb, rows // tm),
        in_specs=[
            pl.BlockSpec((tm, ka), lambda j, i: (i, 0)),
            pl.BlockSpec((tm, kb), lambda j, i: (i, 0)),
            pl.BlockSpec((tm, tn), lambda j, i: (i, j)),
            pl.BlockSpec((tm, tn), lambda j, i: (i, j + nb)),
            _wspec(wa, layer, (ka, tn), lambda j, i: (0, j)),
            _wspec(wb, layer, (kb, tn), lambda j, i: (0, j)),
        ],
        out_specs=pl.BlockSpec((tm, tn), lambda j, i: (i, j)),
        scratch_shapes=[pltpu.VMEM((ka, tn), BF16), pltpu.VMEM((kb, tn), BF16)],
        compiler_params=_cparams(2),
        name="branch_merge",
    )(ya, yb, gates, gates, wa, wb)


def _outproj_kernel(m_ref, w_ref, x_ref, nw_ref, g_ref, o_ref, wbf_ref):
    @pl.when(pl.program_id(0) == 0)
    def _():
        wbf_ref[...] = w_ref[...].astype(BF16)

    y = jnp.dot(m_ref[...], wbf_ref[...], preferred_element_type=F32)
    yn = y * lax.rsqrt(jnp.mean(y * y, axis=-1, keepdims=True) + EPS) * nw_ref[...]
    o_ref[...] = x_ref[...] + g_ref[0] * yn


def outproj_residual(m, w, x, nw, gate, set_of_tile, *, layer, rows, tm=512):
    d = w.shape[-1]
    k = w.shape[-2]
    tm = _row_tile(rows, tm)
    return pl.pallas_call(
        _outproj_kernel,
        out_shape=jax.ShapeDtypeStruct((rows, d), F32),
        grid=(rows // tm,),
        in_specs=[
            pl.BlockSpec((tm, k), lambda i: (i, 0)),
            _wspec(w, layer, (k, d), lambda i: (0, 0), pipeline_mode=pl.Buffered(1)),
            pl.BlockSpec((tm, d), lambda i: (i, 0)),
            pl.BlockSpec((1, d), lambda i: (0, 0)),
            pl.BlockSpec((1, 1, d), lambda i: (set_of_tile(i, tm), 0, 0)),
        ],
        out_specs=pl.BlockSpec((tm, d), lambda i: (i, 0)),
        scratch_shapes=[pltpu.VMEM((k, d), BF16)],
        compiler_params=_cparams(1),
        name="outproj_residual",
    )(m, w, x, nw.reshape(1, d), gate)


def s5_operators(lam_re, lam_im, log_step, b_re, b_im, c_re, c_im, n_levels):
    tc = S5_CHUNK
    hp = lax.Precision.HIGHEST
    lr, li = lam_re.astype(F32), lam_im.astype(F32)
    step = jnp.exp(log_step.astype(F32))[..., None]

    def apow(l):
        mag = jnp.exp(lr * step * l)
        return mag * jnp.cos(li * step * l), mag * jnp.sin(li * step * l)

    ab_re, ab_im = apow(1.0)
    den = lr * lr + li * li
    nr = ab_re - 1.0
    cr = (nr * lr + ab_im * li) / den
    ci = (ab_im * lr - nr * li) / den
    br, bi = b_re.astype(F32), b_im.astype(F32)
    bb_re = cr[..., None] * br - ci[..., None] * bi
    bb_im = cr[..., None] * bi + ci[..., None] * br
    lags = jnp.arange(tc + 1, dtype=F32)[:, None, None, None]
    pw_re, pw_im = apow(lags)
    pw_re, pw_im = jnp.moveaxis(pw_re, 0, 2), jnp.moveaxis(pw_im, 0, 2)
    cre, cim = c_re.astype(F32), c_im.astype(F32)
    cp_re = cre[:, :, None] * pw_re[:, :, :, None] - cim[:, :, None] * pw_im[:, :, :, None]
    cp_im = cre[:, :, None] * pw_im[:, :, :, None] + cim[:, :, None] * pw_re[:, :, :, None]
    bbt = jnp.concatenate([bb_re.transpose(0, 1, 3, 2), -bb_im.transpose(0, 1, 3, 2)], axis=-1)
    cps = jnp.concatenate([cp_re[:, :, :tc], cp_im[:, :, :tc]], axis=-1)
    cps = jnp.stack([cps[0], cps[1, :, ::-1]])
    cps = cps.transpose(0, 1, 4, 2, 3)
    pr, pi = pw_re[:, :, tc - 1 - jnp.arange(tc)], pw_im[:, :, tc - 1 - jnp.arange(tc)]
    win_re = pr[:, :, :, None, :] * bb_re.transpose(0, 1, 3, 2)[:, :, None] - pi[:, :, :, None, :] * bb_im.transpose(0, 1, 3, 2)[:, :, None]
    win_im = pr[:, :, :, None, :] * bb_im.transpose(0, 1, 3, 2)[:, :, None] + pi[:, :, :, None, :] * bb_re.transpose(0, 1, 3, 2)[:, :, None]
    win = jnp.concatenate([win_re, win_im], axis=-1)
    wo_re = cp_re[:, :, 1:].transpose(0, 1, 4, 2, 3)
    wo_im = -cp_im[:, :, 1:].transpose(0, 1, 4, 2, 3)
    wout = jnp.concatenate([wo_re, wo_im], axis=2)
    win = jnp.stack([win[0], win[1, :, ::-1]])
    wout = jnp.stack([wout[0], wout[1, :, :, ::-1]])
    g = lr.shape[1]
    j = br.shape[-1]
    p = lr.shape[-1]
    rows = []
    for k in range(n_levels):
        ar, ai = apow(float(tc * 2 ** k))
        rows.append(jnp.concatenate([ar, ar], axis=-1))
        rows.append(jnp.concatenate([-ai, ai], axis=-1))
    apw = jnp.stack(rows, axis=2)
    return ((bbt.transpose(1, 0, 2, 3), cps.reshape(2, g, 2 * p, tc * j).transpose(1, 0, 2, 3)),
            win.reshape(2, g, tc * j, 2 * p).transpose(1, 0, 2, 3).astype(BF16),
            wout.reshape(2, g, 2 * p, tc * j).transpose(1, 0, 2, 3).astype(BF16),
            apw.transpose(1, 0, 2, 3))


def _s5_kernel(u_ref, bbt_ref, cps_ref, win_ref, wout_ref, apw_ref, dsk_ref, y_ref, *, nb, ctx_rows, n_levels, p):
    u = u_ref[0]
    n, lanes = u.shape
    jw = bbt_ref.shape[2]
    tc = lanes // jw
    row = lax.broadcasted_iota(jnp.int32, (n, 1), 0)
    lane = lax.broadcasted_iota(jnp.int32, (jw, lanes), 1)
    y = u.astype(F32) * dsk_ref[0]
    for d in range(2):
        kt = jnp.dot(bbt_ref[0, d], cps_ref[0, d], precision=lax.Precision.HIGHEST, preferred_element_type=F32)
        blocks = []
        for s in range(tc):
            sh = jw * s if d == 0 else jw * (tc - 1 - s)
            if sh == 0:
                blocks.append(kt)
            elif d == 0:
                blocks.append(jnp.where(lane >= sh, pltpu.roll(kt, sh, 1), 0.0))
            else:
                blocks.append(jnp.where(lane < lanes - sh, pltpu.roll(kt, lanes - sh, 1), 0.0))
        tmat = jnp.concatenate(blocks, axis=0).astype(BF16)
        y = y + jnp.dot(u, tmat, preferred_element_type=F32)
        x = jnp.dot(u, win_ref[0, d], preferred_element_type=F32)
        if d == 0:
            def shift(a, s):
                return jnp.where(row >= s, pltpu.roll(a, s, 0), 0.0)
        else:
            if ctx_rows:
                x = pltpu.roll(x, n - ctx_rows, 0)

            def shift(a, s):
                return jnp.where(row < n - s, pltpu.roll(a, n - s, 0), 0.0)
        x = shift(x, nb)
        for k in range(n_levels):
            sh = shift(x, nb * 2 ** k)
            a1 = apw_ref[0, d, 2 * k:2 * k + 1, :]
            a2 = apw_ref[0, d, 2 * k + 1:2 * k + 2, :]
            x = x + a1 * sh + a2 * pltpu.roll(sh, p, 1)
        if d == 1 and ctx_rows:
            x = pltpu.roll(x, ctx_rows, 0)
        y = y + jnp.dot(x.astype(BF16), wout_ref[0, d], preferred_element_type=F32)
    y_ref[0] = y.astype(y_ref.dtype)


def s5_scan(uc, toep, win, wout, apw, dsk, *, nb, ctx_rows, n_levels):
    g, n, lanes = uc.shape
    p2 = win.shape[-1]
    bbt, cps = toep
    jw = bbt.shape[2]
    return pl.pallas_call(
        functools.partial(_s5_kernel, nb=nb, ctx_rows=ctx_rows, n_levels=n_levels, p=p2 // 2),
        out_shape=jax.ShapeDtypeStruct((g, n, lanes), BF16),
        grid=(g,),
        in_specs=[
            pl.BlockSpec((1, n, lanes), lambda i: (i, 0, 0)),
            pl.BlockSpec((1, 2, jw, p2), lambda i: (i, 0, 0, 0)),
            pl.BlockSpec((1, 2, p2, lanes), lambda i: (i, 0, 0, 0)),
            pl.BlockSpec((1, 2, lanes, p2), lambda i: (i, 0, 0, 0)),
            pl.BlockSpec((1, 2, p2, lanes), lambda i: (i, 0, 0, 0)),
            pl.BlockSpec((1, 2, 2 * n_levels, p2), lambda i: (i, 0, 0, 0)),
            pl.BlockSpec((1, 1, lanes), lambda i: (i, 0, 0)),
        ],
        out_specs=pl.BlockSpec((1, n, lanes), lambda i: (i, 0, 0)),
        compiler_params=_cparams(1),
        name="s5_scan",
    )(uc, bbt, cps, win, wout, apw, dsk)


def s5_mix(u_rows, ops, d_skip, *, bsz, seq, ctx):
    toep, win, wout, apw = ops
    g = win.shape[0]
    lanes = win.shape[2]
    tc = S5_CHUNK
    j = lanes // tc
    hgt = seq // GRID_W
    ul = u_rows[:bsz * seq].reshape(bsz, hgt, GRID_W, g, j).transpose(0, 2, 1, 3, 4).reshape(bsz, seq, g, j)
    uc = u_rows[bsz * seq:].reshape(bsz, ctx, g, j)
    useq = jnp.concatenate([uc, ul], axis=1)
    nc = (seq + ctx) // tc
    uch = useq.reshape(bsz, nc, tc, g, j).transpose(3, 1, 0, 2, 4).reshape(g, nc * bsz, lanes)
    n_levels = apw.shape[2] // 2
    dsk = jnp.tile(d_skip.astype(F32).reshape(g, 1, j), (1, tc, 1)).reshape(g, 1, lanes)
    ych = s5_scan(uch.astype(BF16), toep, win, wout, apw, dsk, nb=bsz, ctx_rows=(ctx // tc) * bsz, n_levels=n_levels)
    yseq = ych.reshape(g, nc, bsz, tc, j).transpose(2, 1, 3, 0, 4).reshape(bsz, seq + ctx, g * j)
    yc = yseq[:, :ctx].reshape(bsz * ctx, g * j)
    yl = yseq[:, ctx:].reshape(bsz, GRID_W, hgt, g * j).transpose(0, 2, 1, 3).reshape(bsz * seq, g * j)
    return jnp.concatenate([yl, yc], axis=0)


def _softplus(x):
    return jnp.maximum(x, 0.0) + jnp.log1p(jnp.exp(-jnp.abs(x)))


def _dn_kernel(ql_ref, kl_ref, vl_ref, qc_ref, kc_ref, vc_ref, bal_ref, bac_ref, zl_ref, zc_ref,
               cw_ref, lp_ref, nw_ref, ol_ref, oc_ref,
               nt_s, w2t_s, qp_s, el_s, o_s, tok_s, *, n_heads, ctx):
    c = DN_CHUNK
    seq = ql_ref.shape[0]
    t = seq + ctx
    dk = ql_ref.shape[1]
    nch = t // c
    ncc = ctx // c
    h = pl.program_id(1)
    row = lax.broadcasted_iota(jnp.int32, (t, 1), 0)
    rowc = row % c
    lane = lax.broadcasted_iota(jnp.int32, (1, dk), 1)

    first = (row == 0) | (row == ctx)
    last = (row == ctx - 1) | (row == t - 1)

    def conv_silu(xc_ref, xl_ref, kind):
        x = jnp.concatenate([xc_ref[...], xl_ref[...]], axis=0).astype(F32)
        xp = jnp.where(first, 0.0, pltpu.roll(x, 1, 0))
        xn = jnp.where(last, 0.0, pltpu.roll(x, t - 1, 0))
        w = cw_ref[0, kind]
        return _silu(xp * w[0:1] + x * w[1:2] + xn * w[2:3])

    def l2n(x):
        return x * lax.rsqrt(jnp.sum(x * x, axis=-1, keepdims=True) + EPS)

    q = l2n(conv_silu(qc_ref, ql_ref, 0)) * (dk ** -0.5)
    k = l2n(conv_silu(kc_ref, kl_ref, 1))
    v = conv_silu(vc_ref, vl_ref, 2)

    @pl.when(h == 0)
    def _():
        ba = jnp.concatenate([bac_ref[...], bal_ref[...]], axis=0)
        g_all = -lp_ref[0:1, :] * _softplus(ba + lp_ref[1:2, :])
        pf, sf = g_all, g_all
        s = 1
        while s < c:
            pf = pf + jnp.where(rowc >= s, pltpu.roll(pf, s, 0), 0.0)
            sf = sf + jnp.where(rowc < c - s, pltpu.roll(sf, t - s, 0), 0.0)
            s *= 2
        tok_s[0] = jax.nn.sigmoid(ba)
        tok_s[1] = pf
        tok_s[2] = sf

    beta_all, pf, sf = tok_s[0], tok_s[1], tok_s[2]

    def col(a, idx):
        return jnp.sum(jnp.where(lane == idx, a, 0.0), axis=1, keepdims=True)

    causal_f = (lax.broadcasted_iota(jnp.int32, (c, c), 0) >= lax.broadcasted_iota(jnp.int32, (c, c), 1))[None]
    strict_f = (lax.broadcasted_iota(jnp.int32, (c, c), 0) > lax.broadcasted_iota(jnp.int32, (c, c), 1))[None]
    causal_b = (lax.broadcasted_iota(jnp.int32, (c, c), 0) <= lax.broadcasted_iota(jnp.int32, (c, c), 1))[None]
    strict_b = (lax.broadcasted_iota(jnp.int32, (c, c), 0) < lax.broadcasted_iota(jnp.int32, (c, c), 1))[None]
    eye = (lax.broadcasted_iota(jnp.int32, (c, c), 0) == lax.broadcasted_iota(jnp.int32, (c, c), 1))[None].astype(F32)

    sub_blk = (lax.broadcasted_iota(jnp.int32, (c, c), 0) // DN_SUB
               == lax.broadcasted_iota(jnp.int32, (c, c), 1) // DN_SUB)[None]

    def neg_pow_inverse(x, m, limit):
        pinv = eye + x
        while m < limit:
            xb = x.astype(BF16)
            x = jnp.einsum('cij,cjk->cik', xb, xb, preferred_element_type=F32)
            pinv = pinv + jnp.einsum('cij,cjk->cik', pinv.astype(BF16), x.astype(BF16), preferred_element_type=F32)
            m *= 2
        return pinv

    q3 = q.reshape(nch, c, dk)
    k3 = k.reshape(nch, c, dk)
    v3 = v.reshape(nch, c, dk)
    k3b = k3.astype(BF16)
    q3b = q3.astype(BF16)
    for d in range(2):
        causal, strict = (causal_f, strict_f) if d == 0 else (causal_b, strict_b)
        beta = col(beta_all, d * n_heads + h)
        gc = col(pf if d == 0 else sf, (2 + d) * n_heads + h)
        hi = gc.astype(BF16).astype(F32)
        mid = (gc - hi).astype(BF16).astype(F32)
        lo = gc - hi - mid
        pieces = (hi, mid, lo)
        g1 = jnp.zeros((t, dk), F32)
        g2 = jnp.zeros((t, dk), F32)
        for n_p, piece in enumerate(pieces):
            pb = jnp.broadcast_to(piece, (t, dk))
            g1 = jnp.where(lane == n_p, pb, jnp.where(lane == 3 + n_p, 1.0, g1))
            g2 = jnp.where(lane == n_p, 1.0, jnp.where(lane == 3 + n_p, -pb, g2))
        ldiff = jnp.einsum('cid,cjd->cij', g1.astype(BF16).reshape(nch, c, dk), g2.astype(BF16).reshape(nch, c, dk),
                           preferred_element_type=F32)
        decay = jnp.where(causal, jnp.exp(jnp.where(causal, ldiff, 0.0)), 0.0)
        beta3 = beta.reshape(nch, c, 1)
        gc3 = gc.reshape(nch, c, 1)
        glast3 = gc3[:, c - 1:c, :] if d == 0 else gc3[:, 0:1, :]
        kb = k3 * beta3
        a = jnp.einsum('cid,cjd->cij', kb.astype(BF16), k3b, preferred_element_type=F32)
        a = jnp.where(strict, a * decay, 0.0)
        qk = jnp.einsum('cid,cjd->cij', q3b, k3b, preferred_element_type=F32)
        qk = jnp.where(causal, qk * decay, 0.0)
        a_diag = jnp.where(sub_blk, a, 0.0)
        dinv = neg_pow_inverse(-a_diag, 2, DN_SUB)
        n_off = jnp.einsum('cij,cjk->cik', dinv.astype(BF16), (a - a_diag).astype(BF16), preferred_element_type=F32)
        pinv = neg_pow_inverse(-n_off, 2 * DN_SUB, c)
        pinv = jnp.einsum('cij,cjk->cik', pinv.astype(BF16), dinv.astype(BF16), preferred_element_type=F32)
        rhs = jnp.concatenate([v3 * beta3, kb * jnp.exp(gc3)], axis=-1)
        sol = jnp.einsum('cij,cjd->cid', pinv.astype(BF16), rhs.astype(BF16), preferred_element_type=F32)
        solb = sol.astype(BF16)
        qs = jnp.einsum('cij,cjd->cid', qk.astype(BF16), solb, preferred_element_type=F32)
        o_s[d] = qs[:, :, :dk].reshape(t, dk)
        qp_s[d] = (q3 * jnp.exp(gc3) - qs[:, :, dk:]).reshape(t, dk).astype(BF16)
        ke = (k3 * jnp.exp(glast3 - gc3)).astype(BF16)
        solt = jnp.swapaxes(sol, 1, 2).astype(BF16)
        nw2 = jnp.einsum('cdi,cik->cdk', solt, ke, preferred_element_type=F32)
        nt_s[d] = nw2[:, :dk, :]
        w2t_s[d] = nw2[:, dk:, :].astype(BF16)
        el_s[d] = jnp.broadcast_to(jnp.exp(glast3), (nch, 8, dk))

    def chunk_step(d, ci, st):
        r0 = pl.multiple_of(ci * c, c)
        stb = st.astype(BF16)
        o_s[d, pl.ds(r0, c), :] += lax.dot_general(qp_s[d, pl.ds(r0, c), :], stb, (((1,), (1,)), ((), ())),
                                                   preferred_element_type=F32)
        return (st * el_s[d, ci][0:1, :] + nt_s[d, ci]
                - jnp.dot(stb, w2t_s[d, ci], preferred_element_type=F32))

    def ctx_body(n, carry):
        return chunk_step(0, n, carry[0]), chunk_step(1, ncc - 1 - n, carry[1])

    def lat_body(n, carry):
        return chunk_step(0, ncc + n, carry[0]), chunk_step(1, nch - 1 - n, carry[1])

    zero = jnp.zeros((dk, dk), F32)
    carry = lax.fori_loop(0, ncc, ctx_body, (zero, zero))
    lax.fori_loop(0, nch - ncc, lat_body, carry)

    o = o_s[0] + o_s[1]
    on = o * lax.rsqrt(jnp.mean(o * o, axis=-1, keepdims=True) + EPS) * nw_ref[...]
    z = jnp.concatenate([zc_ref[...], zl_ref[...]], axis=0).astype(F32)
    out = (on * _silu(z)).astype(ol_ref.dtype)
    oc_ref[...] = out[:ctx]
    ol_ref[...] = out[ctx:]


def deltanet_mix(p_main, p_ba, conv_w, a_log, dt_bias, norm_w, *, bsz, seq, ctx, u_width):
    n_heads = a_log.shape[-1]
    dk = norm_w.shape[-1]
    c = DN_CHUNK
    t = seq + ctx
    nch = t // c
    assert seq % c == 0 and ctx % c == 0 and u_width % dk == 0 and 4 * n_heads <= p_ba.shape[1]
    cb = u_width // dk
    lat_rows = bsz * seq
    cw = conv_w.astype(F32).reshape(conv_w.shape[0], 3, n_heads, dk).transpose(2, 1, 0, 3)
    lanes = p_ba.shape[1]
    lp = jnp.zeros((2, lanes), F32)
    lp = lp.at[0, 2 * n_heads:4 * n_heads].set(jnp.exp(a_log.astype(F32)).reshape(-1))
    lp = lp.at[1, 2 * n_heads:4 * n_heads].set(dt_bias.astype(F32).reshape(-1))
    cblk = lat_rows // ctx

    def lat_spec(off):
        return pl.BlockSpec((seq, dk), lambda b, h: (b, off + h))

    def ctx_spec(off):
        return pl.BlockSpec((ctx, dk), lambda b, h: (cblk + b, off + h))

    yl, yc = pl.pallas_call(
        functools.partial(_dn_kernel, n_heads=n_heads, ctx=ctx),
        out_shape=(jax.ShapeDtypeStruct((lat_rows, n_heads * dk), BF16),
                   jax.ShapeDtypeStruct((bsz * ctx, n_heads * dk), BF16)),
        grid=(bsz, n_heads),
        in_specs=[
            lat_spec(cb), lat_spec(cb + n_heads), lat_spec(cb + 2 * n_heads),
            ctx_spec(cb), ctx_spec(cb + n_heads), ctx_spec(cb + 2 * n_heads),
            pl.BlockSpec((seq, lanes), lambda b, h: (b, 0)),
            pl.BlockSpec((ctx, lanes), lambda b, h: (cblk + b, 0)),
            lat_spec(cb + 3 * n_heads), ctx_spec(cb + 3 * n_heads),
            pl.BlockSpec((1, 3, conv_w.shape[0], dk), lambda b, h: (h, 0, 0, 0)),
            pl.BlockSpec((2, lanes), lambda b, h: (0, 0)),
            pl.BlockSpec((1, dk), lambda b, h: (0, 0)),
        ],
        out_specs=(pl.BlockSpec((seq, dk), lambda b, h: (b, h)),
                   pl.BlockSpec((ctx, dk), lambda b, h: (b, h))),
        scratch_shapes=[
            pltpu.VMEM((2, nch, dk, dk), F32),
            pltpu.VMEM((2, nch, dk, dk), BF16),
            pltpu.VMEM((2, t, dk), BF16),
            pltpu.VMEM((2, nch, 8, dk), F32),
            pltpu.VMEM((2, t, dk), F32),
            pltpu.VMEM((3, t, lanes), F32),
        ],
        compiler_params=_cparams(2),
        name="deltanet",
    )(p_main, p_main, p_main, p_main, p_main, p_main, p_ba, p_ba, p_main, p_main,
      cw, lp, norm_w.astype(F32).reshape(1, dk))
    return jnp.concatenate([yl, yc], axis=0)


def _store_token_rows(ref2d, x):
    n, d = x.shape
    assert d == ROW_SUB * 2 * LANES, (d, ROW_SUB)
    for m in range(ROW_SUB):
        lo = x[:, 2 * m * LANES:(2 * m + 1) * LANES].astype(BF16).astype(F32)
        hi = x[:, (2 * m + 1) * LANES:(2 * m + 2) * LANES].astype(BF16).astype(F32)
        word = ((lax.bitcast_convert_type(hi, U32) & jnp.uint32(0xFFFF0000))
                | (lax.bitcast_convert_type(lo, U32) >> 16))
        ref2d[pl.ds(m, n, stride=ROW_SUB), :] = word


def _load_token_rows(ref2d, first_token, n_tokens):
    cols = []
    for m in range(ROW_SUB):
        word = ref2d[pl.ds(first_token * ROW_SUB + m, n_tokens, stride=ROW_SUB), :]
        cols.append(lax.bitcast_convert_type(word << 16, F32))
        cols.append(lax.bitcast_convert_type(word & jnp.uint32(0xFFFF0000), F32))
    return jnp.concatenate(cols, axis=1)


def _row_gather(table_hbm, idx_ref, buf, sem, n_rows):
    def copy(r, src_row):
        return pltpu.make_async_copy(table_hbm.at[pl.ds(pl.multiple_of(src_row * ROW_SUB, ROW_SUB), ROW_SUB)],
                                     buf.at[pl.ds(pl.multiple_of(r * ROW_SUB, ROW_SUB), ROW_SUB)], sem)

    def start():
        def body(r, carry):
            copy(r, idx_ref[0, 0, r]).start()
            return carry
        lax.fori_loop(0, n_rows, body, 0, unroll=8)

    def wait():
        def body(r, carry):
            copy(r, 0).wait()
            return carry
        lax.fori_loop(0, n_rows, body, 0, unroll=8)

    return start, wait


def _experts_kernel(be_ref, nxt_ref, nv_ref, tok_ref, tokn_ref, h_hbm, wg_hbm, wu_hbm, wd_hbm, o_ref,
                    wgs, wus, wds, wgb, wub, wdb, xbuf, sem, wsem):
    i = pl.program_id(0)
    nv = nv_ref[0]
    mb = xbuf.shape[1] // ROW_SUB
    slot = i % 2

    def weight_copies(e):
        return (pltpu.make_async_copy(wg_hbm.at[e], wgs, wsem.at[0]),
                pltpu.make_async_copy(wu_hbm.at[e], wus, wsem.at[1]),
                pltpu.make_async_copy(wd_hbm.at[e], wds, wsem.at[2]))

    start_first, _ = _row_gather(h_hbm, tok_ref, xbuf.at[0], sem.at[0], mb)
    start_next, _ = _row_gather(h_hbm, tokn_ref, xbuf.at[1 - slot], sem.at[1 - slot], mb)
    _, wait_cur = _row_gather(h_hbm, tok_ref, xbuf.at[slot], sem.at[slot], mb)

    @pl.when((i == 0) & (nv > 0))
    def _():
        start_first()

    @pl.when(i + 1 < nv)
    def _():
        start_next()

    @pl.when((i == 0) & (nv > 0))
    def _():
        for cp in weight_copies(be_ref[0]):
            cp.start()

    prev = be_ref[jnp.maximum(i - 1, 0)]

    @pl.when((i < nv) & ((i == 0) | (be_ref[i] != prev)))
    def _():
        for cp in weight_copies(be_ref[i]):
            cp.wait()
        wgb[...] = wgs[...].astype(BF16)
        wub[...] = wus[...].astype(BF16)
        wdb[...] = wds[...].astype(BF16)

        @pl.when(nxt_ref[i] >= 0)
        def _():
            for cp in weight_copies(nxt_ref[i]):
                cp.start()

    @pl.when(i < nv)
    def _():
        wait_cur()
        x = _load_token_rows(xbuf.at[slot], 0, mb).astype(BF16)
        g = jnp.dot(x, wgb[...], preferred_element_type=F32)
        u = jnp.dot(x, wub[...], preferred_element_type=F32)
        a = (_silu(g) * u).astype(BF16)
        _store_token_rows(o_ref, jnp.dot(a, wdb[...], preferred_element_type=F32))

    @pl.when(i >= nv)
    def _():
        o_ref[...] = jnp.zeros_like(o_ref)


def routed_experts(h, tok, block_e, next_e, n_valid, w_gate, w_up, w_down):
    d = w_gate.shape[-2]
    f = w_gate.shape[-1]
    mb = MOE_BLOCK
    n_blocks = tok.shape[0] // mb
    lw = LANES
    w_gate, w_up = w_gate.reshape(-1, d, f), w_up.reshape(-1, d, f)
    w_down = w_down.reshape(-1, f, d)
    tok3 = tok.reshape(n_blocks, 1, mb)
    return pl.pallas_call(
        _experts_kernel,
        out_shape=jax.ShapeDtypeStruct((n_blocks * mb * ROW_SUB, lw), U32),
        grid_spec=pltpu.PrefetchScalarGridSpec(
            num_scalar_prefetch=3,
            grid=(n_blocks,),
            in_specs=[
                pl.BlockSpec((1, 1, mb), lambda i, be, nx, nv: (i, 0, 0), memory_space=pltpu.SMEM),
                pl.BlockSpec((1, 1, mb), lambda i, be, nx, nv: (jnp.minimum(i + 1, n_blocks - 1), 0, 0),
                             memory_space=pltpu.SMEM),
                pl.BlockSpec(memory_space=pl.ANY),
                pl.BlockSpec(memory_space=pl.ANY),
                pl.BlockSpec(memory_space=pl.ANY),
                pl.BlockSpec(memory_space=pl.ANY),
            ],
            out_specs=pl.BlockSpec((mb * ROW_SUB, lw), lambda i, be, nx, nv: (i, 0)),
            scratch_shapes=[pltpu.VMEM((d, f), F32), pltpu.VMEM((d, f), F32), pltpu.VMEM((f, d), F32),
                            pltpu.VMEM((d, f), BF16), pltpu.VMEM((d, f), BF16), pltpu.VMEM((f, d), BF16),
                            pltpu.VMEM((2, mb * ROW_SUB, lw), U32), pltpu.SemaphoreType.DMA((2,)),
                            pltpu.SemaphoreType.DMA((3,))],
        ),
        compiler_params=_cparams(1),
        name="routed_experts",
    )(block_e, next_e, n_valid, tok3, tok3, h, w_gate, w_up, w_down)


def _swiglu_kernel(x_ref, wg_ref, wu_ref, wd_ref, o_ref, wgb, wub, wdb):
    @pl.when(pl.program_id(0) == 0)
    def _():
        wgb[...] = wg_ref[...].astype(BF16)
        wub[...] = wu_ref[...].astype(BF16)
        wdb[...] = wd_ref[...].astype(BF16)

    x = x_ref[...].astype(BF16)
    g = jnp.dot(x, wgb[...], preferred_element_type=F32)
    u = jnp.dot(x, wub[...], preferred_element_type=F32)
    a = (_silu(g) * u).astype(BF16)
    o_ref[...] = jnp.dot(a, wdb[...], preferred_element_type=F32).astype(o_ref.dtype)


def shared_expert(x, wg, wu, wd, *, layer, rows, tm=None):
    d = x.shape[1]
    f = wg.shape[-1]
    tm = _row_tile(rows, tm)
    const = lambda i: (0, 0)
    return pl.pallas_call(
        _swiglu_kernel,
        out_shape=jax.ShapeDtypeStruct((rows, d), F32),
        grid=(rows // tm,),
        in_specs=[
            pl.BlockSpec((tm, d), lambda i: (i, 0)),
            _wspec(wg, layer, (d, f), const, pipeline_mode=pl.Buffered(1)),
            _wspec(wu, layer, (d, f), const, pipeline_mode=pl.Buffered(1)),
            _wspec(wd, layer, (f, d), const, pipeline_mode=pl.Buffered(1)),
        ],
        out_specs=pl.BlockSpec((tm, d), lambda i: (i, 0)),
        scratch_shapes=[pltpu.VMEM((d, f), BF16), pltpu.VMEM((d, f), BF16), pltpu.VMEM((f, d), BF16)],
        compiler_params=_cparams(1),
        name="shared_expert",
    )(x, wg, wu, wd)


def _ffn_res_kernel(dst_ref, dstn_ref, y_hbm, w_ref, s_ref, x_ref, nw_ref, g_ref, o_ref, buf, sem, *, top_k):
    i = pl.program_id(0)
    tm = x_ref.shape[0]
    nr = top_k * tm
    slot = i % 2
    start_first, _ = _row_gather(y_hbm, dst_ref, buf.at[0], sem.at[0], nr)
    start_next, _ = _row_gather(y_hbm, dstn_ref, buf.at[1 - slot], sem.at[1 - slot], nr)
    _, wait_cur = _row_gather(y_hbm, dst_ref, buf.at[slot], sem.at[slot], nr)

    @pl.when(i == 0)
    def _():
        start_first()

    @pl.when(i + 1 < pl.num_programs(0))
    def _():
        start_next()

    wait_cur()
    w = w_ref[...]
    f = s_ref[...]
    for k in range(top_k):
        f = f + _load_token_rows(buf.at[slot], k * tm, tm) * w[:, k:k + 1]
    fn = f * lax.rsqrt(jnp.mean(f * f, axis=-1, keepdims=True) + EPS) * nw_ref[...]
    o_ref[...] = x_ref[...] + g_ref[0] * fn


def ffn_residual(y, dest, wts, shared, x, nw, gate, set_of_tile, *, rows, tm=128):
    d = x.shape[1]
    lw = LANES
    top_k = wts.shape[1]
    tm = _row_tile(rows, tm)
    n_tiles = rows // tm
    dst3 = dest.reshape(top_k, n_tiles, tm).transpose(1, 0, 2).reshape(n_tiles, 1, top_k * tm)
    return pl.pallas_call(
        functools.partial(_ffn_res_kernel, top_k=top_k),
        out_shape=jax.ShapeDtypeStruct((rows, d), F32),
        grid=(n_tiles,),
        in_specs=[
            pl.BlockSpec((1, 1, top_k * tm), lambda i: (i, 0, 0), memory_space=pltpu.SMEM),
            pl.BlockSpec((1, 1, top_k * tm), lambda i: (jnp.minimum(i + 1, n_tiles - 1), 0, 0),
                         memory_space=pltpu.SMEM),
            pl.BlockSpec(memory_space=pl.ANY),
            pl.BlockSpec((tm, top_k), lambda i: (i, 0)),
            pl.BlockSpec((tm, d), lambda i: (i, 0)),
            pl.BlockSpec((tm, d), lambda i: (i, 0)),
            pl.BlockSpec((1, d), lambda i: (0, 0)),
            pl.BlockSpec((1, 1, d), lambda i: (set_of_tile(i, tm), 0, 0)),
        ],
        out_specs=pl.BlockSpec((tm, d), lambda i: (i, 0)),
        scratch_shapes=[pltpu.VMEM((2, top_k * tm * ROW_SUB, lw), U32), pltpu.SemaphoreType.DMA((2,))],
        compiler_params=_cparams(1),
        name="ffn_residual",
    )(dst3, dst3, y, wts, shared, x, nw.reshape(1, d), gate)


def _route_kernel(h_ref, wr_ref, b_ref, eidx_ref, wts_ref, rank_ref, cnt_ref, wrb, tri, cnt_s, *,
                  n_groups, topk_groups, top_k):
    n_exp, tm = wr_ref.shape[0], h_ref.shape[0]
    gs = n_exp // n_groups
    ninf = -jnp.inf

    @pl.when(pl.program_id(0) == 0)
    def _():
        wrb[...] = wr_ref[...].astype(BF16)
        tri[...] = (lax.broadcasted_iota(jnp.int32, (tm, tm), 0)
                    < lax.broadcasted_iota(jnp.int32, (tm, tm), 1)).astype(BF16)
        cnt_s[...] = jnp.zeros_like(cnt_s)

    logits = lax.dot_general(wrb[...], h_ref[...].astype(BF16), (((1,), (1,)), ((), ())),
                             preferred_element_type=F32)
    scores = jax.nn.sigmoid(logits)
    sel = scores + b_ref[:, 0:1]
    s3 = sel.reshape(n_groups, gs, tm)
    io3 = lax.broadcasted_iota(jnp.int32, (n_groups, gs, tm), 1)
    m1 = jnp.max(s3, axis=1, keepdims=True)
    i1 = jnp.min(jnp.where(s3 == m1, io3, gs), axis=1, keepdims=True)
    m2 = jnp.max(jnp.where(io3 == i1, ninf, s3), axis=1, keepdims=True)
    gscore = (m1 + m2).reshape(n_groups, tm)
    iog = lax.broadcasted_iota(jnp.int32, (n_groups, tm), 0)
    gsel = jnp.zeros((n_groups, tm), jnp.bool_)
    for _ in range(topk_groups):
        gm = jnp.max(gscore, axis=0, keepdims=True)
        gi = jnp.min(jnp.where(gscore == gm, iog, n_groups), axis=0, keepdims=True)
        hit = iog == gi
        gsel = gsel | hit
        gscore = jnp.where(hit, ninf, gscore)
    x = jnp.where(gsel.reshape(n_groups, 1, tm), s3, ninf).reshape(n_exp, tm)
    ioe = lax.broadcasted_iota(jnp.int32, (n_exp, tm), 0)
    hits = []
    chosen = jnp.zeros((n_exp, tm), jnp.bool_)
    for k in range(top_k):
        m = jnp.max(x, axis=0, keepdims=True)
        idx = jnp.min(jnp.where(x == m, ioe, n_exp), axis=0, keepdims=True)
        hit = ioe == idx
        x = jnp.where(hit, ninf, x)
        chosen = chosen | hit
        hits.append(hit)
        eidx_ref[k:k + 1, :] = idx
    wsel = jnp.where(chosen, scores, 0.0)
    wd = wsel / jnp.sum(wsel, axis=0, keepdims=True) * ROUTE_SCALE
    cm = jnp.where(chosen, 1.0, 0.0)
    rank = jnp.dot(cm.astype(BF16), tri[...], preferred_element_type=F32) + cnt_s[:, 0:1]
    for k in range(top_k):
        wts_ref[k:k + 1, :] = jnp.sum(jnp.where(hits[k], wd, 0.0), axis=0, keepdims=True)
        rank_ref[k:k + 1, :] = jnp.sum(jnp.where(hits[k], rank, 0.0), axis=0, keepdims=True).astype(jnp.int32)
    cnt_s[...] = cnt_s[...] + jnp.sum(cm, axis=1, keepdims=True)
    cnt_ref[...] = cnt_s[...].astype(jnp.int32)


def route(h, w_router, e_bias, *, rows, tm=None):
    d = h.shape[1]
    n_exp = w_router.shape[1]
    tm = _row_tile(rows, tm)
    bias = jnp.broadcast_to(e_bias.astype(F32).reshape(n_exp, 1), (n_exp, LANES))
    kt = lambda i: (0, i)
    eidx, wts, rank, cnt = pl.pallas_call(
        functools.partial(_route_kernel, n_groups=N_GROUPS, topk_groups=TOPK_GROUPS, top_k=TOP_K),
        out_shape=(jax.ShapeDtypeStruct((TOP_K, rows), jnp.int32), jax.ShapeDtypeStruct((TOP_K, rows), F32),
                   jax.ShapeDtypeStruct((TOP_K, rows), jnp.int32), jax.ShapeDtypeStruct((n_exp, LANES), jnp.int32)),
        grid=(rows // tm,),
        in_specs=[
            pl.BlockSpec((tm, d), lambda i: (i, 0)),
            pl.BlockSpec((n_exp, d), lambda i: (0, 0)),
            pl.BlockSpec((n_exp, LANES), lambda i: (0, 0)),
        ],
        out_specs=(pl.BlockSpec((TOP_K, tm), kt), pl.BlockSpec((TOP_K, tm), kt), pl.BlockSpec((TOP_K, tm), kt),
                   pl.BlockSpec((n_exp, LANES), lambda i: (0, 0))),
        scratch_shapes=[pltpu.VMEM((n_exp, d), BF16), pltpu.VMEM((tm, tm), BF16), pltpu.VMEM((n_exp, LANES), F32)],
        compiler_params=_cparams(1),
        name="route",
    )(h, w_router.T, bias)
    return eidx, wts, rank, cnt[:, 0]


def moe_ffn(h, h_packed, w_router, e_bias, w_gate, w_up, w_down, ws_gate, ws_up, ws_down, *, layer, rows):
    n_exp = w_router.shape[1]
    eidx, wts, rank, counts = route(h, w_router, e_bias, rows=rows)
    mb = MOE_BLOCK
    padded = (counts + mb - 1) // mb * mb
    pad_end = jnp.cumsum(padded)
    pad_start = pad_end - padded
    start_of = jnp.sum(jnp.where(eidx[:, :, None] == jnp.arange(n_exp)[None, None, :], pad_start, 0), axis=-1)
    dest = start_of + rank
    n_blocks = (rows * TOP_K + n_exp * (mb - 1) + mb - 1) // mb
    tok = jnp.zeros((n_blocks * mb,), jnp.int32).at[dest.reshape(-1)].set(
        jnp.tile(jnp.arange(rows, dtype=jnp.int32), TOP_K), unique_indices=True)
    n_valid = (pad_end[-1] // mb).astype(jnp.int32)
    starts = jnp.arange(n_blocks, dtype=jnp.int32) * mb
    block_e = jnp.minimum(jnp.sum(starts[:, None] >= pad_end[None, :], axis=1), n_exp - 1).astype(jnp.int32)
    last_e = block_e[jnp.maximum(n_valid - 1, 0)]
    block_e = jnp.where(jnp.arange(n_blocks) < n_valid, block_e, last_e)
    seg_end = pad_end[block_e] // mb
    next_e = jnp.where(seg_end < n_valid, block_e[jnp.minimum(seg_end, n_blocks - 1)] + layer * n_exp, -1)
    y = routed_experts(h_packed, tok, block_e + layer * n_exp, next_e.astype(jnp.int32), n_valid.reshape(1),
                       w_gate, w_up, w_down)
    shared = shared_expert(h, ws_gate, ws_up, ws_down, layer=layer, rows=rows)
    return y, dest, wts.T, shared


def kernel(x, c, ctx, c_ctx, w_mod, b_mod, norm_mix_pre, norm_mix_post, norm_ffn_pre, norm_ffn_post,
           w_in, s5_lam_re, s5_lam_im, s5_log_step, s5_b_re, s5_b_im, s5_c_re, s5_c_im, s5_d, s5_w_glu,
           dn_conv, dn_a_log, dn_dt_bias, dn_norm, w_br_s5, w_br_dn, w_out,
           moe_router, moe_bias, moe_w_gate, moe_w_up, moe_w_down, sh_w_gate, sh_w_up, sh_w_down):
    bsz, seq, d = x.shape
    n_ctx = ctx.shape[1]
    depth = w_mod.shape[0]
    lat_rows, ctx_rows = bsz * seq, bsz * n_ctx
    all_rows = lat_rows + ctx_rows
    s5_width = s5_d.shape[1]
    dn_width = w_br_dn.shape[1]
    n_heads = dn_a_log.shape[-1]
    main_cols = s5_width + 4 * dn_width
    ba_cols = 4 * n_heads
    lanes = 128
    nc = (seq + n_ctx) // S5_CHUNK
    n_levels = max(1, (nc - 1).bit_length())

    def set_of_tile(i, tm):
        return jnp.minimum((i * tm) // seq, bsz)

    xs = jnp.concatenate([x.reshape(lat_rows, d), ctx.reshape(ctx_rows, d)], axis=0)
    n_sets = 8
    cin = jnp.zeros((n_sets, d), F32).at[:bsz].set(_silu(c)).at[bsz].set(_silu(c_ctx))
    for i in range(depth):
        last = i == depth - 1
        rows = lat_rows if last else all_rows
        mods = matmul(cin, w_mod, b_mod[i].reshape(1, -1), layer=i, tm=n_sets, name="mods").reshape(n_sets, 6, 1, d)
        mod = [mods[:, k] for k in range(6)]
        hmix = prenorm(xs, norm_mix_pre[i], mod[0], mod[1], set_of_tile)
        p_main = matmul(hmix, w_in, layer=i, n_cols=main_cols, out_dtype=BF16, name="in_proj")
        p_ba = matmul(hmix, w_in, layer=i, col0=main_cols, n_cols=lanes, tn=lanes, name="in_proj_ba")
        gates = matmul_unaligned(hmix, w_in, layer=i, col0=main_cols + ba_cols, n_cols=2 * d, rows=rows,
                                 name="in_proj_gates")
        ops = s5_operators(s5_lam_re[i], s5_lam_im[i], s5_log_step[i], s5_b_re[i], s5_b_im[i],
                           s5_c_re[i], s5_c_im[i], n_levels)
        y_s5 = s5_mix(p_main[:, :s5_width], ops, s5_d[i], bsz=bsz, seq=seq, ctx=n_ctx)
        y_s5 = s5_glu(y_s5, s5_w_glu, layer=i, rows=rows)
        y_dn = deltanet_mix(p_main, p_ba, dn_conv[i], dn_a_log[i], dn_dt_bias[i], dn_norm[i],
                            bsz=bsz, seq=seq, ctx=n_ctx, u_width=s5_width)
        m = branch_merge(y_s5, y_dn, gates, w_br_s5, w_br_dn, layer=i, rows=rows)
        xs = outproj_residual(m, w_out, xs, norm_mix_post[i], mod[2], set_of_tile, layer=i, rows=rows)
        hffn, hpacked = prenorm(xs, norm_ffn_pre[i], mod[3], mod[4], set_of_tile, rows=rows, packed=True)
        y, dest, wts, shared = moe_ffn(hffn, hpacked, moe_router[i], moe_bias[i], moe_w_gate, moe_w_up, moe_w_down,
                                       sh_w_gate, sh_w_up, sh_w_down, layer=i, rows=rows)
        xs = ffn_residual(y, dest, wts, shared, xs, norm_ffn_post[i], mod[5], set_of_tile, rows=rows)
    return xs[:lat_rows].reshape(bsz, seq, d)
```

```python
import functools

import jax
import jax.numpy as jnp
from jax import lax
from jax.experimental import pallas as pl
from jax.experimental.pallas import tpu as pltpu

F32 = jnp.float32
BF16 = jnp.bfloat16

EPS = 1e-6
GRID_W = 64
S5_CHUNK = 16
DN_CHUNK = 64
DN_SUB = 16
TOP_K = 8
N_GROUPS = 8
TOPK_GROUPS = 4
ROUTE_SCALE = 2.5
MOE_BLOCK = 256

V7X_VMEM_LIMIT = 56 * 1024 * 1024
LANES = 128
ROW_TILE = 1024
ROW_SUB = 8
U32 = jnp.uint32


def _cparams(n_axes, vmem=V7X_VMEM_LIMIT):
    return pltpu.CompilerParams(dimension_semantics=("arbitrary",) * n_axes, vmem_limit_bytes=vmem)


def _silu(x):
    return x * jax.nn.sigmoid(x)


def _row_tile(rows, tm=None):
    tm = min(tm or ROW_TILE, ROW_TILE, rows)
    assert rows % tm == 0, (rows, tm)
    return tm


def _col_tile(n, tn, col0=0):
    tn = min(tn, n)
    while n % tn or col0 % tn:
        tn -= LANES
    return tn


def _wspec(w, layer, block, index_map, **kw):
    if w.ndim == len(block):
        return pl.BlockSpec(block, index_map, **kw)
    return pl.BlockSpec((None,) + tuple(block), lambda *a: (layer,) + tuple(index_map(*a)), **kw)


def _mm_kernel(x_ref, w_ref, b_ref, o_ref, wbf_ref):
    @pl.when(pl.program_id(1) == 0)
    def _():
        wbf_ref[...] = w_ref[...].astype(BF16)

    acc = jnp.dot(x_ref[...].astype(BF16), wbf_ref[...], preferred_element_type=F32)
    o_ref[...] = (acc + b_ref[...]).astype(o_ref.dtype)


def matmul(x, w, bias=None, *, layer=0, n_cols=None, col0=0, out_dtype=F32, tm=None, tn=1024, rows=None,
           name="matmul"):
    m, k = x.shape
    rows = m if rows is None else rows
    n_cols = w.shape[-1] - col0 if n_cols is None else n_cols
    tm = _row_tile(rows, tm)
    tn = _col_tile(n_cols, tn, col0)
    assert rows % tm == 0 and n_cols % tn == 0 and col0 % tn == 0, (rows, tm, n_cols, tn, col0)
    if bias is None:
        bias = jnp.zeros((1, n_cols), F32)
    cb0 = col0 // tn
    return pl.pallas_call(
        _mm_kernel,
        out_shape=jax.ShapeDtypeStruct((rows, n_cols), out_dtype),
        grid=(n_cols // tn, rows // tm),
        in_specs=[
            pl.BlockSpec((tm, k), lambda j, i: (i, 0)),
            _wspec(w, layer, (k, tn), lambda j, i: (0, j + cb0)),
            pl.BlockSpec((1, tn), lambda j, i: (0, j)),
        ],
        out_specs=pl.BlockSpec((tm, tn), lambda j, i: (i, j)),
        scratch_shapes=[pltpu.VMEM((k, tn), BF16)],
        compiler_params=_cparams(2),
        name=name,
    )(x, w, bias)


def _mm_shifted_kernel(x_ref, wa_ref, wb_ref, o_ref, wbf_ref, *, shift):
    @pl.when(pl.program_id(1) == 0)
    def _():
        tn = wbf_ref.shape[1]
        w = jnp.concatenate([wa_ref[...], wb_ref[...]], axis=1)
        wbf_ref[...] = w[:, shift:shift + tn].astype(BF16)

    o_ref[...] = jnp.dot(x_ref[...], wbf_ref[...], preferred_element_type=F32).astype(o_ref.dtype)


def matmul_unaligned(x, w, *, layer, col0, n_cols, out_dtype=BF16, rows=None, tm=None, tn=512, name="matmul_unaligned"):
    m, k = x.shape
    rows = m if rows is None else rows
    tm = _row_tile(rows, tm)
    tn = _col_tile(n_cols, tn)
    base, shift = col0 // tn, col0 % tn
    return pl.pallas_call(
        functools.partial(_mm_shifted_kernel, shift=shift),
        out_shape=jax.ShapeDtypeStruct((rows, n_cols), out_dtype),
        grid=(n_cols // tn, rows // tm),
        in_specs=[
            pl.BlockSpec((tm, k), lambda j, i: (i, 0)),
            _wspec(w, layer, (k, tn), lambda j, i: (0, base + j)),
            _wspec(w, layer, (k, tn), lambda j, i: (0, base + j + 1)),
        ],
        out_specs=pl.BlockSpec((tm, tn), lambda j, i: (i, j)),
        scratch_shapes=[pltpu.VMEM((k, tn), BF16)],
        compiler_params=_cparams(2),
        name=name,
    )(x, w, w)


def _prenorm_kernel(x_ref, w_ref, sh_ref, sc_ref, o_ref, *packed_ref):
    x = x_ref[...]
    y = x * lax.rsqrt(jnp.mean(x * x, axis=-1, keepdims=True) + EPS) * w_ref[...]
    h = y * (1.0 + sc_ref[0]) + sh_ref[0]
    o_ref[...] = h.astype(o_ref.dtype)
    if packed_ref:
        _store_token_rows(packed_ref[0], h)


def prenorm(x, w, shift, scale, set_of_tile, *, rows=None, tm=None, packed=False):
    m, d = x.shape
    rows = m if rows is None else rows
    tm = _row_tile(rows, tm)
    out_shape = [jax.ShapeDtypeStruct((rows, d), BF16)]
    out_specs = [pl.BlockSpec((tm, d), lambda i: (i, 0))]
    if packed:
        out_shape.append(jax.ShapeDtypeStruct((rows * ROW_SUB, LANES), U32))
        out_specs.append(pl.BlockSpec((tm * ROW_SUB, LANES), lambda i: (i, 0)))
    out = pl.pallas_call(
        _prenorm_kernel,
        out_shape=out_shape,
        grid=(rows // tm,),
        in_specs=[
            pl.BlockSpec((tm, d), lambda i: (i, 0)),
            pl.BlockSpec((1, d), lambda i: (0, 0)),
            pl.BlockSpec((1, 1, d), lambda i: (set_of_tile(i, tm), 0, 0)),
            pl.BlockSpec((1, 1, d), lambda i: (set_of_tile(i, tm), 0, 0)),
        ],
        out_specs=out_specs,
        compiler_params=_cparams(1),
        name="prenorm",
    )(x, w.reshape(1, d), shift, scale)
    return out if packed else out[0]


def _glu_kernel(y_ref, w_ref, o_ref, wbf_ref, *, tn):
    @pl.when(pl.program_id(1) == 0)
    def _():
        wbf_ref[...] = w_ref[...].astype(BF16)

    j = pl.program_id(0)
    g = jax.nn.gelu(y_ref[...].astype(F32))
    acc = jnp.dot(g.astype(BF16), wbf_ref[...], preferred_element_type=F32)
    gj = jax.nn.gelu(y_ref[:, pl.ds(pl.multiple_of(j * tn, tn), tn)].astype(F32))
    o_ref[...] = (gj * jax.nn.sigmoid(acc)).astype(o_ref.dtype)


def s5_glu(y, w, *, layer, rows, tm=None, tn=512):
    m, k = y.shape
    tm = _row_tile(rows, tm)
    tn = _col_tile(k, tn)
    return pl.pallas_call(
        functools.partial(_glu_kernel, tn=tn),
        out_shape=jax.ShapeDtypeStruct((rows, k), BF16),
        grid=(k // tn, rows // tm),
        in_specs=[
            pl.BlockSpec((tm, k), lambda j, i: (i, 0)),
            _wspec(w, layer, (k, tn), lambda j, i: (0, j)),
        ],
        out_specs=pl.BlockSpec((tm, tn), lambda j, i: (i, j)),
        scratch_shapes=[pltpu.VMEM((k, tn), BF16)],
        compiler_params=_cparams(2),
        name="s5_glu",
    )(y, w)


def _merge_kernel(a_ref, b_ref, ga_ref, gb_ref, wa_ref, wb_ref, o_ref, wabf_ref, wbbf_ref):
    @pl.when(pl.program_id(1) == 0)
    def _():
        wabf_ref[...] = wa_ref[...].astype(BF16)
        wbbf_ref[...] = wb_ref[...].astype(BF16)

    ya = jnp.dot(a_ref[...], wabf_ref[...], preferred_element_type=F32)
    yb = jnp.dot(b_ref[...], wbbf_ref[...], preferred_element_type=F32)
    m = jax.nn.sigmoid(ga_ref[...].astype(F32)) * ya + jax.nn.sigmoid(gb_ref[...].astype(F32)) * yb
    o_ref[...] = m.astype(o_ref.dtype)


def branch_merge(ya, yb, gates, wa, wb, *, layer, rows, tm=None, tn=512):
    ka, kb = ya.shape[1], yb.shape[1]
    d = wa.shape[-1]
    tm = _row_tile(rows, tm)
    tn = _col_tile(d, tn)
    nb = d // tn
    return pl.pallas_call(
        _merge_kernel,
        out_shape=jax.ShapeDtypeStruct((rows, d), BF16),
        grid=(nb, rows // tm),
        in_specs=[
            pl.BlockSpec((tm, ka), lambda j, i: (i, 0)),
            pl.BlockSpec((tm, kb), lambda j, i: (i, 0)),
            pl.BlockSpec((tm, tn), lambda j, i: (i, j)),
            pl.BlockSpec((tm, tn), lambda j, i: (i, j + nb)),
            _wspec(wa, layer, (ka, tn), lambda j, i: (0, j)),
            _wspec(wb, layer, (kb, tn), lambda j, i: (0, j)),
        ],
        out_specs=pl.BlockSpec((tm, tn), lambda j, i: (i, j)),
        scratch_shapes=[pltpu.VMEM((ka, tn), BF16), pltpu.VMEM((kb, tn), BF16)],
        compiler_params=_cparams(2),
        name="branch_merge",
    )(ya, yb, gates, gates, wa, wb)


def _outproj_kernel(m_ref, w_ref, x_ref, nw_ref, g_ref, o_ref, wbf_ref):
    @pl.when(pl.program_id(0) == 0)
    def _():
        wbf_ref[...] = w_ref[...].astype(BF16)

    y = jnp.dot(m_ref[...], wbf_ref[...], preferred_element_type=F32)
    yn = y * lax.rsqrt(jnp.mean(y * y, axis=-1, keepdims=True) + EPS) * nw_ref[...]
    o_ref[...] = x_ref[...] + g_ref[0] * yn


def outproj_residual(m, w, x, nw, gate, set_of_tile, *, layer, rows, tm=512):
    d = w.shape[-1]
    k = w.shape[-2]
    tm = _row_tile(rows, tm)
    return pl.pallas_call(
        _outproj_kernel,
        out_shape=jax.ShapeDtypeStruct((rows, d), F32),
        grid=(rows // tm,),
        in_specs=[
            pl.BlockSpec((tm, k), lambda i: (i, 0)),
            _wspec(w, layer, (k, d), lambda i: (0, 0), pipeline_mode=pl.Buffered(1)),
            pl.BlockSpec((tm, d), lambda i: (i, 0)),
            pl.BlockSpec((1, d), lambda i: (0, 0)),
            pl.BlockSpec((1, 1, d), lambda i: (set_of_tile(i, tm), 0, 0)),
        ],
        out_specs=pl.BlockSpec((tm, d), lambda i: (i, 0)),
        scratch_shapes=[pltpu.VMEM((k, d), BF16)],
        compiler_params=_cparams(1),
        name="outproj_residual",
    )(m, w, x, nw.reshape(1, d), gate)


def s5_operators(lam_re, lam_im, log_step, b_re, b_im, c_re, c_im, n_levels):
    tc = S5_CHUNK
    hp = lax.Precision.HIGHEST
    lr, li = lam_re.astype(F32), lam_im.astype(F32)
    step = jnp.exp(log_step.astype(F32))[..., None]

    def apow(l):
        mag = jnp.exp(lr * step * l)
        return mag * jnp.cos(li * step * l), mag * jnp.sin(li * step * l)

    ab_re, ab_im = apow(1.0)
    den = lr * lr + li * li
    nr = ab_re - 1.0
    cr = (nr * lr + ab_im * li) / den
    ci = (ab_im * lr - nr * li) / den
    br, bi = b_re.astype(F32), b_im.astype(F32)
    bb_re = cr[..., None] * br - ci[..., None] * bi
    bb_im = cr[..., None] * bi + ci[..., None] * br
    lags = jnp.arange(tc + 1, dtype=F32)[:, None, None, None]
    pw_re, pw_im = apow(lags)
    pw_re, pw_im = jnp.moveaxis(pw_re, 0, 2), jnp.moveaxis(pw_im, 0, 2)
    cre, cim = c_re.astype(F32), c_im.astype(F32)
    cp_re = cre[:, :, None] * pw_re[:, :, :, None] - cim[:, :, None] * pw_im[:, :, :, None]
    cp_im = cre[:, :, None] * pw_im[:, :, :, None] + cim[:, :, None] * pw_re[:, :, :, None]
    bbt = jnp.concatenate([bb_re.transpose(0, 1, 3, 2), -bb_im.transpose(0, 1, 3, 2)], axis=-1)
    cps = jnp.concatenate([cp_re[:, :, :tc], cp_im[:, :, :tc]], axis=-1)
    cps = jnp.stack([cps[0], cps[1, :, ::-1]])
    cps = cps.transpose(0, 1, 4, 2, 3)
    pr, pi = pw_re[:, :, tc - 1 - jnp.arange(tc)], pw_im[:, :, tc - 1 - jnp.arange(tc)]
    win_re = pr[:, :, :, None, :] * bb_re.transpose(0, 1, 3, 2)[:, :, None] - pi[:, :, :, None, :] * bb_im.transpose(0, 1, 3, 2)[:, :, None]
    win_im = pr[:, :, :, None, :] * bb_im.transpose(0, 1, 3, 2)[:, :, None] + pi[:, :, :, None, :] * bb_re.transpose(0, 1, 3, 2)[:, :, None]
    win = jnp.concatenate([win_re, win_im], axis=-1)
    wo_re = cp_re[:, :, 1:].transpose(0, 1, 4, 2, 3)
    wo_im = -cp_im[:, :, 1:].transpose(0, 1, 4, 2, 3)
    wout = jnp.concatenate([wo_re, wo_im], axis=2)
    win = jnp.stack([win[0], win[1, :, ::-1]])
    wout = jnp.stack([wout[0], wout[1, :, :, ::-1]])
    g = lr.shape[1]
    j = br.shape[-1]
    p = lr.shape[-1]
    rows = []
    for k in range(n_levels):
        ar, ai = apow(float(tc * 2 ** k))
        rows.append(jnp.concatenate([ar, ar], axis=-1))
        rows.append(jnp.concatenate([-ai, ai], axis=-1))
    apw = jnp.stack(rows, axis=2)
    return ((bbt.transpose(1, 0, 2, 3), cps.reshape(2, g, 2 * p, tc * j).transpose(1, 0, 2, 3)),
            win.reshape(2, g, tc * j, 2 * p).transpose(1, 0, 2, 3).astype(BF16),
            wout.reshape(2, g, 2 * p, tc * j).transpose(1, 0, 2, 3).astype(BF16),
            apw.transpose(1, 0, 2, 3))


def _s5_kernel(u_ref, bbt_ref, cps_ref, win_ref, wout_ref, apw_ref, dsk_ref, y_ref, *, nb, ctx_rows, n_levels, p):
    u = u_ref[0]
    n, lanes = u.shape
    jw = bbt_ref.shape[2]
    tc = lanes // jw
    row = lax.broadcasted_iota(jnp.int32, (n, 1), 0)
    lane = lax.broadcasted_iota(jnp.int32, (jw, lanes), 1)
    y = u.astype(F32) * dsk_ref[0]
    for d in range(2):
        kt = jnp.dot(bbt_ref[0, d], cps_ref[0, d], precision=lax.Precision.HIGHEST, preferred_element_type=F32)
        blocks = []
        for s in range(tc):
            sh = jw * s if d == 0 else jw * (tc - 1 - s)
            if sh == 0:
                blocks.append(kt)
            elif d == 0:
                blocks.append(jnp.where(lane >= sh, pltpu.roll(kt, sh, 1), 0.0))
            else:
                blocks.append(jnp.where(lane < lanes - sh, pltpu.roll(kt, lanes - sh, 1), 0.0))
        tmat = jnp.concatenate(blocks, axis=0).astype(BF16)
        y = y + jnp.dot(u, tmat, preferred_element_type=F32)
        x = jnp.dot(u, win_ref[0, d], preferred_element_type=F32)
        if d == 0:
            def shift(a, s):
                return jnp.where(row >= s, pltpu.roll(a, s, 0), 0.0)
        else:
            if ctx_rows:
                x = pltpu.roll(x, n - ctx_rows, 0)

            def shift(a, s):
                return jnp.where(row < n - s, pltpu.roll(a, n - s, 0), 0.0)
        x = shift(x, nb)
        for k in range(n_levels):
            sh = shift(x, nb * 2 ** k)
            a1 = apw_ref[0, d, 2 * k:2 * k + 1, :]
            a2 = apw_ref[0, d, 2 * k + 1:2 * k + 2, :]
            x = x + a1 * sh + a2 * pltpu.roll(sh, p, 1)
        if d == 1 and ctx_rows:
            x = pltpu.roll(x, ctx_rows, 0)
        y = y + jnp.dot(x.astype(BF16), wout_ref[0, d], preferred_element_type=F32)
    y_ref[0] = y.astype(y_ref.dtype)


def s5_scan(uc, toep, win, wout, apw, dsk, *, nb, ctx_rows, n_levels):
    g, n, lanes = uc.shape
    p2 = win.shape[-1]
    bbt, cps = toep
    jw = bbt.shape[2]
    return pl.pallas_call(
        functools.partial(_s5_kernel, nb=nb, ctx_rows=ctx_rows, n_levels=n_levels, p=p2 // 2),
        out_shape=jax.ShapeDtypeStruct((g, n, lanes), BF16),
        grid=(g,),
        in_specs=[
            pl.BlockSpec((1, n, lanes), lambda i: (i, 0, 0)),
            pl.BlockSpec((1, 2, jw, p2), lambda i: (i, 0, 0, 0)),
            pl.BlockSpec((1, 2, p2, lanes), lambda i: (i, 0, 0, 0)),
            pl.BlockSpec((1, 2, lanes, p2), lambda i: (i, 0, 0, 0)),
            pl.BlockSpec((1, 2, p2, lanes), lambda i: (i, 0, 0, 0)),
            pl.BlockSpec((1, 2, 2 * n_levels, p2), lambda i: (i, 0, 0, 0)),
            pl.BlockSpec((1, 1, lanes), lambda i: (i, 0, 0)),
        ],
        out_specs=pl.BlockSpec((1, n, lanes), lambda i: (i, 0, 0)),
        compiler_params=_cparams(1),
        name="s5_scan",
    )(uc, bbt, cps, win, wout, apw, dsk)


def s5_mix(u_rows, ops, d_skip, *, bsz, seq, ctx):
    toep, win, wout, apw = ops
    g = win.shape[0]
    lanes = win.shape[2]
    tc = S5_CHUNK
    j = lanes // tc
    hgt = seq // GRID_W
    ul = u_rows[:bsz * seq].reshape(bsz, hgt, GRID_W, g, j).transpose(0, 2, 1, 3, 4).reshape(bsz, seq, g, j)
    uc = u_rows[bsz * seq:].reshape(bsz, ctx, g, j)
    useq = jnp.concatenate([uc, ul], axis=1)
    nc = (seq + ctx) // tc
    uch = useq.reshape(bsz, nc, tc, g, j).transpose(3, 1, 0, 2, 4).reshape(g, nc * bsz, lanes)
    n_levels = apw.shape[2] // 2
    dsk = jnp.tile(d_skip.astype(F32).reshape(g, 1, j), (1, tc, 1)).reshape(g, 1, lanes)
    ych = s5_scan(uch.astype(BF16), toep, win, wout, apw, dsk, nb=bsz, ctx_rows=(ctx // tc) * bsz, n_levels=n_levels)
    yseq = ych.reshape(g, nc, bsz, tc, j).transpose(2, 1, 3, 0, 4).reshape(bsz, seq + ctx, g * j)
    yc = yseq[:, :ctx].reshape(bsz * ctx, g * j)
    yl = yseq[:, ctx:].reshape(bsz, GRID_W, hgt, g * j).transpose(0, 2, 1, 3).reshape(bsz * seq, g * j)
    return jnp.concatenate([yl, yc], axis=0)


def _softplus(x):
    return jnp.maximum(x, 0.0) + jnp.log1p(jnp.exp(-jnp.abs(x)))


def _dn_kernel(ql_ref, kl_ref, vl_ref, qc_ref, kc_ref, vc_ref, bal_ref, bac_ref, zl_ref, zc_ref,
               cw_ref, lp_ref, nw_ref, ol_ref, oc_ref,
               nt_s, w2t_s, qp_s, el_s, o_s, tok_s, *, n_heads, ctx):
    c = DN_CHUNK
    seq = ql_ref.shape[0]
    t = seq + ctx
    dk = ql_ref.shape[1]
    nch = t // c
    ncc = ctx // c
    h = pl.program_id(1)
    row = lax.broadcasted_iota(jnp.int32, (t, 1), 0)
    rowc = row % c
    lane = lax.broadcasted_iota(jnp.int32, (1, dk), 1)

    first = (row == 0) | (row == ctx)
    last = (row == ctx - 1) | (row == t - 1)

    def conv_silu(xc_ref, xl_ref, kind):
        x = jnp.concatenate([xc_ref[...], xl_ref[...]], axis=0).astype(F32)
        xp = jnp.where(first, 0.0, pltpu.roll(x, 1, 0))
        xn = jnp.where(last, 0.0, pltpu.roll(x, t - 1, 0))
        w = cw_ref[0, kind]
        return _silu(xp * w[0:1] + x * w[1:2] + xn * w[2:3])

    def l2n(x):
        return x * lax.rsqrt(jnp.sum(x * x, axis=-1, keepdims=True) + EPS)

    q = l2n(conv_silu(qc_ref, ql_ref, 0)) * (dk ** -0.5)
    k = l2n(conv_silu(kc_ref, kl_ref, 1))
    v = conv_silu(vc_ref, vl_ref, 2)

    @pl.when(h == 0)
    def _():
        ba = jnp.concatenate([bac_ref[...], bal_ref[...]], axis=0)
        g_all = -lp_ref[0:1, :] * _softplus(ba + lp_ref[1:2, :])
        pf, sf = g_all, g_all
        s = 1
        while s < c:
            pf = pf + jnp.where(rowc >= s, pltpu.roll(pf, s, 0), 0.0)
            sf = sf + jnp.where(rowc < c - s, pltpu.roll(sf, t - s, 0), 0.0)
            s *= 2
        tok_s[0] = jax.nn.sigmoid(ba)
        tok_s[1] = pf
        tok_s[2] = sf

    beta_all, pf, sf = tok_s[0], tok_s[1], tok_s[2]

    def col(a, idx):
        return jnp.sum(jnp.where(lane == idx, a, 0.0), axis=1, keepdims=True)

    causal_f = (lax.broadcasted_iota(jnp.int32, (c, c), 0) >= lax.broadcasted_iota(jnp.int32, (c, c), 1))[None]
    strict_f = (lax.broadcasted_iota(jnp.int32, (c, c), 0) > lax.broadcasted_iota(jnp.int32, (c, c), 1))[None]
    causal_b = (lax.broadcasted_iota(jnp.int32, (c, c), 0) <= lax.broadcasted_iota(jnp.int32, (c, c), 1))[None]
    strict_b = (lax.broadcasted_iota(jnp.int32, (c, c), 0) < lax.broadcasted_iota(jnp.int32, (c, c), 1))[None]
    eye = (lax.broadcasted_iota(jnp.int32, (c, c), 0) == lax.broadcasted_iota(jnp.int32, (c, c), 1))[None].astype(F32)

    sub_blk = (lax.broadcasted_iota(jnp.int32, (c, c), 0) // DN_SUB
               == lax.broadcasted_iota(jnp.int32, (c, c), 1) // DN_SUB)[None]

    def neg_pow_inverse(x, m, limit):
        pinv = eye + x
        while m < limit:
            xb = x.astype(BF16)
            x = jnp.einsum('cij,cjk->cik', xb, xb, preferred_element_type=F32)
            pinv = pinv + jnp.einsum('cij,cjk->cik', pinv.astype(BF16), x.astype(BF16), preferred_element_type=F32)
            m *= 2
        return pinv

    q3 = q.reshape(nch, c, dk)
    k3 = k.reshape(nch, c, dk)
    v3 = v.reshape(nch, c, dk)
    k3b = k3.astype(BF16)
    q3b = q3.astype(BF16)
    for d in range(2):
        causal, strict = (causal_f, strict_f) if d == 0 else (causal_b, strict_b)
        beta = col(beta_all, d * n_heads + h)
        gc = col(pf if d == 0 else sf, (2 + d) * n_heads + h)
        hi = gc.astype(BF16).astype(F32)
        mid = (gc - hi).astype(BF16).astype(F32)
        lo = gc - hi - mid
        pieces = (hi, mid, lo)
        g1 = jnp.zeros((t, dk), F32)
        g2 = jnp.zeros((t, dk), F32)
        for n_p, piece in enumerate(pieces):
            pb = jnp.broadcast_to(piece, (t, dk))
            g1 = jnp.where(lane == n_p, pb, jnp.where(lane == 3 + n_p, 1.0, g1))
            g2 = jnp.where(lane == n_p, 1.0, jnp.where(lane == 3 + n_p, -pb, g2))
        ldiff = jnp.einsum('cid,cjd->cij', g1.astype(BF16).reshape(nch, c, dk), g2.astype(BF16).reshape(nch, c, dk),
                           preferred_element_type=F32)
        decay = jnp.where(causal, jnp.exp(jnp.where(causal, ldiff, 0.0)), 0.0)
        beta3 = beta.reshape(nch, c, 1)
        gc3 = gc.reshape(nch, c, 1)
        glast3 = gc3[:, c - 1:c, :] if d == 0 else gc3[:, 0:1, :]
        kb = k3 * beta3
        a = jnp.einsum('cid,cjd->cij', kb.astype(BF16), k3b, preferred_element_type=F32)
        a = jnp.where(strict, a * decay, 0.0)
        qk = jnp.einsum('cid,cjd->cij', q3b, k3b, preferred_element_type=F32)
        qk = jnp.where(causal, qk * decay, 0.0)
        a_diag = jnp.where(sub_blk, a, 0.0)
        dinv = neg_pow_inverse(-a_diag, 2, DN_SUB)
        n_off = jnp.einsum('cij,cjk->cik', dinv.astype(BF16), (a - a_diag).astype(BF16), preferred_element_type=F32)
        pinv = neg_pow_inverse(-n_off, 2 * DN_SUB, c)
        pinv = jnp.einsum('cij,cjk->cik', pinv.astype(BF16), dinv.astype(BF16), preferred_element_type=F32)
        rhs = jnp.concatenate([v3 * beta3, kb * jnp.exp(gc3)], axis=-1)
        sol = jnp.einsum('cij,cjd->cid', pinv.astype(BF16), rhs.astype(BF16), preferred_element_type=F32)
        solb = sol.astype(BF16)
        qs = jnp.einsum('cij,cjd->cid', qk.astype(BF16), solb, preferred_element_type=F32)
        o_s[d] = qs[:, :, :dk].reshape(t, dk)
        qp_s[d] = (q3 * jnp.exp(gc3) - qs[:, :, dk:]).reshape(t, dk).astype(BF16)
        ke = (k3 * jnp.exp(glast3 - gc3)).astype(BF16)
        solt = jnp.swapaxes(sol, 1, 2).astype(BF16)
        nw2 = jnp.einsum('cdi,cik->cdk', solt, ke, preferred_element_type=F32)
        nt_s[d] = nw2[:, :dk, :]
        w2t_s[d] = nw2[:, dk:, :].astype(BF16)
        el_s[d] = jnp.broadcast_to(jnp.exp(glast3), (nch, 8, dk))

    def chunk_step(d, ci, st):
        r0 = pl.multiple_of(ci * c, c)
        stb = st.astype(BF16)
        o_s[d, pl.ds(r0, c), :] += lax.dot_general(qp_s[d, pl.ds(r0, c), :], stb, (((1,), (1,)), ((), ())),
                                                   preferred_element_type=F32)
        return (st * el_s[d, ci][0:1, :] + nt_s[d, ci]
                - jnp.dot(stb, w2t_s[d, ci], preferred_element_type=F32))

    def ctx_body(n, carry):
        return chunk_step(0, n, carry[0]), chunk_step(1, ncc - 1 - n, carry[1])

    def lat_body(n, carry):
        return chunk_step(0, ncc + n, carry[0]), chunk_step(1, nch - 1 - n, carry[1])

    zero = jnp.zeros((dk, dk), F32)
    carry = lax.fori_loop(0, ncc, ctx_body, (zero, zero))
    lax.fori_loop(0, nch - ncc, lat_body, carry)

    o = o_s[0] + o_s[1]
    on = o * lax.rsqrt(jnp.mean(o * o, axis=-1, keepdims=True) + EPS) * nw_ref[...]
    z = jnp.concatenate([zc_ref[...], zl_ref[...]], axis=0).astype(F32)
    out = (on * _silu(z)).astype(ol_ref.dtype)
    oc_ref[...] = out[:ctx]
    ol_ref[...] = out[ctx:]


def deltanet_mix(p_main, p_ba, conv_w, a_log, dt_bias, norm_w, *, bsz, seq, ctx, u_width):
    n_heads = a_log.shape[-1]
    dk = norm_w.shape[-1]
    c = DN_CHUNK
    t = seq + ctx
    nch = t // c
    assert seq % c == 0 and ctx % c == 0 and u_width % dk == 0 and 4 * n_heads <= p_ba.shape[1]
    cb = u_width // dk
    lat_rows = bsz * seq
    cw = conv_w.astype(F32).reshape(conv_w.shape[0], 3, n_heads, dk).transpose(2, 1, 0, 3)
    lanes = p_ba.shape[1]
    lp = jnp.zeros((2, lanes), F32)
    lp = lp.at[0, 2 * n_heads:4 * n_heads].set(jnp.exp(a_log.astype(F32)).reshape(-1))
    lp = lp.at[1, 2 * n_heads:4 * n_heads].set(dt_bias.astype(F32).reshape(-1))
    cblk = lat_rows // ctx

    def lat_spec(off):
        return pl.BlockSpec((seq, dk), lambda b, h: (b, off + h))

    def ctx_spec(off):
        return pl.BlockSpec((ctx, dk), lambda b, h: (cblk + b, off + h))

    yl, yc = pl.pallas_call(
        functools.partial(_dn_kernel, n_heads=n_heads, ctx=ctx),
        out_shape=(jax.ShapeDtypeStruct((lat_rows, n_heads * dk), BF16),
                   jax.ShapeDtypeStruct((bsz * ctx, n_heads * dk), BF16)),
        grid=(bsz, n_heads),
        in_specs=[
            lat_spec(cb), lat_spec(cb + n_heads), lat_spec(cb + 2 * n_heads),
            ctx_spec(cb), ctx_spec(cb + n_heads), ctx_spec(cb + 2 * n_heads),
            pl.BlockSpec((seq, lanes), lambda b, h: (b, 0)),
            pl.BlockSpec((ctx, lanes), lambda b, h: (cblk + b, 0)),
            lat_spec(cb + 3 * n_heads), ctx_spec(cb + 3 * n_heads),
            pl.BlockSpec((1, 3, conv_w.shape[0], dk), lambda b, h: (h, 0, 0, 0)),
            pl.BlockSpec((2, lanes), lambda b, h: (0, 0)),
            pl.BlockSpec((1, dk), lambda b, h: (0, 0)),
        ],
        out_specs=(pl.BlockSpec((seq, dk), lambda b, h: (b, h)),
                   pl.BlockSpec((ctx, dk), lambda b, h: (b, h))),
        scratch_shapes=[
            pltpu.VMEM((2, nch, dk, dk), F32),
            pltpu.VMEM((2, nch, dk, dk), BF16),
            pltpu.VMEM((2, t, dk), BF16),
            pltpu.VMEM((2, nch, 8, dk), F32),
            pltpu.VMEM((2, t, dk), F32),
            pltpu.VMEM((3, t, lanes), F32),
        ],
        compiler_params=_cparams(2),
        name="deltanet",
    )(p_main, p_main, p_main, p_main, p_main, p_main, p_ba, p_ba, p_main, p_main,
      cw, lp, norm_w.astype(F32).reshape(1, dk))
    return jnp.concatenate([yl, yc], axis=0)


def _store_token_rows(ref2d, x):
    n, d = x.shape
    assert d == ROW_SUB * 2 * LANES, (d, ROW_SUB)
    for m in range(ROW_SUB):
        lo = x[:, 2 * m * LANES:(2 * m + 1) * LANES].astype(BF16).astype(F32)
        hi = x[:, (2 * m + 1) * LANES:(2 * m + 2) * LANES].astype(BF16).astype(F32)
        word = ((lax.bitcast_convert_type(hi, U32) & jnp.uint32(0xFFFF0000))
                | (lax.bitcast_convert_type(lo, U32) >> 16))
        ref2d[pl.ds(m, n, stride=ROW_SUB), :] = word


def _load_token_rows(ref2d, first_token, n_tokens):
    cols = []
    for m in range(ROW_SUB):
        word = ref2d[pl.ds(first_token * ROW_SUB + m, n_tokens, stride=ROW_SUB), :]
        cols.append(lax.bitcast_convert_type(word << 16, F32))
        cols.append(lax.bitcast_convert_type(word & jnp.uint32(0xFFFF0000), F32))
    return jnp.concatenate(cols, axis=1)


def _row_gather(table_hbm, idx_ref, buf, sem, n_rows):
    def copy(r, src_row):
        return pltpu.make_async_copy(table_hbm.at[pl.ds(pl.multiple_of(src_row * ROW_SUB, ROW_SUB), ROW_SUB)],
                                     buf.at[pl.ds(pl.multiple_of(r * ROW_SUB, ROW_SUB), ROW_SUB)], sem)

    def start():
        def body(r, carry):
            copy(r, idx_ref[0, 0, r]).start()
            return carry
        lax.fori_loop(0, n_rows, body, 0, unroll=8)

    def wait():
        def body(r, carry):
            copy(r, 0).wait()
            return carry
        lax.fori_loop(0, n_rows, body, 0, unroll=8)

    return start, wait


def _dispatch_kernel(cnt_ref, ps_ref, pe_ref, dst_ref, h_ref, xg_hbm, zrow, sem, *, top_k):
    i = pl.program_id(0)
    tm = h_ref.shape[0] // ROW_SUB
    n_exp = cnt_ref.shape[0]

    def row_copy(t, dst_row):
        return pltpu.make_async_copy(h_ref.at[pl.ds(pl.multiple_of(t * ROW_SUB, ROW_SUB), ROW_SUB)],
                                     xg_hbm.at[pl.ds(pl.multiple_of(dst_row * ROW_SUB, ROW_SUB), ROW_SUB)], sem.at[0])

    for k in range(top_k):
        def start(t, carry, k=k):
            row_copy(t, dst_ref[0, 0, k * tm + t]).start()
            return carry
        lax.fori_loop(0, tm, start, 0, unroll=8)

    def wait(r, carry):
        row_copy(0, 0).wait()
        return carry

    lax.fori_loop(0, top_k * tm, wait, 0, unroll=8)

    @pl.when(i == pl.num_programs(0) - 1)
    def _():
        zrow[...] = jnp.zeros_like(zrow)

        def pad_copy(s):
            return pltpu.make_async_copy(zrow, xg_hbm.at[pl.ds(pl.multiple_of(s * ROW_SUB, ROW_SUB), ROW_SUB)],
                                         sem.at[1])

        def zero_fill(lo, hi):
            lax.fori_loop(lo, hi, lambda s, c: (pad_copy(s).start(), c)[1], 0)
            lax.fori_loop(lo, hi, lambda s, c: (pad_copy(s).wait(), c)[1], 0)

        def per_expert(e, carry):
            zero_fill(ps_ref[e] + cnt_ref[e], pe_ref[e])
            return carry

        lax.fori_loop(0, n_exp, per_expert, 0)
        zero_fill(pe_ref[n_exp - 1], xg_hbm.shape[0] // ROW_SUB)


def dispatch_rows(h_packed, dest, counts, pad_start, pad_end, *, n_slots, rows, tm=128):
    top_k = dest.shape[0]
    tm = _row_tile(rows, tm)
    n_tiles = rows // tm
    dst3 = dest.reshape(top_k, n_tiles, tm).transpose(1, 0, 2).reshape(n_tiles, 1, top_k * tm)
    return pl.pallas_call(
        functools.partial(_dispatch_kernel, top_k=top_k),
        out_shape=jax.ShapeDtypeStruct((n_slots * ROW_SUB, LANES), U32),
        grid_spec=pltpu.PrefetchScalarGridSpec(
            num_scalar_prefetch=3,
            grid=(n_tiles,),
            in_specs=[
                pl.BlockSpec((1, 1, top_k * tm), lambda i, c, s, e: (i, 0, 0), memory_space=pltpu.SMEM),
                pl.BlockSpec((tm * ROW_SUB, LANES), lambda i, c, s, e: (i, 0)),
            ],
            out_specs=pl.BlockSpec(memory_space=pl.ANY),
            scratch_shapes=[pltpu.VMEM((ROW_SUB, LANES), U32), pltpu.SemaphoreType.DMA((2,))],
        ),
        compiler_params=_cparams(1),
        name="dispatch_rows",
    )(counts, pad_start, pad_end, dst3, h_packed)


def _experts_kernel(be_ref, nxt_ref, nv_ref, x_ref, wg_hbm, wu_hbm, wd_hbm, o_ref,
                    wgs, wus, wds, wgb, wub, wdb, wsem):
    i = pl.program_id(0)
    nv = nv_ref[0]
    mb = x_ref.shape[0] // ROW_SUB

    def weight_copies(e):
        return (pltpu.make_async_copy(wg_hbm.at[e], wgs, wsem.at[0]),
                pltpu.make_async_copy(wu_hbm.at[e], wus, wsem.at[1]),
                pltpu.make_async_copy(wd_hbm.at[e], wds, wsem.at[2]))

    @pl.when((i == 0) & (nv > 0))
    def _():
        for cp in weight_copies(be_ref[0]):
            cp.start()

    prev = be_ref[jnp.maximum(i - 1, 0)]

    @pl.when((i < nv) & ((i == 0) | (be_ref[i] != prev)))
    def _():
        for cp in weight_copies(be_ref[i]):
            cp.wait()
        wgb[...] = wgs[...].astype(BF16)
        wub[...] = wus[...].astype(BF16)
        wdb[...] = wds[...].astype(BF16)

        @pl.when(nxt_ref[i] >= 0)
        def _():
            for cp in weight_copies(nxt_ref[i]):
                cp.start()

    @pl.when(i < nv)
    def _():
        x = _load_token_rows(x_ref, 0, mb).astype(BF16)
        g = jnp.dot(x, wgb[...], preferred_element_type=F32)
        u = jnp.dot(x, wub[...], preferred_element_type=F32)
        a = (_silu(g) * u).astype(BF16)
        _store_token_rows(o_ref, jnp.dot(a, wdb[...], preferred_element_type=F32))

    @pl.when(i >= nv)
    def _():
        o_ref[...] = jnp.zeros_like(o_ref)


def routed_experts(xg, block_e, next_e, n_valid, w_gate, w_up, w_down):
    d = w_gate.shape[-2]
    f = w_gate.shape[-1]
    mb = MOE_BLOCK
    n_blocks = xg.shape[0] // (mb * ROW_SUB)
    lw = LANES
    w_gate, w_up = w_gate.reshape(-1, d, f), w_up.reshape(-1, d, f)
    w_down = w_down.reshape(-1, f, d)
    return pl.pallas_call(
        _experts_kernel,
        out_shape=jax.ShapeDtypeStruct((n_blocks * mb * ROW_SUB, lw), U32),
        grid_spec=pltpu.PrefetchScalarGridSpec(
            num_scalar_prefetch=3,
            grid=(n_blocks,),
            in_specs=[
                pl.BlockSpec((mb * ROW_SUB, lw), lambda i, be, nx, nv: (jnp.minimum(i, jnp.maximum(nv[0] - 1, 0)), 0)),
                pl.BlockSpec(memory_space=pl.ANY),
                pl.BlockSpec(memory_space=pl.ANY),
                pl.BlockSpec(memory_space=pl.ANY),
            ],
            out_specs=pl.BlockSpec((mb * ROW_SUB, lw), lambda i, be, nx, nv: (i, 0)),
            scratch_shapes=[pltpu.VMEM((d, f), F32), pltpu.VMEM((d, f), F32), pltpu.VMEM((f, d), F32),
                            pltpu.VMEM((d, f), BF16), pltpu.VMEM((d, f), BF16), pltpu.VMEM((f, d), BF16),
                            pltpu.SemaphoreType.DMA((3,))],
        ),
        compiler_params=_cparams(1),
        name="routed_experts",
    )(block_e, next_e, n_valid, xg, w_gate, w_up, w_down)


def _swiglu_kernel(x_ref, wg_ref, wu_ref, wd_ref, o_ref, wgb, wub, wdb):
    @pl.when(pl.program_id(0) == 0)
    def _():
        wgb[...] = wg_ref[...].astype(BF16)
        wub[...] = wu_ref[...].astype(BF16)
        wdb[...] = wd_ref[...].astype(BF16)

    x = x_ref[...].astype(BF16)
    g = jnp.dot(x, wgb[...], preferred_element_type=F32)
    u = jnp.dot(x, wub[...], preferred_element_type=F32)
    a = (_silu(g) * u).astype(BF16)
    o_ref[...] = jnp.dot(a, wdb[...], preferred_element_type=F32).astype(o_ref.dtype)


def shared_expert(x, wg, wu, wd, *, layer, rows, tm=None):
    d = x.shape[1]
    f = wg.shape[-1]
    tm = _row_tile(rows, tm)
    const = lambda i: (0, 0)
    return pl.pallas_call(
        _swiglu_kernel,
        out_shape=jax.ShapeDtypeStruct((rows, d), F32),
        grid=(rows // tm,),
        in_specs=[
            pl.BlockSpec((tm, d), lambda i: (i, 0)),
            _wspec(wg, layer, (d, f), const, pipeline_mode=pl.Buffered(1)),
            _wspec(wu, layer, (d, f), const, pipeline_mode=pl.Buffered(1)),
            _wspec(wd, layer, (f, d), const, pipeline_mode=pl.Buffered(1)),
        ],
        out_specs=pl.BlockSpec((tm, d), lambda i: (i, 0)),
        scratch_shapes=[pltpu.VMEM((d, f), BF16), pltpu.VMEM((d, f), BF16), pltpu.VMEM((f, d), BF16)],
        compiler_params=_cparams(1),
        name="shared_expert",
    )(x, wg, wu, wd)


def _ffn_res_kernel(dst_ref, dstn_ref, y_hbm, w_ref, s_ref, x_ref, nw_ref, g_ref, o_ref, buf, sem, *, top_k):
    i = pl.program_id(0)
    tm = x_ref.shape[0]
    nr = top_k * tm
    slot = i % 2
    start_first, _ = _row_gather(y_hbm, dst_ref, buf.at[0], sem.at[0], nr)
    start_next, _ = _row_gather(y_hbm, dstn_ref, buf.at[1 - slot], sem.at[1 - slot], nr)
    _, wait_cur = _row_gather(y_hbm, dst_ref, buf.at[slot], sem.at[slot], nr)

    @pl.when(i == 0)
    def _():
        start_first()

    @pl.when(i + 1 < pl.num_programs(0))
    def _():
        start_next()

    wait_cur()
    w = w_ref[...]
    f = s_ref[...]
    for k in range(top_k):
        f = f + _load_token_rows(buf.at[slot], k * tm, tm) * w[:, k:k + 1]
    fn = f * lax.rsqrt(jnp.mean(f * f, axis=-1, keepdims=True) + EPS) * nw_ref[...]
    o_ref[...] = x_ref[...] + g_ref[0] * fn


def ffn_residual(y, dest, wts, shared, x, nw, gate, set_of_tile, *, rows, tm=128):
    d = x.shape[1]
    lw = LANES
    top_k = wts.shape[1]
    tm = _row_tile(rows, tm)
    n_tiles = rows // tm
    dst3 = dest.reshape(top_k, n_tiles, tm).transpose(1, 0, 2).reshape(n_tiles, 1, top_k * tm)
    return pl.pallas_call(
        functools.partial(_ffn_res_kernel, top_k=top_k),
        out_shape=jax.ShapeDtypeStruct((rows, d), F32),
        grid=(n_tiles,),
        in_specs=[
            pl.BlockSpec((1, 1, top_k * tm), lambda i: (i, 0, 0), memory_space=pltpu.SMEM),
            pl.BlockSpec((1, 1, top_k * tm), lambda i: (jnp.minimum(i + 1, n_tiles - 1), 0, 0),
                         memory_space=pltpu.SMEM),
            pl.BlockSpec(memory_space=pl.ANY),
            pl.BlockSpec((tm, top_k), lambda i: (i, 0)),
            pl.BlockSpec((tm, d), lambda i: (i, 0)),
            pl.BlockSpec((tm, d), lambda i: (i, 0)),
            pl.BlockSpec((1, d), lambda i: (0, 0)),
            pl.BlockSpec((1, 1, d), lambda i: (set_of_tile(i, tm), 0, 0)),
        ],
        out_specs=pl.BlockSpec((tm, d), lambda i: (i, 0)),
        scratch_shapes=[pltpu.VMEM((2, top_k * tm * ROW_SUB, lw), U32), pltpu.SemaphoreType.DMA((2,))],
        compiler_params=_cparams(1),
        name="ffn_residual",
    )(dst3, dst3, y, wts, shared, x, nw.reshape(1, d), gate)


def _route_kernel(h_ref, wr_ref, b_ref, eidx_ref, wts_ref, rank_ref, cnt_ref, wrb, tri, cnt_s, *,
                  n_groups, topk_groups, top_k):
    n_exp, tm = wr_ref.shape[0], h_ref.shape[0]
    gs = n_exp // n_groups
    ninf = -jnp.inf

    @pl.when(pl.program_id(0) == 0)
    def _():
        wrb[...] = wr_ref[...].astype(BF16)
        tri[...] = (lax.broadcasted_iota(jnp.int32, (tm, tm), 0)
                    < lax.broadcasted_iota(jnp.int32, (tm, tm), 1)).astype(BF16)
        cnt_s[...] = jnp.zeros_like(cnt_s)

    logits = lax.dot_general(wrb[...], h_ref[...].astype(BF16), (((1,), (1,)), ((), ())),
                             preferred_element_type=F32)
    scores = jax.nn.sigmoid(logits)
    sel = scores + b_ref[:, 0:1]
    s3 = sel.reshape(n_groups, gs, tm)
    io3 = lax.broadcasted_iota(jnp.int32, (n_groups, gs, tm), 1)
    m1 = jnp.max(s3, axis=1, keepdims=True)
    i1 = jnp.min(jnp.where(s3 == m1, io3, gs), axis=1, keepdims=True)
    m2 = jnp.max(jnp.where(io3 == i1, ninf, s3), axis=1, keepdims=True)
    gscore = (m1 + m2).reshape(n_groups, tm)
    iog = lax.broadcasted_iota(jnp.int32, (n_groups, tm), 0)
    gsel = jnp.zeros((n_groups, tm), jnp.bool_)
    for _ in range(topk_groups):
        gm = jnp.max(gscore, axis=0, keepdims=True)
        gi = jnp.min(jnp.where(gscore == gm, iog, n_groups), axis=0, keepdims=True)
        hit = iog == gi
        gsel = gsel | hit
        gscore = jnp.where(hit, ninf, gscore)
    x = jnp.where(gsel.reshape(n_groups, 1, tm), s3, ninf).reshape(n_exp, tm)
    ioe = lax.broadcasted_iota(jnp.int32, (n_exp, tm), 0)
    hits = []
    chosen = jnp.zeros((n_exp, tm), jnp.bool_)
    for k in range(top_k):
        m = jnp.max(x, axis=0, keepdims=True)
        idx = jnp.min(jnp.where(x == m, ioe, n_exp), axis=0, keepdims=True)
        hit = ioe == idx
        x = jnp.where(hit, ninf, x)
        chosen = chosen | hit
        hits.append(hit)
        eidx_ref[k:k + 1, :] = idx
    wsel = jnp.where(chosen, scores, 0.0)
    wd = wsel / jnp.sum(wsel, axis=0, keepdims=True) * ROUTE_SCALE
    cm = jnp.where(chosen, 1.0, 0.0)
    rank = jnp.dot(cm.astype(BF16), tri[...], preferred_element_type=F32) + cnt_s[:, 0:1]
    for k in range(top_k):
        wts_ref[k:k + 1, :] = jnp.sum(jnp.where(hits[k], wd, 0.0), axis=0, keepdims=True)
        rank_ref[k:k + 1, :] = jnp.sum(jnp.where(hits[k], rank, 0.0), axis=0, keepdims=True).astype(jnp.int32)
    cnt_s[...] = cnt_s[...] + jnp.sum(cm, axis=1, keepdims=True)
    cnt_ref[...] = cnt_s[...].astype(jnp.int32)


def route(h, w_router, e_bias, *, rows, tm=None):
    d = h.shape[1]
    n_exp = w_router.shape[1]
    tm = _row_tile(rows, tm)
    bias = jnp.broadcast_to(e_bias.astype(F32).reshape(n_exp, 1), (n_exp, LANES))
    kt = lambda i: (0, i)
    eidx, wts, rank, cnt = pl.pallas_call(
        functools.partial(_route_kernel, n_groups=N_GROUPS, topk_groups=TOPK_GROUPS, top_k=TOP_K),
        out_shape=(jax.ShapeDtypeStruct((TOP_K, rows), jnp.int32), jax.ShapeDtypeStruct((TOP_K, rows), F32),
                   jax.ShapeDtypeStruct((TOP_K, rows), jnp.int32), jax.ShapeDtypeStruct((n_exp, LANES), jnp.int32)),
        grid=(rows // tm,),
        in_specs=[
            pl.BlockSpec((tm, d), lambda i: (i, 0)),
            pl.BlockSpec((n_exp, d), lambda i: (0, 0)),
            pl.BlockSpec((n_exp, LANES), lambda i: (0, 0)),
        ],
        out_specs=(pl.BlockSpec((TOP_K, tm), kt), pl.BlockSpec((TOP_K, tm), kt), pl.BlockSpec((TOP_K, tm), kt),
                   pl.BlockSpec((n_exp, LANES), lambda i: (0, 0))),
        scratch_shapes=[pltpu.VMEM((n_exp, d), BF16), pltpu.VMEM((tm, tm), BF16), pltpu.VMEM((n_exp, LANES), F32)],
        compiler_params=_cparams(1),
        name="route",
    )(h, w_router.T, bias)
    return eidx, wts, rank, cnt[:, 0]


def moe_ffn(h, h_packed, w_router, e_bias, w_gate, w_up, w_down, ws_gate, ws_up, ws_down, *, layer, rows):
    n_exp = w_router.shape[1]
    eidx, wts, rank, counts = route(h, w_router, e_bias, rows=rows)
    mb = MOE_BLOCK
    padded = (counts + mb - 1) // mb * mb
    pad_end = jnp.cumsum(padded)
    pad_start = pad_end - padded
    start_of = jnp.sum(jnp.where(eidx[:, :, None] == jnp.arange(n_exp)[None, None, :], pad_start, 0), axis=-1)
    dest = start_of + rank
    n_blocks = (rows * TOP_K + n_exp * (mb - 1) + mb - 1) // mb
    xg = dispatch_rows(h_packed, dest, counts, pad_start, pad_end, n_slots=n_blocks * mb, rows=rows)
    n_valid = (pad_end[-1] // mb).astype(jnp.int32)
    starts = jnp.arange(n_blocks, dtype=jnp.int32) * mb
    block_e = jnp.minimum(jnp.sum(starts[:, None] >= pad_end[None, :], axis=1), n_exp - 1).astype(jnp.int32)
    last_e = block_e[jnp.maximum(n_valid - 1, 0)]
    block_e = jnp.where(jnp.arange(n_blocks) < n_valid, block_e, last_e)
    seg_end = pad_end[block_e] // mb
    next_e = jnp.where(seg_end < n_valid, block_e[jnp.minimum(seg_end, n_blocks - 1)] + layer * n_exp, -1)
    y = routed_experts(xg, block_e + layer * n_exp, next_e.astype(jnp.int32), n_valid.reshape(1),
                       w_gate, w_up, w_down)
    shared = shared_expert(h, ws_gate, ws_up, ws_down, layer=layer, rows=rows)
    return y, dest, wts.T, shared


def kernel(x, c, ctx, c_ctx, w_mod, b_mod, norm_mix_pre, norm_mix_post, norm_ffn_pre, norm_ffn_post,
           w_in, s5_lam_re, s5_lam_im, s5_log_step, s5_b_re, s5_b_im, s5_c_re, s5_c_im, s5_d, s5_w_glu,
           dn_conv, dn_a_log, dn_dt_bias, dn_norm, w_br_s5, w_br_dn, w_out,
           moe_router, moe_bias, moe_w_gate, moe_w_up, moe_w_down, sh_w_gate, sh_w_up, sh_w_down):
    bsz, seq, d = x.shape
    n_ctx = ctx.shape[1]
    depth = w_mod.shape[0]
    lat_rows, ctx_rows = bsz * seq, bsz * n_ctx
    all_rows = lat_rows + ctx_rows
    s5_width = s5_d.shape[1]
    dn_width = w_br_dn.shape[1]
    n_heads = dn_a_log.shape[-1]
    main_cols = s5_width + 4 * dn_width
    ba_cols = 4 * n_heads
    lanes = 128
    nc = (seq + n_ctx) // S5_CHUNK
    n_levels = max(1, (nc - 1).bit_length())

    def set_of_tile(i, tm):
        return jnp.minimum((i * tm) // seq, bsz)

    xs = jnp.concatenate([x.reshape(lat_rows, d), ctx.reshape(ctx_rows, d)], axis=0)
    n_sets = 8
    cin = jnp.zeros((n_sets, d), F32).at[:bsz].set(_silu(c)).at[bsz].set(_silu(c_ctx))
    for i in range(depth):
        last = i == depth - 1
        rows = lat_rows if last else all_rows
        mods = matmul(cin, w_mod, b_mod[i].reshape(1, -1), layer=i, tm=n_sets, name="mods").reshape(n_sets, 6, 1, d)
        mod = [mods[:, k] for k in range(6)]
        hmix = prenorm(xs, norm_mix_pre[i], mod[0], mod[1], set_of_tile)
        p_main = matmul(hmix, w_in, layer=i, n_cols=main_cols, out_dtype=BF16, name="in_proj")
        p_ba = matmul(hmix, w_in, layer=i, col0=main_cols, n_cols=lanes, tn=lanes, name="in_proj_ba")
        gates = matmul_unaligned(hmix, w_in, layer=i, col0=main_cols + ba_cols, n_cols=2 * d, rows=rows,
                                 name="in_proj_gates")
        ops = s5_operators(s5_lam_re[i], s5_lam_im[i], s5_log_step[i], s5_b_re[i], s5_b_im[i],
                           s5_c_re[i], s5_c_im[i], n_levels)
        y_s5 = s5_mix(p_main[:, :s5_width], ops, s5_d[i], bsz=bsz, seq=seq, ctx=n_ctx)
        y_s5 = s5_glu(y_s5, s5_w_glu, layer=i, rows=rows)
        y_dn = deltanet_mix(p_main, p_ba, dn_conv[i], dn_a_log[i], dn_dt_bias[i], dn_norm[i],
                            bsz=bsz, seq=seq, ctx=n_ctx, u_width=s5_width)
        m = branch_merge(y_s5, y_dn, gates, w_br_s5, w_br_dn, layer=i, rows=rows)
        xs = outproj_residual(m, w_out, xs, norm_mix_post[i], mod[2], set_of_tile, layer=i, rows=rows)
        hffn, hpacked = prenorm(xs, norm_ffn_pre[i], mod[3], mod[4], set_of_tile, rows=rows, packed=True)
        y, dest, wts, shared = moe_ffn(hffn, hpacked, moe_router[i], moe_bias[i], moe_w_gate, moe_w_up, moe_w_down,
                                       sh_w_gate, sh_w_up, sh_w_down, layer=i, rows=rows)
        xs = ffn_residual(y, dest, wts, shared, xs, norm_ffn_post[i], mod[5], set_of_tile, rows=rows)
    return xs[:lat_rows].reshape(bsz, seq, d)
```

```python
import functools

import jax
import jax.numpy as jnp
from jax import lax
from jax.experimental import pallas as pl
from jax.experimental.pallas import tpu as pltpu

F32 = jnp.float32
BF16 = jnp.bfloat16

EPS = 1e-6
GRID_W = 64
S5_CHUNK = 16
DN_CHUNK = 64
DN_SUB = 16
TOP_K = 8
N_GROUPS = 8
TOPK_GROUPS = 4
ROUTE_SCALE = 2.5
MOE_BLOCK = 256

V7X_VMEM_LIMIT = 56 * 1024 * 1024
LANES = 128
ROW_TILE = 1024
ROW_SUB = 8
U32 = jnp.uint32


def _cparams(n_axes, vmem=V7X_VMEM_LIMIT):
    return pltpu.CompilerParams(dimension_semantics=("arbitrary",) * n_axes, vmem_limit_bytes=vmem)


def _silu(x):
    return x * jax.nn.sigmoid(x)


def _row_tile(rows, tm=None):
    tm = min(tm or ROW_TILE, ROW_TILE, rows)
    assert rows % tm == 0, (rows, tm)
    return tm


def _col_tile(n, tn, col0=0):
    tn = min(tn, n)
    while n % tn or col0 % tn:
        tn -= LANES
    return tn


def _wspec(w, layer, block, index_map, **kw):
    if w.ndim == len(block):
        return pl.BlockSpec(block, index_map, **kw)
    return pl.BlockSpec((None,) + tuple(block), lambda *a: (layer,) + tuple(index_map(*a)), **kw)


def _mm_kernel(x_ref, w_ref, b_ref, o_ref, wbf_ref):
    @pl.when(pl.program_id(1) == 0)
    def _():
        wbf_ref[...] = w_ref[...].astype(BF16)

    acc = jnp.dot(x_ref[...].astype(BF16), wbf_ref[...], preferred_element_type=F32)
    o_ref[...] = (acc + b_ref[...]).astype(o_ref.dtype)


def matmul(x, w, bias=None, *, layer=0, n_cols=None, col0=0, out_dtype=F32, tm=None, tn=1024, rows=None,
           name="matmul"):
    m, k = x.shape
    rows = m if rows is None else rows
    n_cols = w.shape[-1] - col0 if n_cols is None else n_cols
    tm = _row_tile(rows, tm)
    tn = _col_tile(n_cols, tn, col0)
    assert rows % tm == 0 and n_cols % tn == 0 and col0 % tn == 0, (rows, tm, n_cols, tn, col0)
    if bias is None:
        bias = jnp.zeros((1, n_cols), F32)
    cb0 = col0 // tn
    return pl.pallas_call(
        _mm_kernel,
        out_shape=jax.ShapeDtypeStruct((rows, n_cols), out_dtype),
        grid=(n_cols // tn, rows // tm),
        in_specs=[
            pl.BlockSpec((tm, k), lambda j, i: (i, 0)),
            _wspec(w, layer, (k, tn), lambda j, i: (0, j + cb0)),
            pl.BlockSpec((1, tn), lambda j, i: (0, j)),
        ],
        out_specs=pl.BlockSpec((tm, tn), lambda j, i: (i, j)),
        scratch_shapes=[pltpu.VMEM((k, tn), BF16)],
        compiler_params=_cparams(2),
        name=name,
    )(x, w, bias)


def _mm_shifted_kernel(x_ref, wa_ref, wb_ref, o_ref, wbf_ref, *, shift):
    @pl.when(pl.program_id(1) == 0)
    def _():
        tn = wbf_ref.shape[1]
        w = jnp.concatenate([wa_ref[...], wb_ref[...]], axis=1)
        wbf_ref[...] = w[:, shift:shift + tn].astype(BF16)

    o_ref[...] = jnp.dot(x_ref[...], wbf_ref[...], preferred_element_type=F32).astype(o_ref.dtype)


def matmul_unaligned(x, w, *, layer, col0, n_cols, out_dtype=BF16, rows=None, tm=None, tn=512, name="matmul_unaligned"):
    m, k = x.shape
    rows = m if rows is None else rows
    tm = _row_tile(rows, tm)
    tn = _col_tile(n_cols, tn)
    base, shift = col0 // tn, col0 % tn
    return pl.pallas_call(
        functools.partial(_mm_shifted_kernel, shift=shift),
        out_shape=jax.ShapeDtypeStruct((rows, n_cols), out_dtype),
        grid=(n_cols // tn, rows // tm),
        in_specs=[
            pl.BlockSpec((tm, k), lambda j, i: (i, 0)),
            _wspec(w, layer, (k, tn), lambda j, i: (0, base + j)),
            _wspec(w, layer, (k, tn), lambda j, i: (0, base + j + 1)),
        ],
        out_specs=pl.BlockSpec((tm, tn), lambda j, i: (i, j)),
        scratch_shapes=[pltpu.VMEM((k, tn), BF16)],
        compiler_params=_cparams(2),
        name=name,
    )(x, w, w)


def _prenorm_kernel(x_ref, w_ref, sh_ref, sc_ref, o_ref, *packed_ref):
    x = x_ref[...]
    y = x * lax.rsqrt(jnp.mean(x * x, axis=-1, keepdims=True) + EPS) * w_ref[...]
    h = y * (1.0 + sc_ref[0]) + sh_ref[0]
    o_ref[...] = h.astype(o_ref.dtype)
    if packed_ref:
        _store_token_rows(packed_ref[0], h)


def prenorm(x, w, shift, scale, set_of_tile, *, rows=None, tm=None, packed=False):
    m, d = x.shape
    rows = m if rows is None else rows
    tm = _row_tile(rows, tm)
    out_shape = [jax.ShapeDtypeStruct((rows, d), BF16)]
    out_specs = [pl.BlockSpec((tm, d), lambda i: (i, 0))]
    if packed:
        out_shape.append(jax.ShapeDtypeStruct((rows * ROW_SUB, LANES), U32))
        out_specs.append(pl.BlockSpec((tm * ROW_SUB, LANES), lambda i: (i, 0)))
    out = pl.pallas_call(
        _prenorm_kernel,
        out_shape=out_shape,
        grid=(rows // tm,),
        in_specs=[
            pl.BlockSpec((tm, d), lambda i: (i, 0)),
            pl.BlockSpec((1, d), lambda i: (0, 0)),
            pl.BlockSpec((1, 1, d), lambda i: (set_of_tile(i, tm), 0, 0)),
            pl.BlockSpec((1, 1, d), lambda i: (set_of_tile(i, tm), 0, 0)),
        ],
        out_specs=out_specs,
        compiler_params=_cparams(1),
        name="prenorm",
    )(x, w.reshape(1, d), shift, scale)
    return out if packed else out[0]


def _glu_kernel(y_ref, w_ref, o_ref, wbf_ref, *, tn):
    @pl.when(pl.program_id(1) == 0)
    def _():
        wbf_ref[...] = w_ref[...].astype(BF16)

    j = pl.program_id(0)
    g = jax.nn.gelu(y_ref[...].astype(F32))
    acc = jnp.dot(g.astype(BF16), wbf_ref[...], preferred_element_type=F32)
    gj = jax.nn.gelu(y_ref[:, pl.ds(pl.multiple_of(j * tn, tn), tn)].astype(F32))
    o_ref[...] = (gj * jax.nn.sigmoid(acc)).astype(o_ref.dtype)


def s5_glu(y, w, *, layer, rows, tm=None, tn=512):
    m, k = y.shape
    tm = _row_tile(rows, tm)
    tn = _col_tile(k, tn)
    return pl.pallas_call(
        functools.partial(_glu_kernel, tn=tn),
        out_shape=jax.ShapeDtypeStruct((rows, k), BF16),
        grid=(k // tn, rows // tm),
        in_specs=[
            pl.BlockSpec((tm, k), lambda j, i: (i, 0)),
            _wspec(w, layer, (k, tn), lambda j, i: (0, j)),
        ],
        out_specs=pl.BlockSpec((tm, tn), lambda j, i: (i, j)),
        scratch_shapes=[pltpu.VMEM((k, tn), BF16)],
        compiler_params=_cparams(2),
        name="s5_glu",
    )(y, w)


def _merge_kernel(a_ref, b_ref, ga_ref, gb_ref, wa_ref, wb_ref, o_ref, wabf_ref, wbbf_ref):
    @pl.when(pl.program_id(1) == 0)
    def _():
        wabf_ref[...] = wa_ref[...].astype(BF16)
        wbbf_ref[...] = wb_ref[...].astype(BF16)

    ya = jnp.dot(a_ref[...], wabf_ref[...], preferred_element_type=F32)
    yb = jnp.dot(b_ref[...], wbbf_ref[...], preferred_element_type=F32)
    m = jax.nn.sigmoid(ga_ref[...].astype(F32)) * ya + jax.nn.sigmoid(gb_ref[...].astype(F32)) * yb
    o_ref[...] = m.astype(o_ref.dtype)


def branch_merge(ya, yb, gates, wa, wb, *, layer, rows, tm=None, tn=512):
    ka, kb = ya.shape[1], yb.shape[1]
    d = wa.shape[-1]
    tm = _row_tile(rows, tm)
    tn = _col_tile(d, tn)
    nb = d // tn
    return pl.pallas_call(
        _merge_kernel,
        out_shape=jax.ShapeDtypeStruct((rows, d), BF16),
        grid=(nb, rows // tm),
        in_specs=[
            pl.BlockSpec((tm, ka), lambda j, i: (i, 0)),
            pl.BlockSpec((tm, kb), lambda j, i: (i, 0)),
            pl.BlockSpec((tm, tn), lambda j, i: (i, j)),
            pl.BlockSpec((tm, tn), lambda j, i: (i, j + nb)),
            _wspec(wa, layer, (ka, tn), lambda j, i: (0, j)),
            _wspec(wb, layer, (kb, tn), lambda j, i: (0, j)),
        ],
        out_specs=pl.BlockSpec((tm, tn), lambda j, i: (i, j)),
        scratch_shapes=[pltpu.VMEM((ka, tn), BF16), pltpu.VMEM((kb, tn), BF16)],
        compiler_params=_cparams(2),
        name="branch_merge",
    )(ya, yb, gates, gates, wa, wb)


def _outproj_kernel(m_ref, w_ref, x_ref, nw_ref, g_ref, o_ref, wbf_ref):
    @pl.when(pl.program_id(0) == 0)
    def _():
        wbf_ref[...] = w_ref[...].astype(BF16)

    y = jnp.dot(m_ref[...], wbf_ref[...], preferred_element_type=F32)
    yn = y * lax.rsqrt(jnp.mean(y * y, axis=-1, keepdims=True) + EPS) * nw_ref[...]
    o_ref[...] = x_ref[...] + g_ref[0] * yn


def outproj_residual(m, w, x, nw, gate, set_of_tile, *, layer, rows, tm=512):
    d = w.shape[-1]
    k = w.shape[-2]
    tm = _row_tile(rows, tm)
    return pl.pallas_call(
        _outproj_kernel,
        out_shape=jax.ShapeDtypeStruct((rows, d), F32),
        grid=(rows // tm,),
        in_specs=[
            pl.BlockSpec((tm, k), lambda i: (i, 0)),
            _wspec(w, layer, (k, d), lambda i: (0, 0), pipeline_mode=pl.Buffered(1)),
            pl.BlockSpec((tm, d), lambda i: (i, 0)),
            pl.BlockSpec((1, d), lambda i: (0, 0)),
            pl.BlockSpec((1, 1, d), lambda i: (set_of_tile(i, tm), 0, 0)),
        ],
        out_specs=pl.BlockSpec((tm, d), lambda i: (i, 0)),
        scratch_shapes=[pltpu.VMEM((k, d), BF16)],
        compiler_params=_cparams(1),
        name="outproj_residual",
    )(m, w, x, nw.reshape(1, d), gate)


def s5_operators(lam_re, lam_im, log_step, b_re, b_im, c_re, c_im, n_levels):
    tc = S5_CHUNK
    hp = lax.Precision.HIGHEST
    lr, li = lam_re.astype(F32), lam_im.astype(F32)
    step = jnp.exp(log_step.astype(F32))[..., None]

    def apow(l):
        mag = jnp.exp(lr * step * l)
        return mag * jnp.cos(li * step * l), mag * jnp.sin(li * step * l)

    ab_re, ab_im = apow(1.0)
    den = lr * lr + li * li
    nr = ab_re - 1.0
    cr = (nr * lr + ab_im * li) / den
    ci = (ab_im * lr - nr * li) / den
    br, bi = b_re.astype(F32), b_im.astype(F32)
    bb_re = cr[..., None] * br - ci[..., None] * bi
    bb_im = cr[..., None] * bi + ci[..., None] * br
    lags = jnp.arange(tc + 1, dtype=F32)[:, None, None, None]
    pw_re, pw_im = apow(lags)
    pw_re, pw_im = jnp.moveaxis(pw_re, 0, 2), jnp.moveaxis(pw_im, 0, 2)
    cre, cim = c_re.astype(F32), c_im.astype(F32)
    cp_re = cre[:, :, None] * pw_re[:, :, :, None] - cim[:, :, None] * pw_im[:, :, :, None]
    cp_im = cre[:, :, None] * pw_im[:, :, :, None] + cim[:, :, None] * pw_re[:, :, :, None]
    bbt = jnp.concatenate([bb_re.transpose(0, 1, 3, 2), -bb_im.transpose(0, 1, 3, 2)], axis=-1)
    cps = jnp.concatenate([cp_re[:, :, :tc], cp_im[:, :, :tc]], axis=-1)
    cps = jnp.stack([cps[0], cps[1, :, ::-1]])
    cps = cps.transpose(0, 1, 4, 2, 3)
    pr, pi = pw_re[:, :, tc - 1 - jnp.arange(tc)], pw_im[:, :, tc - 1 - jnp.arange(tc)]
    win_re = pr[:, :, :, None, :] * bb_re.transpose(0, 1, 3, 2)[:, :, None] - pi[:, :, :, None, :] * bb_im.transpose(0, 1, 3, 2)[:, :, None]
    win_im = pr[:, :, :, None, :] * bb_im.transpose(0, 1, 3, 2)[:, :, None] + pi[:, :, :, None, :] * bb_re.transpose(0, 1, 3, 2)[:, :, None]
    win = jnp.concatenate([win_re, win_im], axis=-1)
    wo_re = cp_re[:, :, 1:].transpose(0, 1, 4, 2, 3)
    wo_im = -cp_im[:, :, 1:].transpose(0, 1, 4, 2, 3)
    wout = jnp.concatenate([wo_re, wo_im], axis=2)
    win = jnp.stack([win[0], win[1, :, ::-1]])
    wout = jnp.stack([wout[0], wout[1, :, :, ::-1]])
    g = lr.shape[1]
    j = br.shape[-1]
    p = lr.shape[-1]
    rows = []
    for k in range(n_levels):
        ar, ai = apow(float(tc * 2 ** k))
        rows.append(jnp.concatenate([ar, ar], axis=-1))
        rows.append(jnp.concatenate([-ai, ai], axis=-1))
    apw = jnp.stack(rows, axis=2)
    return ((bbt.transpose(1, 0, 2, 3), cps.reshape(2, g, 2 * p, tc * j).transpose(1, 0, 2, 3)),
            win.reshape(2, g, tc * j, 2 * p).transpose(1, 0, 2, 3).astype(BF16),
            wout.reshape(2, g, 2 * p, tc * j).transpose(1, 0, 2, 3).astype(BF16),
            apw.transpose(1, 0, 2, 3))


def _s5_kernel(u_ref, bbt_ref, cps_ref, win_ref, wout_ref, apw_ref, dsk_ref, y_ref, *, nb, ctx_rows, n_levels, p):
    u = u_ref[0]
    n, lanes = u.shape
    jw = bbt_ref.shape[2]
    tc = lanes // jw
    row = lax.broadcasted_iota(jnp.int32, (n, 1), 0)
    lane = lax.broadcasted_iota(jnp.int32, (jw, lanes), 1)
    y = u.astype(F32) * dsk_ref[0]
    for d in range(2):
        kt = jnp.dot(bbt_ref[0, d], cps_ref[0, d], precision=lax.Precision.HIGHEST, preferred_element_type=F32)
        blocks = []
        for s in range(tc):
            sh = jw * s if d == 0 else jw * (tc - 1 - s)
            if sh == 0:
                blocks.append(kt)
            elif d == 0:
                blocks.append(jnp.where(lane >= sh, pltpu.roll(kt, sh, 1), 0.0))
            else:
                blocks.append(jnp.where(lane < lanes - sh, pltpu.roll(kt, lanes - sh, 1), 0.0))
        tmat = jnp.concatenate(blocks, axis=0).astype(BF16)
        y = y + jnp.dot(u, tmat, preferred_element_type=F32)
        x = jnp.dot(u, win_ref[0, d], preferred_element_type=F32)
        if d == 0:
            def shift(a, s):
                return jnp.where(row >= s, pltpu.roll(a, s, 0), 0.0)
        else:
            if ctx_rows:
                x = pltpu.roll(x, n - ctx_rows, 0)

            def shift(a, s):
                return jnp.where(row < n - s, pltpu.roll(a, n - s, 0), 0.0)
        x = shift(x, nb)
        for k in range(n_levels):
            sh = shift(x, nb * 2 ** k)
            a1 = apw_ref[0, d, 2 * k:2 * k + 1, :]
            a2 = apw_ref[0, d, 2 * k + 1:2 * k + 2, :]
            x = x + a1 * sh + a2 * pltpu.roll(sh, p, 1)
        if d == 1 and ctx_rows:
            x = pltpu.roll(x, ctx_rows, 0)
        y = y + jnp.dot(x.astype(BF16), wout_ref[0, d], preferred_element_type=F32)
    y_ref[0] = y.astype(y_ref.dtype)


def s5_scan(uc, toep, win, wout, apw, dsk, *, nb, ctx_rows, n_levels):
    g, n, lanes = uc.shape
    p2 = win.shape[-1]
    bbt, cps = toep
    jw = bbt.shape[2]
    return pl.pallas_call(
        functools.partial(_s5_kernel, nb=nb, ctx_rows=ctx_rows, n_levels=n_levels, p=p2 // 2),
        out_shape=jax.ShapeDtypeStruct((g, n, lanes), BF16),
        grid=(g,),
        in_specs=[
            pl.BlockSpec((1, n, lanes), lambda i: (i, 0, 0)),
            pl.BlockSpec((1, 2, jw, p2), lambda i: (i, 0, 0, 0)),
            pl.BlockSpec((1, 2, p2, lanes), lambda i: (i, 0, 0, 0)),
            pl.BlockSpec((1, 2, lanes, p2), lambda i: (i, 0, 0, 0)),
            pl.BlockSpec((1, 2, p2, lanes), lambda i: (i, 0, 0, 0)),
            pl.BlockSpec((1, 2, 2 * n_levels, p2), lambda i: (i, 0, 0, 0)),
            pl.BlockSpec((1, 1, lanes), lambda i: (i, 0, 0)),
        ],
        out_specs=pl.BlockSpec((1, n, lanes), lambda i: (i, 0, 0)),
        compiler_params=_cparams(1),
        name="s5_scan",
    )(uc, bbt, cps, win, wout, apw, dsk)


def s5_mix(u_rows, ops, d_skip, *, bsz, seq, ctx):
    toep, win, wout, apw = ops
    g = win.shape[0]
    lanes = win.shape[2]
    tc = S5_CHUNK
    j = lanes // tc
    hgt = seq // GRID_W
    ul = u_rows[:bsz * seq].reshape(bsz, hgt, GRID_W, g, j).transpose(0, 2, 1, 3, 4).reshape(bsz, seq, g, j)
    uc = u_rows[bsz * seq:].reshape(bsz, ctx, g, j)
    useq = jnp.concatenate([uc, ul], axis=1)
    nc = (seq + ctx) // tc
    uch = useq.reshape(bsz, nc, tc, g, j).transpose(3, 1, 0, 2, 4).reshape(g, nc * bsz, lanes)
    n_levels = apw.shape[2] // 2
    dsk = jnp.tile(d_skip.astype(F32).reshape(g, 1, j), (1, tc, 1)).reshape(g, 1, lanes)
    ych = s5_scan(uch.astype(BF16), toep, win, wout, apw, dsk, nb=bsz, ctx_rows=(ctx // tc) * bsz, n_levels=n_levels)
    yseq = ych.reshape(g, nc, bsz, tc, j).transpose(2, 1, 3, 0, 4).reshape(bsz, seq + ctx, g * j)
    yc = yseq[:, :ctx].reshape(bsz * ctx, g * j)
    yl = yseq[:, ctx:].reshape(bsz, GRID_W, hgt, g * j).transpose(0, 2, 1, 3).reshape(bsz * seq, g * j)
    return jnp.concatenate([yl, yc], axis=0)


def _softplus(x):
    return jnp.maximum(x, 0.0) + jnp.log1p(jnp.exp(-jnp.abs(x)))


def _dn_kernel(ql_ref, kl_ref, vl_ref, qc_ref, kc_ref, vc_ref, bal_ref, bac_ref, zl_ref, zc_ref,
               cw_ref, lp_ref, nw_ref, ol_ref, oc_ref,
               nt_s, w2t_s, qp_s, el_s, o_s, tok_s, *, n_heads, ctx):
    c = DN_CHUNK
    seq = ql_ref.shape[0]
    t = seq + ctx
    dk = ql_ref.shape[1]
    nch = t // c
    ncc = ctx // c
    h = pl.program_id(1)
    row = lax.broadcasted_iota(jnp.int32, (t, 1), 0)
    rowc = row % c
    lane = lax.broadcasted_iota(jnp.int32, (1, dk), 1)

    first = (row == 0) | (row == ctx)
    last = (row == ctx - 1) | (row == t - 1)

    def conv_silu(xc_ref, xl_ref, kind):
        x = jnp.concatenate([xc_ref[...], xl_ref[...]], axis=0).astype(F32)
        xp = jnp.where(first, 0.0, pltpu.roll(x, 1, 0))
        xn = jnp.where(last, 0.0, pltpu.roll(x, t - 1, 0))
        w = cw_ref[0, kind]
        return _silu(xp * w[0:1] + x * w[1:2] + xn * w[2:3])

    def l2n(x):
        return x * lax.rsqrt(jnp.sum(x * x, axis=-1, keepdims=True) + EPS)

    q = l2n(conv_silu(qc_ref, ql_ref, 0)) * (dk ** -0.5)
    k = l2n(conv_silu(kc_ref, kl_ref, 1))
    v = conv_silu(vc_ref, vl_ref, 2)

    @pl.when(h == 0)
    def _():
        ba = jnp.concatenate([bac_ref[...], bal_ref[...]], axis=0)
        g_all = -lp_ref[0:1, :] * _softplus(ba + lp_ref[1:2, :])
        pf, sf = g_all, g_all
        s = 1
        while s < c:
            pf = pf + jnp.where(rowc >= s, pltpu.roll(pf, s, 0), 0.0)
            sf = sf + jnp.where(rowc < c - s, pltpu.roll(sf, t - s, 0), 0.0)
            s *= 2
        tok_s[0] = jax.nn.sigmoid(ba)
        tok_s[1] = pf
        tok_s[2] = sf

    beta_all, pf, sf = tok_s[0], tok_s[1], tok_s[2]

    def col(a, idx):
        return jnp.sum(jnp.where(lane == idx, a, 0.0), axis=1, keepdims=True)

    causal_f = (lax.broadcasted_iota(jnp.int32, (c, c), 0) >= lax.broadcasted_iota(jnp.int32, (c, c), 1))[None]
    strict_f = (lax.broadcasted_iota(jnp.int32, (c, c), 0) > lax.broadcasted_iota(jnp.int32, (c, c), 1))[None]
    causal_b = (lax.broadcasted_iota(jnp.int32, (c, c), 0) <= lax.broadcasted_iota(jnp.int32, (c, c), 1))[None]
    strict_b = (lax.broadcasted_iota(jnp.int32, (c, c), 0) < lax.broadcasted_iota(jnp.int32, (c, c), 1))[None]
    eye = (lax.broadcasted_iota(jnp.int32, (c, c), 0) == lax.broadcasted_iota(jnp.int32, (c, c), 1))[None].astype(F32)

    sub_blk = (lax.broadcasted_iota(jnp.int32, (c, c), 0) // DN_SUB
               == lax.broadcasted_iota(jnp.int32, (c, c), 1) // DN_SUB)[None]

    def neg_pow_inverse(x, m, limit):
        pinv = eye + x
        while m < limit:
            xb = x.astype(BF16)
            x = jnp.einsum('cij,cjk->cik', xb, xb, preferred_element_type=F32)
            pinv = pinv + jnp.einsum('cij,cjk->cik', pinv.astype(BF16), x.astype(BF16), preferred_element_type=F32)
            m *= 2
        return pinv

    q3 = q.reshape(nch, c, dk)
    k3 = k.reshape(nch, c, dk)
    v3 = v.reshape(nch, c, dk)
    k3b = k3.astype(BF16)
    q3b = q3.astype(BF16)
    for d in range(2):
        causal, strict = (causal_f, strict_f) if d == 0 else (causal_b, strict_b)
        beta = col(beta_all, d * n_heads + h)
        gc = col(pf if d == 0 else sf, (2 + d) * n_heads + h)
        hi = gc.astype(BF16).astype(F32)
        mid = (gc - hi).astype(BF16).astype(F32)
        lo = gc - hi - mid
        pieces = (hi, mid, lo)
        g1 = jnp.zeros((t, dk), F32)
        g2 = jnp.zeros((t, dk), F32)
        for n_p, piece in enumerate(pieces):
            pb = jnp.broadcast_to(piece, (t, dk))
            g1 = jnp.where(lane == n_p, pb, jnp.where(lane == 3 + n_p, 1.0, g1))
            g2 = jnp.where(lane == n_p, 1.0, jnp.where(lane == 3 + n_p, -pb, g2))
        ldiff = jnp.einsum('cid,cjd->cij', g1.astype(BF16).reshape(nch, c, dk), g2.astype(BF16).reshape(nch, c, dk),
                           preferred_element_type=F32)
        decay = jnp.where(causal, jnp.exp(jnp.where(causal, ldiff, 0.0)), 0.0)
        beta3 = beta.reshape(nch, c, 1)
        gc3 = gc.reshape(nch, c, 1)
        glast3 = gc3[:, c - 1:c, :] if d == 0 else gc3[:, 0:1, :]
        kb = k3 * beta3
        a = jnp.einsum('cid,cjd->cij', kb.astype(BF16), k3b, preferred_element_type=F32)
        a = jnp.where(strict, a * decay, 0.0)
        qk = jnp.einsum('cid,cjd->cij', q3b, k3b, preferred_element_type=F32)
        qk = jnp.where(causal, qk * decay, 0.0)
        a_diag = jnp.where(sub_blk, a, 0.0)
        dinv = neg_pow_inverse(-a_diag, 2, DN_SUB)
        n_off = jnp.einsum('cij,cjk->cik', dinv.astype(BF16), (a - a_diag).astype(BF16), preferred_element_type=F32)
        pinv = neg_pow_inverse(-n_off, 2 * DN_SUB, c)
        pinv = jnp.einsum('cij,cjk->cik', pinv.astype(BF16), dinv.astype(BF16), preferred_element_type=F32)
        rhs = jnp.concatenate([v3 * beta3, kb * jnp.exp(gc3)], axis=-1)
        sol = jnp.einsum('cij,cjd->cid', pinv.astype(BF16), rhs.astype(BF16), preferred_element_type=F32)
        solb = sol.astype(BF16)
        qs = jnp.einsum('cij,cjd->cid', qk.astype(BF16), solb, preferred_element_type=F32)
        o_s[d] = qs[:, :, :dk].reshape(t, dk)
        qp_s[d] = (q3 * jnp.exp(gc3) - qs[:, :, dk:]).reshape(t, dk).astype(BF16)
        ke = (k3 * jnp.exp(glast3 - gc3)).astype(BF16)
        solt = jnp.swapaxes(sol, 1, 2).astype(BF16)
        nw2 = jnp.einsum('cdi,cik->cdk', solt, ke, preferred_element_type=F32)
        nt_s[d] = nw2[:, :dk, :]
        w2t_s[d] = nw2[:, dk:, :].astype(BF16)
        el_s[d] = jnp.broadcast_to(jnp.exp(glast3), (nch, 8, dk))

    def chunk_step(d, ci, st):
        r0 = pl.multiple_of(ci * c, c)
        stb = st.astype(BF16)
        o_s[d, pl.ds(r0, c), :] += lax.dot_general(qp_s[d, pl.ds(r0, c), :], stb, (((1,), (1,)), ((), ())),
                                                   preferred_element_type=F32)
        return (st * el_s[d, ci][0:1, :] + nt_s[d, ci]
                - jnp.dot(stb, w2t_s[d, ci], preferred_element_type=F32))

    def ctx_body(n, carry):
        return chunk_step(0, n, carry[0]), chunk_step(1, ncc - 1 - n, carry[1])

    def lat_body(n, carry):
        return chunk_step(0, ncc + n, carry[0]), chunk_step(1, nch - 1 - n, carry[1])

    zero = jnp.zeros((dk, dk), F32)
    carry = lax.fori_loop(0, ncc, ctx_body, (zero, zero))
    lax.fori_loop(0, nch - ncc, lat_body, carry)

    o = o_s[0] + o_s[1]
    on = o * lax.rsqrt(jnp.mean(o * o, axis=-1, keepdims=True) + EPS) * nw_ref[...]
    z = jnp.concatenate([zc_ref[...], zl_ref[...]], axis=0).astype(F32)
    out = (on * _silu(z)).astype(ol_ref.dtype)
    oc_ref[...] = out[:ctx]
    ol_ref[...] = out[ctx:]


def deltanet_mix(p_main, p_ba, conv_w, a_log, dt_bias, norm_w, *, bsz, seq, ctx, u_width):
    n_heads = a_log.shape[-1]
    dk = norm_w.shape[-1]
    c = DN_CHUNK
    t = seq + ctx
    nch = t // c
    assert seq % c == 0 and ctx % c == 0 and u_width % dk == 0 and 4 * n_heads <= p_ba.shape[1]
    cb = u_width // dk
    lat_rows = bsz * seq
    cw = conv_w.astype(F32).reshape(conv_w.shape[0], 3, n_heads, dk).transpose(2, 1, 0, 3)
    lanes = p_ba.shape[1]
    lp = jnp.zeros((2, lanes), F32)
    lp = lp.at[0, 2 * n_heads:4 * n_heads].set(jnp.exp(a_log.astype(F32)).reshape(-1))
    lp = lp.at[1, 2 * n_heads:4 * n_heads].set(dt_bias.astype(F32).reshape(-1))
    cblk = lat_rows // ctx

    def lat_spec(off):
        return pl.BlockSpec((seq, dk), lambda b, h: (b, off + h))

    def ctx_spec(off):
        return pl.BlockSpec((ctx, dk), lambda b, h: (cblk + b, off + h))

    yl, yc = pl.pallas_call(
        functools.partial(_dn_kernel, n_heads=n_heads, ctx=ctx),
        out_shape=(jax.ShapeDtypeStruct((lat_rows, n_heads * dk), BF16),
                   jax.ShapeDtypeStruct((bsz * ctx, n_heads * dk), BF16)),
        grid=(bsz, n_heads),
        in_specs=[
            lat_spec(cb), lat_spec(cb + n_heads), lat_spec(cb + 2 * n_heads),
            ctx_spec(cb), ctx_spec(cb + n_heads), ctx_spec(cb + 2 * n_heads),
            pl.BlockSpec((seq, lanes), lambda b, h: (b, 0)),
            pl.BlockSpec((ctx, lanes), lambda b, h: (cblk + b, 0)),
            lat_spec(cb + 3 * n_heads), ctx_spec(cb + 3 * n_heads),
            pl.BlockSpec((1, 3, conv_w.shape[0], dk), lambda b, h: (h, 0, 0, 0)),
            pl.BlockSpec((2, lanes), lambda b, h: (0, 0)),
            pl.BlockSpec((1, dk), lambda b, h: (0, 0)),
        ],
        out_specs=(pl.BlockSpec((seq, dk), lambda b, h: (b, h)),
                   pl.BlockSpec((ctx, dk), lambda b, h: (b, h))),
        scratch_shapes=[
            pltpu.VMEM((2, nch, dk, dk), F32),
            pltpu.VMEM((2, nch, dk, dk), BF16),
            pltpu.VMEM((2, t, dk), BF16),
            pltpu.VMEM((2, nch, 8, dk), F32),
            pltpu.VMEM((2, t, dk), F32),
            pltpu.VMEM((3, t, lanes), F32),
        ],
        compiler_params=_cparams(2),
        name="deltanet",
    )(p_main, p_main, p_main, p_main, p_main, p_main, p_ba, p_ba, p_main, p_main,
      cw, lp, norm_w.astype(F32).reshape(1, dk))
    return jnp.concatenate([yl, yc], axis=0)


def _store_token_rows(ref2d, x):
    n, d = x.shape
    assert d == ROW_SUB * 2 * LANES, (d, ROW_SUB)
    for m in range(ROW_SUB):
        lo = x[:, 2 * m * LANES:(2 * m + 1) * LANES].astype(BF16).astype(F32)
        hi = x[:, (2 * m + 1) * LANES:(2 * m + 2) * LANES].astype(BF16).astype(F32)
        word = ((lax.bitcast_convert_type(hi, U32) & jnp.uint32(0xFFFF0000))
                | (lax.bitcast_convert_type(lo, U32) >> 16))
        ref2d[pl.ds(m, n, stride=ROW_SUB), :] = word


def _load_token_rows(ref2d, first_token, n_tokens):
    cols = []
    for m in range(ROW_SUB):
        word = ref2d[pl.ds(first_token * ROW_SUB + m, n_tokens, stride=ROW_SUB), :]
        cols.append(lax.bitcast_convert_type(word << 16, F32))
        cols.append(lax.bitcast_convert_type(word & jnp.uint32(0xFFFF0000), F32))
    return jnp.concatenate(cols, axis=1)


def _row_gather(table_hbm, idx_ref, buf, sem, n_rows):
    def copy(r, src_row):
        return pltpu.make_async_copy(table_hbm.at[pl.ds(pl.multiple_of(src_row * ROW_SUB, ROW_SUB), ROW_SUB)],
                                     buf.at[pl.ds(pl.multiple_of(r * ROW_SUB, ROW_SUB), ROW_SUB)], sem)

    def start():
        def body(r, carry):
            copy(r, idx_ref[0, 0, r]).start()
            return carry
        lax.fori_loop(0, n_rows, body, 0, unroll=8)

    def wait():
        def body(r, carry):
            copy(r, 0).wait()
            return carry
        lax.fori_loop(0, n_rows, body, 0, unroll=8)

    return start, wait


def _dispatch_kernel(cnt_ref, ps_ref, pe_ref, dst_ref, h_hbm, xg_hbm, zrow, sem, *, top_k, tm):
    i = pl.program_id(0)
    last = pl.num_programs(0) - 1
    n_exp = cnt_ref.shape[0]

    def row_copy(tok, dst_row):
        return pltpu.make_async_copy(h_hbm.at[pl.ds(pl.multiple_of(tok * ROW_SUB, ROW_SUB), ROW_SUB)],
                                     xg_hbm.at[pl.ds(pl.multiple_of(dst_row * ROW_SUB, ROW_SUB), ROW_SUB)], sem.at[0])

    for k in range(top_k):
        def start(t, carry, k=k):
            row_copy(i * tm + t, dst_ref[0, 0, k * tm + t]).start()
            return carry
        lax.fori_loop(0, tm, start, 0, unroll=8)

    def wait_tile():
        def wait(r, carry):
            row_copy(0, 0).wait()
            return carry
        lax.fori_loop(0, top_k * tm, wait, 0, unroll=8)

    @pl.when(i > 0)
    def _():
        wait_tile()

    @pl.when(i == last)
    def _():
        wait_tile()
        zrow[...] = jnp.zeros_like(zrow)

        def pad_copy(s):
            return pltpu.make_async_copy(zrow, xg_hbm.at[pl.ds(pl.multiple_of(s * ROW_SUB, ROW_SUB), ROW_SUB)],
                                         sem.at[1])

        def zero_fill(lo, hi):
            lax.fori_loop(lo, hi, lambda s, c: (pad_copy(s).start(), c)[1], 0)
            lax.fori_loop(lo, hi, lambda s, c: (pad_copy(s).wait(), c)[1], 0)

        def per_expert(e, carry):
            zero_fill(ps_ref[e] + cnt_ref[e], pe_ref[e])
            return carry

        lax.fori_loop(0, n_exp, per_expert, 0)
        zero_fill(pe_ref[n_exp - 1], xg_hbm.shape[0] // ROW_SUB)


def dispatch_rows(h_packed, dest, counts, pad_start, pad_end, *, n_slots, rows, tm=128):
    top_k = dest.shape[0]
    tm = _row_tile(rows, tm)
    n_tiles = rows // tm
    dst3 = dest.reshape(top_k, n_tiles, tm).transpose(1, 0, 2).reshape(n_tiles, 1, top_k * tm)
    return pl.pallas_call(
        functools.partial(_dispatch_kernel, top_k=top_k, tm=tm),
        out_shape=jax.ShapeDtypeStruct((n_slots * ROW_SUB, LANES), U32),
        grid_spec=pltpu.PrefetchScalarGridSpec(
            num_scalar_prefetch=3,
            grid=(n_tiles,),
            in_specs=[
                pl.BlockSpec((1, 1, top_k * tm), lambda i, c, s, e: (i, 0, 0), memory_space=pltpu.SMEM),
                pl.BlockSpec(memory_space=pl.ANY),
            ],
            out_specs=pl.BlockSpec(memory_space=pl.ANY),
            scratch_shapes=[pltpu.VMEM((ROW_SUB, LANES), U32), pltpu.SemaphoreType.DMA((2,))],
        ),
        compiler_params=_cparams(1),
        name="dispatch_rows",
    )(counts, pad_start, pad_end, dst3, h_packed)


def _experts_kernel(be_ref, nxt_ref, nv_ref, x_ref, wg_hbm, wu_hbm, wd_hbm, o_ref,
                    wgs, wus, wds, wgb, wub, wdb, wsem):
    i = pl.program_id(0)
    nv = nv_ref[0]
    mb = x_ref.shape[0] // ROW_SUB

    def weight_copies(e):
        return (pltpu.make_async_copy(wg_hbm.at[e], wgs, wsem.at[0]),
                pltpu.make_async_copy(wu_hbm.at[e], wus, wsem.at[1]),
                pltpu.make_async_copy(wd_hbm.at[e], wds, wsem.at[2]))

    @pl.when((i == 0) & (nv > 0))
    def _():
        for cp in weight_copies(be_ref[0]):
            cp.start()

    prev = be_ref[jnp.maximum(i - 1, 0)]

    @pl.when((i < nv) & ((i == 0) | (be_ref[i] != prev)))
    def _():
        for cp in weight_copies(be_ref[i]):
            cp.wait()
        wgb[...] = wgs[...].astype(BF16)
        wub[...] = wus[...].astype(BF16)
        wdb[...] = wds[...].astype(BF16)

        @pl.when(nxt_ref[i] >= 0)
        def _():
            for cp in weight_copies(nxt_ref[i]):
                cp.start()

    @pl.when(i < nv)
    def _():
        x = _load_token_rows(x_ref, 0, mb).astype(BF16)
        g = jnp.dot(x, wgb[...], preferred_element_type=F32)
        u = jnp.dot(x, wub[...], preferred_element_type=F32)
        a = (_silu(g) * u).astype(BF16)
        _store_token_rows(o_ref, jnp.dot(a, wdb[...], preferred_element_type=F32))

    @pl.when(i >= nv)
    def _():
        o_ref[...] = jnp.zeros_like(o_ref)


def routed_experts(xg, block_e, next_e, n_valid, w_gate, w_up, w_down):
    d = w_gate.shape[-2]
    f = w_gate.shape[-1]
    mb = MOE_BLOCK
    n_blocks = xg.shape[0] // (mb * ROW_SUB)
    lw = LANES
    w_gate, w_up = w_gate.reshape(-1, d, f), w_up.reshape(-1, d, f)
    w_down = w_down.reshape(-1, f, d)
    return pl.pallas_call(
        _experts_kernel,
        out_shape=jax.ShapeDtypeStruct((n_blocks * mb * ROW_SUB, lw), U32),
        grid_spec=pltpu.PrefetchScalarGridSpec(
            num_scalar_prefetch=3,
            grid=(n_blocks,),
            in_specs=[
                pl.BlockSpec((mb * ROW_SUB, lw), lambda i, be, nx, nv: (jnp.minimum(i, jnp.maximum(nv[0] - 1, 0)), 0)),
                pl.BlockSpec(memory_space=pl.ANY),
                pl.BlockSpec(memory_space=pl.ANY),
                pl.BlockSpec(memory_space=pl.ANY),
            ],
            out_specs=pl.BlockSpec((mb * ROW_SUB, lw), lambda i, be, nx, nv: (i, 0)),
            scratch_shapes=[pltpu.VMEM((d, f), F32), pltpu.VMEM((d, f), F32), pltpu.VMEM((f, d), F32),
                            pltpu.VMEM((d, f), BF16), pltpu.VMEM((d, f), BF16), pltpu.VMEM((f, d), BF16),
                            pltpu.SemaphoreType.DMA((3,))],
        ),
        compiler_params=_cparams(1),
        name="routed_experts",
    )(block_e, next_e, n_valid, xg, w_gate, w_up, w_down)


def _swiglu_kernel(x_ref, wg_ref, wu_ref, wd_ref, o_ref, wgb, wub, wdb):
    @pl.when(pl.program_id(0) == 0)
    def _():
        wgb[...] = wg_ref[...].astype(BF16)
        wub[...] = wu_ref[...].astype(BF16)
        wdb[...] = wd_ref[...].astype(BF16)

    x = x_ref[...].astype(BF16)
    g = jnp.dot(x, wgb[...], preferred_element_type=F32)
    u = jnp.dot(x, wub[...], preferred_element_type=F32)
    a = (_silu(g) * u).astype(BF16)
    o_ref[...] = jnp.dot(a, wdb[...], preferred_element_type=F32).astype(o_ref.dtype)


def shared_expert(x, wg, wu, wd, *, layer, rows, tm=None):
    d = x.shape[1]
    f = wg.shape[-1]
    tm = _row_tile(rows, tm)
    const = lambda i: (0, 0)
    return pl.pallas_call(
        _swiglu_kernel,
        out_shape=jax.ShapeDtypeStruct((rows, d), F32),
        grid=(rows // tm,),
        in_specs=[
            pl.BlockSpec((tm, d), lambda i: (i, 0)),
            _wspec(wg, layer, (d, f), const, pipeline_mode=pl.Buffered(1)),
            _wspec(wu, layer, (d, f), const, pipeline_mode=pl.Buffered(1)),
            _wspec(wd, layer, (f, d), const, pipeline_mode=pl.Buffered(1)),
        ],
        out_specs=pl.BlockSpec((tm, d), lambda i: (i, 0)),
        scratch_shapes=[pltpu.VMEM((d, f), BF16), pltpu.VMEM((d, f), BF16), pltpu.VMEM((f, d), BF16)],
        compiler_params=_cparams(1),
        name="shared_expert",
    )(x, wg, wu, wd)


def _ffn_res_kernel(dst_ref, dstn_ref, y_hbm, w_ref, s_ref, x_ref, nw_ref, g_ref, o_ref, buf, sem, *, top_k):
    i = pl.program_id(0)
    tm = x_ref.shape[0]
    nr = top_k * tm
    slot = i % 2
    start_first, _ = _row_gather(y_hbm, dst_ref, buf.at[0], sem.at[0], nr)
    start_next, _ = _row_gather(y_hbm, dstn_ref, buf.at[1 - slot], sem.at[1 - slot], nr)
    _, wait_cur = _row_gather(y_hbm, dst_ref, buf.at[slot], sem.at[slot], nr)

    @pl.when(i == 0)
    def _():
        start_first()

    @pl.when(i + 1 < pl.num_programs(0))
    def _():
        start_next()

    wait_cur()
    w = w_ref[...]
    f = s_ref[...]
    for k in range(top_k):
        f = f + _load_token_rows(buf.at[slot], k * tm, tm) * w[:, k:k + 1]
    fn = f * lax.rsqrt(jnp.mean(f * f, axis=-1, keepdims=True) + EPS) * nw_ref[...]
    o_ref[...] = x_ref[...] + g_ref[0] * fn


def ffn_residual(y, dest, wts, shared, x, nw, gate, set_of_tile, *, rows, tm=128):
    d = x.shape[1]
    lw = LANES
    top_k = wts.shape[1]
    tm = _row_tile(rows, tm)
    n_tiles = rows // tm
    dst3 = dest.reshape(top_k, n_tiles, tm).transpose(1, 0, 2).reshape(n_tiles, 1, top_k * tm)
    return pl.pallas_call(
        functools.partial(_ffn_res_kernel, top_k=top_k),
        out_shape=jax.ShapeDtypeStruct((rows, d), F32),
        grid=(n_tiles,),
        in_specs=[
            pl.BlockSpec((1, 1, top_k * tm), lambda i: (i, 0, 0), memory_space=pltpu.SMEM),
            pl.BlockSpec((1, 1, top_k * tm), lambda i: (jnp.minimum(i + 1, n_tiles - 1), 0, 0),
                         memory_space=pltpu.SMEM),
            pl.BlockSpec(memory_space=pl.ANY),
            pl.BlockSpec((tm, top_k), lambda i: (i, 0)),
            pl.BlockSpec((tm, d), lambda i: (i, 0)),
            pl.BlockSpec((tm, d), lambda i: (i, 0)),
            pl.BlockSpec((1, d), lambda i: (0, 0)),
            pl.BlockSpec((1, 1, d), lambda i: (set_of_tile(i, tm), 0, 0)),
        ],
        out_specs=pl.BlockSpec((tm, d), lambda i: (i, 0)),
        scratch_shapes=[pltpu.VMEM((2, top_k * tm * ROW_SUB, lw), U32), pltpu.SemaphoreType.DMA((2,))],
        compiler_params=_cparams(1),
        name="ffn_residual",
    )(dst3, dst3, y, wts, shared, x, nw.reshape(1, d), gate)


def _route_kernel(h_ref, wr_ref, b_ref, eidx_ref, wts_ref, rank_ref, cnt_ref, wrb, tri, cnt_s, *,
                  n_groups, topk_groups, top_k):
    n_exp, tm = wr_ref.shape[0], h_ref.shape[0]
    gs = n_exp // n_groups
    ninf = -jnp.inf

    @pl.when(pl.program_id(0) == 0)
    def _():
        wrb[...] = wr_ref[...].astype(BF16)
        tri[...] = (lax.broadcasted_iota(jnp.int32, (tm, tm), 0)
                    < lax.broadcasted_iota(jnp.int32, (tm, tm), 1)).astype(BF16)
        cnt_s[...] = jnp.zeros_like(cnt_s)

    logits = lax.dot_general(wrb[...], h_ref[...].astype(BF16), (((1,), (1,)), ((), ())),
                             preferred_element_type=F32)
    scores = jax.nn.sigmoid(logits)
    sel = scores + b_ref[:, 0:1]
    s3 = sel.reshape(n_groups, gs, tm)
    io3 = lax.broadcasted_iota(jnp.int32, (n_groups, gs, tm), 1)
    m1 = jnp.max(s3, axis=1, keepdims=True)
    i1 = jnp.min(jnp.where(s3 == m1, io3, gs), axis=1, keepdims=True)
    m2 = jnp.max(jnp.where(io3 == i1, ninf, s3), axis=1, keepdims=True)
    gscore = (m1 + m2).reshape(n_groups, tm)
    iog = lax.broadcasted_iota(jnp.int32, (n_groups, tm), 0)
    gsel = jnp.zeros((n_groups, tm), jnp.bool_)
    for _ in range(topk_groups):
        gm = jnp.max(gscore, axis=0, keepdims=True)
        gi = jnp.min(jnp.where(gscore == gm, iog, n_groups), axis=0, keepdims=True)
        hit = iog == gi
        gsel = gsel | hit
        gscore = jnp.where(hit, ninf, gscore)
    x = jnp.where(gsel.reshape(n_groups, 1, tm), s3, ninf).reshape(n_exp, tm)
    ioe = lax.broadcasted_iota(jnp.int32, (n_exp, tm), 0)
    hits = []
    chosen = jnp.zeros((n_exp, tm), jnp.bool_)
    for k in range(top_k):
        m = jnp.max(x, axis=0, keepdims=True)
        idx = jnp.min(jnp.where(x == m, ioe, n_exp), axis=0, keepdims=True)
        hit = ioe == idx
        x = jnp.where(hit, ninf, x)
        chosen = chosen | hit
        hits.append(hit)
        eidx_ref[k:k + 1, :] = idx
    wsel = jnp.where(chosen, scores, 0.0)
    wd = wsel / jnp.sum(wsel, axis=0, keepdims=True) * ROUTE_SCALE
    cm = jnp.where(chosen, 1.0, 0.0)
    rank = jnp.dot(cm.astype(BF16), tri[...], preferred_element_type=F32) + cnt_s[:, 0:1]
    for k in range(top_k):
        wts_ref[k:k + 1, :] = jnp.sum(jnp.where(hits[k], wd, 0.0), axis=0, keepdims=True)
        rank_ref[k:k + 1, :] = jnp.sum(jnp.where(hits[k], rank, 0.0), axis=0, keepdims=True).astype(jnp.int32)
    cnt_s[...] = cnt_s[...] + jnp.sum(cm, axis=1, keepdims=True)
    cnt_ref[...] = cnt_s[...].astype(jnp.int32)


def route(h, w_router, e_bias, *, rows, tm=None):
    d = h.shape[1]
    n_exp = w_router.shape[1]
    tm = _row_tile(rows, tm)
    bias = jnp.broadcast_to(e_bias.astype(F32).reshape(n_exp, 1), (n_exp, LANES))
    kt = lambda i: (0, i)
    eidx, wts, rank, cnt = pl.pallas_call(
        functools.partial(_route_kernel, n_groups=N_GROUPS, topk_groups=TOPK_GROUPS, top_k=TOP_K),
        out_shape=(jax.ShapeDtypeStruct((TOP_K, rows), jnp.int32), jax.ShapeDtypeStruct((TOP_K, rows), F32),
                   jax.ShapeDtypeStruct((TOP_K, rows), jnp.int32), jax.ShapeDtypeStruct((n_exp, LANES), jnp.int32)),
        grid=(rows // tm,),
        in_specs=[
            pl.BlockSpec((tm, d), lambda i: (i, 0)),
            pl.BlockSpec((n_exp, d), lambda i: (0, 0)),
            pl.BlockSpec((n_exp, LANES), lambda i: (0, 0)),
        ],
        out_specs=(pl.BlockSpec((TOP_K, tm), kt), pl.BlockSpec((TOP_K, tm), kt), pl.BlockSpec((TOP_K, tm), kt),
                   pl.BlockSpec((n_exp, LANES), lambda i: (0, 0))),
        scratch_shapes=[pltpu.VMEM((n_exp, d), BF16), pltpu.VMEM((tm, tm), BF16), pltpu.VMEM((n_exp, LANES), F32)],
        compiler_params=_cparams(1),
        name="route",
    )(h, w_router.T, bias)
    return eidx, wts, rank, cnt[:, 0]


def moe_ffn(h, h_packed, w_router, e_bias, w_gate, w_up, w_down, ws_gate, ws_up, ws_down, *, layer, rows):
    n_exp = w_router.shape[1]
    eidx, wts, rank, counts = route(h, w_router, e_bias, rows=rows)
    mb = MOE_BLOCK
    padded = (counts + mb - 1) // mb * mb
    pad_end = jnp.cumsum(padded)
    pad_start = pad_end - padded
    start_of = jnp.sum(jnp.where(eidx[:, :, None] == jnp.arange(n_exp)[None, None, :], pad_start, 0), axis=-1)
    dest = start_of + rank
    n_blocks = (rows * TOP_K + n_exp * (mb - 1) + mb - 1) // mb
    xg = dispatch_rows(h_packed, dest, counts, pad_start, pad_end, n_slots=n_blocks * mb, rows=rows)
    n_valid = (pad_end[-1] // mb).astype(jnp.int32)
    starts = jnp.arange(n_blocks, dtype=jnp.int32) * mb
    block_e = jnp.minimum(jnp.sum(starts[:, None] >= pad_end[None, :], axis=1), n_exp - 1).astype(jnp.int32)
    last_e = block_e[jnp.maximum(n_valid - 1, 0)]
    block_e = jnp.where(jnp.arange(n_blocks) < n_valid, block_e, last_e)
    seg_end = pad_end[block_e] // mb
    next_e = jnp.where(seg_end < n_valid, block_e[jnp.minimum(seg_end, n_blocks - 1)] + layer * n_exp, -1)
    y = routed_experts(xg, block_e + layer * n_exp, next_e.astype(jnp.int32), n_valid.reshape(1),
                       w_gate, w_up, w_down)
    shared = shared_expert(h, ws_gate, ws_up, ws_down, layer=layer, rows=rows)
    return y, dest, wts.T, shared


def kernel(x, c, ctx, c_ctx, w_mod, b_mod, norm_mix_pre, norm_mix_post, norm_ffn_pre, norm_ffn_post,
           w_in, s5_lam_re, s5_lam_im, s5_log_step, s5_b_re, s5_b_im, s5_c_re, s5_c_im, s5_d, s5_w_glu,
           dn_conv, dn_a_log, dn_dt_bias, dn_norm, w_br_s5, w_br_dn, w_out,
           moe_router, moe_bias, moe_w_gate, moe_w_up, moe_w_down, sh_w_gate, sh_w_up, sh_w_down):
    bsz, seq, d = x.shape
    n_ctx = ctx.shape[1]
    depth = w_mod.shape[0]
    lat_rows, ctx_rows = bsz * seq, bsz * n_ctx
    all_rows = lat_rows + ctx_rows
    s5_width = s5_d.shape[1]
    dn_width = w_br_dn.shape[1]
    n_heads = dn_a_log.shape[-1]
    main_cols = s5_width + 4 * dn_width
    ba_cols = 4 * n_heads
    lanes = 128
    nc = (seq + n_ctx) // S5_CHUNK
    n_levels = max(1, (nc - 1).bit_length())

    def set_of_tile(i, tm):
        return jnp.minimum((i * tm) // seq, bsz)

    xs = jnp.concatenate([x.reshape(lat_rows, d), ctx.reshape(ctx_rows, d)], axis=0)
    n_sets = 8
    cin = jnp.zeros((n_sets, d), F32).at[:bsz].set(_silu(c)).at[bsz].set(_silu(c_ctx))
    for i in range(depth):
        last = i == depth - 1
        rows = lat_rows if last else all_rows
        mods = matmul(cin, w_mod, b_mod[i].reshape(1, -1), layer=i, tm=n_sets, name="mods").reshape(n_sets, 6, 1, d)
        mod = [mods[:, k] for k in range(6)]
        hmix = prenorm(xs, norm_mix_pre[i], mod[0], mod[1], set_of_tile)
        p_main = matmul(hmix, w_in, layer=i, n_cols=main_cols, out_dtype=BF16, name="in_proj")
        p_ba = matmul(hmix, w_in, layer=i, col0=main_cols, n_cols=lanes, tn=lanes, name="in_proj_ba")
        gates = matmul_unaligned(hmix, w_in, layer=i, col0=main_cols + ba_cols, n_cols=2 * d, rows=rows,
                                 name="in_proj_gates")
        ops = s5_operators(s5_lam_re[i], s5_lam_im[i], s5_log_step[i], s5_b_re[i], s5_b_im[i],
                           s5_c_re[i], s5_c_im[i], n_levels)
        y_s5 = s5_mix(p_main[:, :s5_width], ops, s5_d[i], bsz=bsz, seq=seq, ctx=n_ctx)
        y_s5 = s5_glu(y_s5, s5_w_glu, layer=i, rows=rows)
        y_dn = deltanet_mix(p_main, p_ba, dn_conv[i], dn_a_log[i], dn_dt_bias[i], dn_norm[i],
                            bsz=bsz, seq=seq, ctx=n_ctx, u_width=s5_width)
        m = branch_merge(y_s5, y_dn, gates, w_br_s5, w_br_dn, layer=i, rows=rows)
        xs = outproj_residual(m, w_out, xs, norm_mix_post[i], mod[2], set_of_tile, layer=i, rows=rows)
        hffn, hpacked = prenorm(xs, norm_ffn_pre[i], mod[3], mod[4], set_of_tile, rows=rows, packed=True)
        y, dest, wts, shared = moe_ffn(hffn, hpacked, moe_router[i], moe_bias[i], moe_w_gate, moe_w_up, moe_w_down,
                                       sh_w_gate, sh_w_up, sh_w_down, layer=i, rows=rows)
        xs = ffn_residual(y, dest, wts, shared, xs, norm_ffn_post[i], mod[5], set_of_tile, rows=rows)
    return xs[:lat_rows].reshape(bsz, seq, d)
```

```python
import functools

import jax
import jax.numpy as jnp
from jax import lax
from jax.experimental import pallas as pl
from jax.experimental.pallas import tpu as pltpu

F32 = jnp.float32
BF16 = jnp.bfloat16

EPS = 1e-6
GRID_W = 64
S5_CHUNK = 16
DN_CHUNK = 64
DN_SUB = 16
TOP_K = 8
N_GROUPS = 8
TOPK_GROUPS = 4
ROUTE_SCALE = 2.5
MOE_BLOCK = 256

V7X_VMEM_LIMIT = 56 * 1024 * 1024
LANES = 128
ROW_TILE = 1024
ROW_SUB = 8
U32 = jnp.uint32


def _cparams(n_axes, vmem=V7X_VMEM_LIMIT):
    return pltpu.CompilerParams(dimension_semantics=("arbitrary",) * n_axes, vmem_limit_bytes=vmem)


def _silu(x):
    return x * jax.nn.sigmoid(x)


def _row_tile(rows, tm=None):
    tm = min(tm or ROW_TILE, ROW_TILE, rows)
    assert rows % tm == 0, (rows, tm)
    return tm


def _col_tile(n, tn, col0=0):
    tn = min(tn, n)
    while n % tn or col0 % tn:
        tn -= LANES
    return tn


def _wspec(w, layer, block, index_map, **kw):
    if w.ndim == len(block):
        return pl.BlockSpec(block, index_map, **kw)
    return pl.BlockSpec((None,) + tuple(block), lambda *a: (layer,) + tuple(index_map(*a)), **kw)


def _mm_kernel(x_ref, w_ref, b_ref, o_ref, wbf_ref):
    @pl.when(pl.program_id(1) == 0)
    def _():
        wbf_ref[...] = w_ref[...].astype(BF16)

    acc = jnp.dot(x_ref[...].astype(BF16), wbf_ref[...], preferred_element_type=F32)
    o_ref[...] = (acc + b_ref[...]).astype(o_ref.dtype)


def matmul(x, w, bias=None, *, layer=0, n_cols=None, col0=0, out_dtype=F32, tm=None, tn=1024, rows=None,
           name="matmul"):
    m, k = x.shape
    rows = m if rows is None else rows
    n_cols = w.shape[-1] - col0 if n_cols is None else n_cols
    tm = _row_tile(rows, tm)
    tn = _col_tile(n_cols, tn, col0)
    assert rows % tm == 0 and n_cols % tn == 0 and col0 % tn == 0, (rows, tm, n_cols, tn, col0)
    if bias is None:
        bias = jnp.zeros((1, n_cols), F32)
    cb0 = col0 // tn
    return pl.pallas_call(
        _mm_kernel,
        out_shape=jax.ShapeDtypeStruct((rows, n_cols), out_dtype),
        grid=(n_cols // tn, rows // tm),
        in_specs=[
            pl.BlockSpec((tm, k), lambda j, i: (i, 0)),
            _wspec(w, layer, (k, tn), lambda j, i: (0, j + cb0)),
            pl.BlockSpec((1, tn), lambda j, i: (0, j)),
        ],
        out_specs=pl.BlockSpec((tm, tn), lambda j, i: (i, j)),
        scratch_shapes=[pltpu.VMEM((k, tn), BF16)],
        compiler_params=_cparams(2),
        name=name,
    )(x, w, bias)


def _mm_shifted_kernel(x_ref, wa_ref, wb_ref, o_ref, wbf_ref, *, shift):
    @pl.when(pl.program_id(1) == 0)
    def _():
        tn = wbf_ref.shape[1]
        w = jnp.concatenate([wa_ref[...], wb_ref[...]], axis=1)
        wbf_ref[...] = w[:, shift:shift + tn].astype(BF16)

    o_ref[...] = jnp.dot(x_ref[...], wbf_ref[...], preferred_element_type=F32).astype(o_ref.dtype)


def matmul_unaligned(x, w, *, layer, col0, n_cols, out_dtype=BF16, rows=None, tm=None, tn=512, name="matmul_unaligned"):
    m, k = x.shape
    rows = m if rows is None else rows
    tm = _row_tile(rows, tm)
    tn = _col_tile(n_cols, tn)
    base, shift = col0 // tn, col0 % tn
    return pl.pallas_call(
        functools.partial(_mm_shifted_kernel, shift=shift),
        out_shape=jax.ShapeDtypeStruct((rows, n_cols), out_dtype),
        grid=(n_cols // tn, rows // tm),
        in_specs=[
            pl.BlockSpec((tm, k), lambda j, i: (i, 0)),
            _wspec(w, layer, (k, tn), lambda j, i: (0, base + j)),
            _wspec(w, layer, (k, tn), lambda j, i: (0, base + j + 1)),
        ],
        out_specs=pl.BlockSpec((tm, tn), lambda j, i: (i, j)),
        scratch_shapes=[pltpu.VMEM((k, tn), BF16)],
        compiler_params=_cparams(2),
        name=name,
    )(x, w, w)


def _prenorm_kernel(x_ref, w_ref, sh_ref, sc_ref, o_ref, *packed_ref):
    x = x_ref[...]
    y = x * lax.rsqrt(jnp.mean(x * x, axis=-1, keepdims=True) + EPS) * w_ref[...]
    h = y * (1.0 + sc_ref[0]) + sh_ref[0]
    o_ref[...] = h.astype(o_ref.dtype)
    if packed_ref:
        _store_token_rows(packed_ref[0], h)


def prenorm(x, w, shift, scale, set_of_tile, *, rows=None, tm=None, packed=False):
    m, d = x.shape
    rows = m if rows is None else rows
    tm = _row_tile(rows, tm)
    out_shape = [jax.ShapeDtypeStruct((rows, d), BF16)]
    out_specs = [pl.BlockSpec((tm, d), lambda i: (i, 0))]
    if packed:
        out_shape.append(jax.ShapeDtypeStruct((rows * ROW_SUB, LANES), U32))
        out_specs.append(pl.BlockSpec((tm * ROW_SUB, LANES), lambda i: (i, 0)))
    out = pl.pallas_call(
        _prenorm_kernel,
        out_shape=out_shape,
        grid=(rows // tm,),
        in_specs=[
            pl.BlockSpec((tm, d), lambda i: (i, 0)),
            pl.BlockSpec((1, d), lambda i: (0, 0)),
            pl.BlockSpec((1, 1, d), lambda i: (set_of_tile(i, tm), 0, 0)),
            pl.BlockSpec((1, 1, d), lambda i: (set_of_tile(i, tm), 0, 0)),
        ],
        out_specs=out_specs,
        compiler_params=_cparams(1),
        name="prenorm",
    )(x, w.reshape(1, d), shift, scale)
    return out if packed else out[0]


def _glu_kernel(y_ref, w_ref, o_ref, wbf_ref, *, tn):
    @pl.when(pl.program_id(1) == 0)
    def _():
        wbf_ref[...] = w_ref[...].astype(BF16)

    j = pl.program_id(0)
    g = jax.nn.gelu(y_ref[...].astype(F32))
    acc = jnp.dot(g.astype(BF16), wbf_ref[...], preferred_element_type=F32)
    gj = jax.nn.gelu(y_ref[:, pl.ds(pl.multiple_of(j * tn, tn), tn)].astype(F32))
    o_ref[...] = (gj * jax.nn.sigmoid(acc)).astype(o_ref.dtype)


def s5_glu(y, w, *, layer, rows, tm=None, tn=512):
    m, k = y.shape
    tm = _row_tile(rows, tm)
    tn = _col_tile(k, tn)
    return pl.pallas_call(
        functools.partial(_glu_kernel, tn=tn),
        out_shape=jax.ShapeDtypeStruct((rows, k), BF16),
        grid=(k // tn, rows // tm),
        in_specs=[
            pl.BlockSpec((tm, k), lambda j, i: (i, 0)),
            _wspec(w, layer, (k, tn), lambda j, i: (0, j)),
        ],
        out_specs=pl.BlockSpec((tm, tn), lambda j, i: (i, j)),
        scratch_shapes=[pltpu.VMEM((k, tn), BF16)],
        compiler_params=_cparams(2),
        name="s5_glu",
    )(y, w)


def _merge_kernel(a_ref, b_ref, ga_ref, gb_ref, wa_ref, wb_ref, o_ref, wabf_ref, wbbf_ref):
    @pl.when(pl.program_id(1) == 0)
    def _():
        wabf_ref[...] = wa_ref[...].astype(BF16)
        wbbf_ref[...] = wb_ref[...].astype(BF16)

    ya = jnp.dot(a_ref[...], wabf_ref[...], preferred_element_type=F32)
    yb = jnp.dot(b_ref[...], wbbf_ref[...], preferred_element_type=F32)
    m = jax.nn.sigmoid(ga_ref[...].astype(F32)) * ya + jax.nn.sigmoid(gb_ref[...].astype(F32)) * yb
    o_ref[...] = m.astype(o_ref.dtype)


def branch_merge(ya, yb, gates, wa, wb, *, layer, rows, tm=None, tn=512):
    ka, kb = ya.shape[1], yb.shape[1]
    d = wa.shape[-1]
    tm = _row_tile(rows, tm)
    tn = _col_tile(d, tn)
    nb = d // tn
    return pl.pallas_call(
        _merge_kernel,
        out_shape=jax.ShapeDtypeStruct((rows, d), BF16),
        grid=(nb, rows // tm),
        in_specs=[
            pl.BlockSpec((tm, ka), lambda j, i: (i, 0)),
            pl.BlockSpec((tm, kb), lambda j, i: (i, 0)),
            pl.BlockSpec((tm, tn), lambda j, i: (i, j)),
            pl.BlockSpec((tm, tn), lambda j, i: (i, j + nb)),
            _wspec(wa, layer, (ka, tn), lambda j, i: (0, j)),
            _wspec(wb, layer, (kb, tn), lambda j, i: (0, j)),
        ],
        out_specs=pl.BlockSpec((tm, tn), lambda j, i: (i, j)),
        scratch_shapes=[pltpu.VMEM((ka, tn), BF16), pltpu.VMEM((kb, tn), BF16)],
        compiler_params=_cparams(2),
        name="branch_merge",
    )(ya, yb, gates, gates, wa, wb)


def _outproj_kernel(m_ref, w_ref, x_ref, nw_ref, g_ref, o_ref, wbf_ref):
    @pl.when(pl.program_id(0) == 0)
    def _():
        wbf_ref[...] = w_ref[...].astype(BF16)

    y = jnp.dot(m_ref[...], wbf_ref[...], preferred_element_type=F32)
    yn = y * lax.rsqrt(jnp.mean(y * y, axis=-1, keepdims=True) + EPS) * nw_ref[...]
    o_ref[...] = x_ref[...] + g_ref[0] * yn


def outproj_residual(m, w, x, nw, gate, set_of_tile, *, layer, rows, tm=512):
    d = w.shape[-1]
    k = w.shape[-2]
    tm = _row_tile(rows, tm)
    return pl.pallas_call(
        _outproj_kernel,
        out_shape=jax.ShapeDtypeStruct((rows, d), F32),
        grid=(rows // tm,),
        in_specs=[
            pl.BlockSpec((tm, k), lambda i: (i, 0)),
            _wspec(w, layer, (k, d), lambda i: (0, 0), pipeline_mode=pl.Buffered(1)),
            pl.BlockSpec((tm, d), lambda i: (i, 0)),
            pl.BlockSpec((1, d), lambda i: (0, 0)),
            pl.BlockSpec((1, 1, d), lambda i: (set_of_tile(i, tm), 0, 0)),
        ],
        out_specs=pl.BlockSpec((tm, d), lambda i: (i, 0)),
        scratch_shapes=[pltpu.VMEM((k, d), BF16)],
        compiler_params=_cparams(1),
        name="outproj_residual",
    )(m, w, x, nw.reshape(1, d), gate)


def s5_operators(lam_re, lam_im, log_step, b_re, b_im, c_re, c_im, n_levels):
    tc = S5_CHUNK
    hp = lax.Precision.HIGHEST
    lr, li = lam_re.astype(F32), lam_im.astype(F32)
    step = jnp.exp(log_step.astype(F32))[..., None]

    def apow(l):
        mag = jnp.exp(lr * step * l)
        return mag * jnp.cos(li * step * l), mag * jnp.sin(li * step * l)

    ab_re, ab_im = apow(1.0)
    den = lr * lr + li * li
    nr = ab_re - 1.0
    cr = (nr * lr + ab_im * li) / den
    ci = (ab_im * lr - nr * li) / den
    br, bi = b_re.astype(F32), b_im.astype(F32)
    bb_re = cr[..., None] * br - ci[..., None] * bi
    bb_im = cr[..., None] * bi + ci[..., None] * br
    lags = jnp.arange(tc + 1, dtype=F32)[:, None, None, None]
    pw_re, pw_im = apow(lags)
    pw_re, pw_im = jnp.moveaxis(pw_re, 0, 2), jnp.moveaxis(pw_im, 0, 2)
    cre, cim = c_re.astype(F32), c_im.astype(F32)
    cp_re = cre[:, :, None] * pw_re[:, :, :, None] - cim[:, :, None] * pw_im[:, :, :, None]
    cp_im = cre[:, :, None] * pw_im[:, :, :, None] + cim[:, :, None] * pw_re[:, :, :, None]
    bbt = jnp.concatenate([bb_re.transpose(0, 1, 3, 2), -bb_im.transpose(0, 1, 3, 2)], axis=-1)
    cps = jnp.concatenate([cp_re[:, :, :tc], cp_im[:, :, :tc]], axis=-1)
    cps = jnp.stack([cps[0], cps[1, :, ::-1]])
    cps = cps.transpose(0, 1, 4, 2, 3)
    pr, pi = pw_re[:, :, tc - 1 - jnp.arange(tc)], pw_im[:, :, tc - 1 - jnp.arange(tc)]
    win_re = pr[:, :, :, None, :] * bb_re.transpose(0, 1, 3, 2)[:, :, None] - pi[:, :, :, None, :] * bb_im.transpose(0, 1, 3, 2)[:, :, None]
    win_im = pr[:, :, :, None, :] * bb_im.transpose(0, 1, 3, 2)[:, :, None] + pi[:, :, :, None, :] * bb_re.transpose(0, 1, 3, 2)[:, :, None]
    win = jnp.concatenate([win_re, win_im], axis=-1)
    wo_re = cp_re[:, :, 1:].transpose(0, 1, 4, 2, 3)
    wo_im = -cp_im[:, :, 1:].transpose(0, 1, 4, 2, 3)
    wout = jnp.concatenate([wo_re, wo_im], axis=2)
    win = jnp.stack([win[0], win[1, :, ::-1]])
    wout = jnp.stack([wout[0], wout[1, :, :, ::-1]])
    g = lr.shape[1]
    j = br.shape[-1]
    p = lr.shape[-1]
    rows = []
    for k in range(n_levels):
        ar, ai = apow(float(tc * 2 ** k))
        rows.append(jnp.concatenate([ar, ar], axis=-1))
        rows.append(jnp.concatenate([-ai, ai], axis=-1))
    apw = jnp.stack(rows, axis=2)
    return ((bbt.transpose(1, 0, 2, 3), cps.reshape(2, g, 2 * p, tc * j).transpose(1, 0, 2, 3)),
            win.reshape(2, g, tc * j, 2 * p).transpose(1, 0, 2, 3).astype(BF16),
            wout.reshape(2, g, 2 * p, tc * j).transpose(1, 0, 2, 3).astype(BF16),
            apw.transpose(1, 0, 2, 3))


def _s5_kernel(u_ref, bbt_ref, cps_ref, win_ref, wout_ref, apw_ref, dsk_ref, y_ref, *, nb, ctx_rows, n_levels, p):
    u = u_ref[0]
    n, lanes = u.shape
    jw = bbt_ref.shape[2]
    tc = lanes // jw
    row = lax.broadcasted_iota(jnp.int32, (n, 1), 0)
    lane = lax.broadcasted_iota(jnp.int32, (jw, lanes), 1)
    y = u.astype(F32) * dsk_ref[0]
    for d in range(2):
        kt = jnp.dot(bbt_ref[0, d], cps_ref[0, d], precision=lax.Precision.HIGHEST, preferred_element_type=F32)
        blocks = []
        for s in range(tc):
            sh = jw * s if d == 0 else jw * (tc - 1 - s)
            if sh == 0:
                blocks.append(kt)
            elif d == 0:
                blocks.append(jnp.where(lane >= sh, pltpu.roll(kt, sh, 1), 0.0))
            else:
                blocks.append(jnp.where(lane < lanes - sh, pltpu.roll(kt, lanes - sh, 1), 0.0))
        tmat = jnp.concatenate(blocks, axis=0).astype(BF16)
        y = y + jnp.dot(u, tmat, preferred_element_type=F32)
        x = jnp.dot(u, win_ref[0, d], preferred_element_type=F32)
        if d == 0:
            def shift(a, s):
                return jnp.where(row >= s, pltpu.roll(a, s, 0), 0.0)
        else:
            if ctx_rows:
                x = pltpu.roll(x, n - ctx_rows, 0)

            def shift(a, s):
                return jnp.where(row < n - s, pltpu.roll(a, n - s, 0), 0.0)
        x = shift(x, nb)
        for k in range(n_levels):
            sh = shift(x, nb * 2 ** k)
            a1 = apw_ref[0, d, 2 * k:2 * k + 1, :]
            a2 = apw_ref[0, d, 2 * k + 1:2 * k + 2, :]
            x = x + a1 * sh + a2 * pltpu.roll(sh, p, 1)
        if d == 1 and ctx_rows:
            x = pltpu.roll(x, ctx_rows, 0)
        y = y + jnp.dot(x.astype(BF16), wout_ref[0, d], preferred_element_type=F32)
    y_ref[0] = y.astype(y_ref.dtype)


def s5_scan(uc, toep, win, wout, apw, dsk, *, nb, ctx_rows, n_levels):
    g, n, lanes = uc.shape
    p2 = win.shape[-1]
    bbt, cps = toep
    jw = bbt.shape[2]
    return pl.pallas_call(
        functools.partial(_s5_kernel, nb=nb, ctx_rows=ctx_rows, n_levels=n_levels, p=p2 // 2),
        out_shape=jax.ShapeDtypeStruct((g, n, lanes), BF16),
        grid=(g,),
        in_specs=[
            pl.BlockSpec((1, n, lanes), lambda i: (i, 0, 0)),
            pl.BlockSpec((1, 2, jw, p2), lambda i: (i, 0, 0, 0)),
            pl.BlockSpec((1, 2, p2, lanes), lambda i: (i, 0, 0, 0)),
            pl.BlockSpec((1, 2, lanes, p2), lambda i: (i, 0, 0, 0)),
            pl.BlockSpec((1, 2, p2, lanes), lambda i: (i, 0, 0, 0)),
            pl.BlockSpec((1, 2, 2 * n_levels, p2), lambda i: (i, 0, 0, 0)),
            pl.BlockSpec((1, 1, lanes), lambda i: (i, 0, 0)),
        ],
        out_specs=pl.BlockSpec((1, n, lanes), lambda i: (i, 0, 0)),
        compiler_params=_cparams(1),
        name="s5_scan",
    )(uc, bbt, cps, win, wout, apw, dsk)


def s5_mix(u_rows, ops, d_skip, *, bsz, seq, ctx):
    toep, win, wout, apw = ops
    g = win.shape[0]
    lanes = win.shape[2]
    tc = S5_CHUNK
    j = lanes // tc
    hgt = seq // GRID_W
    ul = u_rows[:bsz * seq].reshape(bsz, hgt, GRID_W, g, j).transpose(0, 2, 1, 3, 4).reshape(bsz, seq, g, j)
    uc = u_rows[bsz * seq:].reshape(bsz, ctx, g, j)
    useq = jnp.concatenate([uc, ul], axis=1)
    nc = (seq + ctx) // tc
    uch = useq.reshape(bsz, nc, tc, g, j).transpose(3, 1, 0, 2, 4).reshape(g, nc * bsz, lanes)
    n_levels = apw.shape[2] // 2
    dsk = jnp.tile(d_skip.astype(F32).reshape(g, 1, j), (1, tc, 1)).reshape(g, 1, lanes)
    ych = s5_scan(uch.astype(BF16), toep, win, wout, apw, dsk, nb=bsz, ctx_rows=(ctx // tc) * bsz, n_levels=n_levels)
    yseq = ych.reshape(g, nc, bsz, tc, j).transpose(2, 1, 3, 0, 4).reshape(bsz, seq + ctx, g * j)
    yc = yseq[:, :ctx].reshape(bsz * ctx, g * j)
    yl = yseq[:, ctx:].reshape(bsz, GRID_W, hgt, g * j).transpose(0, 2, 1, 3).reshape(bsz * seq, g * j)
    return jnp.concatenate([yl, yc], axis=0)


def _softplus(x):
    return jnp.maximum(x, 0.0) + jnp.log1p(jnp.exp(-jnp.abs(x)))


def _dn_kernel(ql_ref, kl_ref, vl_ref, qc_ref, kc_ref, vc_ref, bal_ref, bac_ref, zl_ref, zc_ref,
               cw_ref, lp_ref, nw_ref, ol_ref, oc_ref,
               nt_s, w2t_s, qp_s, el_s, o_s, tok_s, *, n_heads, ctx):
    c = DN_CHUNK
    seq = ql_ref.shape[0]
    t = seq + ctx
    dk = ql_ref.shape[1]
    nch = t // c
    ncc = ctx // c
    h = pl.program_id(1)
    row = lax.broadcasted_iota(jnp.int32, (t, 1), 0)
    rowc = row % c
    lane = lax.broadcasted_iota(jnp.int32, (1, dk), 1)

    first = (row == 0) | (row == ctx)
    last = (row == ctx - 1) | (row == t - 1)

    def conv_silu(xc_ref, xl_ref, kind):
        x = jnp.concatenate([xc_ref[...], xl_ref[...]], axis=0).astype(F32)
        xp = jnp.where(first, 0.0, pltpu.roll(x, 1, 0))
        xn = jnp.where(last, 0.0, pltpu.roll(x, t - 1, 0))
        w = cw_ref[0, kind]
        return _silu(xp * w[0:1] + x * w[1:2] + xn * w[2:3])

    def l2n(x):
        return x * lax.rsqrt(jnp.sum(x * x, axis=-1, keepdims=True) + EPS)

    q = l2n(conv_silu(qc_ref, ql_ref, 0)) * (dk ** -0.5)
    k = l2n(conv_silu(kc_ref, kl_ref, 1))
    v = conv_silu(vc_ref, vl_ref, 2)

    @pl.when(h == 0)
    def _():
        ba = jnp.concatenate([bac_ref[...], bal_ref[...]], axis=0)
        g_all = -lp_ref[0:1, :] * _softplus(ba + lp_ref[1:2, :])
        pf, sf = g_all, g_all
        s = 1
        while s < c:
            pf = pf + jnp.where(rowc >= s, pltpu.roll(pf, s, 0), 0.0)
            sf = sf + jnp.where(rowc < c - s, pltpu.roll(sf, t - s, 0), 0.0)
            s *= 2
        tok_s[0] = jax.nn.sigmoid(ba)
        tok_s[1] = pf
        tok_s[2] = sf

    beta_all, pf, sf = tok_s[0], tok_s[1], tok_s[2]

    def col(a, idx):
        return jnp.sum(jnp.where(lane == idx, a, 0.0), axis=1, keepdims=True)

    causal_f = (lax.broadcasted_iota(jnp.int32, (c, c), 0) >= lax.broadcasted_iota(jnp.int32, (c, c), 1))[None]
    strict_f = (lax.broadcasted_iota(jnp.int32, (c, c), 0) > lax.broadcasted_iota(jnp.int32, (c, c), 1))[None]
    causal_b = (lax.broadcasted_iota(jnp.int32, (c, c), 0) <= lax.broadcasted_iota(jnp.int32, (c, c), 1))[None]
    strict_b = (lax.broadcasted_iota(jnp.int32, (c, c), 0) < lax.broadcasted_iota(jnp.int32, (c, c), 1))[None]
    eye = (lax.broadcasted_iota(jnp.int32, (c, c), 0) == lax.broadcasted_iota(jnp.int32, (c, c), 1))[None].astype(F32)

    sub_blk = (lax.broadcasted_iota(jnp.int32, (c, c), 0) // DN_SUB
               == lax.broadcasted_iota(jnp.int32, (c, c), 1) // DN_SUB)[None]

    def neg_pow_inverse(x, m, limit):
        pinv = eye + x
        while m < limit:
            xb = x.astype(BF16)
            x = jnp.einsum('cij,cjk->cik', xb, xb, preferred_element_type=F32)
            pinv = pinv + jnp.einsum('cij,cjk->cik', pinv.astype(BF16), x.astype(BF16), preferred_element_type=F32)
            m *= 2
        return pinv

    q3 = q.reshape(nch, c, dk)
    k3 = k.reshape(nch, c, dk)
    v3 = v.reshape(nch, c, dk)
    k3b = k3.astype(BF16)
    q3b = q3.astype(BF16)
    for d in range(2):
        causal, strict = (causal_f, strict_f) if d == 0 else (causal_b, strict_b)
        beta = col(beta_all, d * n_heads + h)
        gc = col(pf if d == 0 else sf, (2 + d) * n_heads + h)
        hi = gc.astype(BF16).astype(F32)
        mid = (gc - hi).astype(BF16).astype(F32)
        lo = gc - hi - mid
        pieces = (hi, mid, lo)
        g1 = jnp.zeros((t, dk), F32)
        g2 = jnp.zeros((t, dk), F32)
        for n_p, piece in enumerate(pieces):
            pb = jnp.broadcast_to(piece, (t, dk))
            g1 = jnp.where(lane == n_p, pb, jnp.where(lane == 3 + n_p, 1.0, g1))
            g2 = jnp.where(lane == n_p, 1.0, jnp.where(lane == 3 + n_p, -pb, g2))
        ldiff = jnp.einsum('cid,cjd->cij', g1.astype(BF16).reshape(nch, c, dk), g2.astype(BF16).reshape(nch, c, dk),
                           preferred_element_type=F32)
        decay = jnp.where(causal, jnp.exp(jnp.where(causal, ldiff, 0.0)), 0.0)
        beta3 = beta.reshape(nch, c, 1)
        gc3 = gc.reshape(nch, c, 1)
        glast3 = gc3[:, c - 1:c, :] if d == 0 else gc3[:, 0:1, :]
        kb = k3 * beta3
        a = jnp.einsum('cid,cjd->cij', kb.astype(BF16), k3b, preferred_element_type=F32)
        a = jnp.where(strict, a * decay, 0.0)
        qk = jnp.einsum('cid,cjd->cij', q3b, k3b, preferred_element_type=F32)
        qk = jnp.where(causal, qk * decay, 0.0)
        a_diag = jnp.where(sub_blk, a, 0.0)
        dinv = neg_pow_inverse(-a_diag, 2, DN_SUB)
        n_off = jnp.einsum('cij,cjk->cik', dinv.astype(BF16), (a - a_diag).astype(BF16), preferred_element_type=F32)
        pinv = neg_pow_inverse(-n_off, 2 * DN_SUB, c)
        pinv = jnp.einsum('cij,cjk->cik', pinv.astype(BF16), dinv.astype(BF16), preferred_element_type=F32)
        rhs = jnp.concatenate([v3 * beta3, kb * jnp.exp(gc3)], axis=-1)
        sol = jnp.einsum('cij,cjd->cid', pinv.astype(BF16), rhs.astype(BF16), preferred_element_type=F32)
        solb = sol.astype(BF16)
        qs = jnp.einsum('cij,cjd->cid', qk.astype(BF16), solb, preferred_element_type=F32)
        o_s[d] = qs[:, :, :dk].reshape(t, dk)
        qp_s[d] = (q3 * jnp.exp(gc3) - qs[:, :, dk:]).reshape(t, dk).astype(BF16)
        ke = (k3 * jnp.exp(glast3 - gc3)).astype(BF16)
        solt = jnp.swapaxes(sol, 1, 2).astype(BF16)
        nw2 = jnp.einsum('cdi,cik->cdk', solt, ke, preferred_element_type=F32)
        nt_s[d] = nw2[:, :dk, :]
        w2t_s[d] = nw2[:, dk:, :].astype(BF16)
        el_s[d] = jnp.broadcast_to(jnp.exp(glast3), (nch, 8, dk))

    def chunk_step(d, ci, st):
        r0 = pl.multiple_of(ci * c, c)
        stb = st.astype(BF16)
        o_s[d, pl.ds(r0, c), :] += lax.dot_general(qp_s[d, pl.ds(r0, c), :], stb, (((1,), (1,)), ((), ())),
                                                   preferred_element_type=F32)
        return (st * el_s[d, ci][0:1, :] + nt_s[d, ci]
                - jnp.dot(stb, w2t_s[d, ci], preferred_element_type=F32))

    def ctx_body(n, carry):
        return chunk_step(0, n, carry[0]), chunk_step(1, ncc - 1 - n, carry[1])

    def lat_body(n, carry):
        return chunk_step(0, ncc + n, carry[0]), chunk_step(1, nch - 1 - n, carry[1])

    zero = jnp.zeros((dk, dk), F32)
    carry = lax.fori_loop(0, ncc, ctx_body, (zero, zero))
    lax.fori_loop(0, nch - ncc, lat_body, carry)

    o = o_s[0] + o_s[1]
    on = o * lax.rsqrt(jnp.mean(o * o, axis=-1, keepdims=True) + EPS) * nw_ref[...]
    z = jnp.concatenate([zc_ref[...], zl_ref[...]], axis=0).astype(F32)
    out = (on * _silu(z)).astype(ol_ref.dtype)
    oc_ref[...] = out[:ctx]
    ol_ref[...] = out[ctx:]


def deltanet_mix(p_main, p_ba, conv_w, a_log, dt_bias, norm_w, *, bsz, seq, ctx, u_width):
    n_heads = a_log.shape[-1]
    dk = norm_w.shape[-1]
    c = DN_CHUNK
    t = seq + ctx
    nch = t // c
    assert seq % c == 0 and ctx % c == 0 and u_width % dk == 0 and 4 * n_heads <= p_ba.shape[1]
    cb = u_width // dk
    lat_rows = bsz * seq
    cw = conv_w.astype(F32).reshape(conv_w.shape[0], 3, n_heads, dk).transpose(2, 1, 0, 3)
    lanes = p_ba.shape[1]
    lp = jnp.zeros((2, lanes), F32)
    lp = lp.at[0, 2 * n_heads:4 * n_heads].set(jnp.exp(a_log.astype(F32)).reshape(-1))
    lp = lp.at[1, 2 * n_heads:4 * n_heads].set(dt_bias.astype(F32).reshape(-1))
    cblk = lat_rows // ctx

    def lat_spec(off):
        return pl.BlockSpec((seq, dk), lambda b, h: (b, off + h))

    def ctx_spec(off):
        return pl.BlockSpec((ctx, dk), lambda b, h: (cblk + b, off + h))

    yl, yc = pl.pallas_call(
        functools.partial(_dn_kernel, n_heads=n_heads, ctx=ctx),
        out_shape=(jax.ShapeDtypeStruct((lat_rows, n_heads * dk), BF16),
                   jax.ShapeDtypeStruct((bsz * ctx, n_heads * dk), BF16)),
        grid=(bsz, n_heads),
        in_specs=[
            lat_spec(cb), lat_spec(cb + n_heads), lat_spec(cb + 2 * n_heads),
            ctx_spec(cb), ctx_spec(cb + n_heads), ctx_spec(cb + 2 * n_heads),
            pl.BlockSpec((seq, lanes), lambda b, h: (b, 0)),
            pl.BlockSpec((ctx, lanes), lambda b, h: (cblk + b, 0)),
            lat_spec(cb + 3 * n_heads), ctx_spec(cb + 3 * n_heads),
            pl.BlockSpec((1, 3, conv_w.shape[0], dk), lambda b, h: (h, 0, 0, 0)),
            pl.BlockSpec((2, lanes), lambda b, h: (0, 0)),
            pl.BlockSpec((1, dk), lambda b, h: (0, 0)),
        ],
        out_specs=(pl.BlockSpec((seq, dk), lambda b, h: (b, h)),
                   pl.BlockSpec((ctx, dk), lambda b, h: (b, h))),
        scratch_shapes=[
            pltpu.VMEM((2, nch, dk, dk), F32),
            pltpu.VMEM((2, nch, dk, dk), BF16),
            pltpu.VMEM((2, t, dk), BF16),
            pltpu.VMEM((2, nch, 8, dk), F32),
            pltpu.VMEM((2, t, dk), F32),
            pltpu.VMEM((3, t, lanes), F32),
        ],
        compiler_params=_cparams(2),
        name="deltanet",
    )(p_main, p_main, p_main, p_main, p_main, p_main, p_ba, p_ba, p_main, p_main,
      cw, lp, norm_w.astype(F32).reshape(1, dk))
    return jnp.concatenate([yl, yc], axis=0)


def _store_token_rows(ref2d, x):
    n, d = x.shape
    assert d == ROW_SUB * 2 * LANES, (d, ROW_SUB)
    for m in range(ROW_SUB):
        lo = x[:, 2 * m * LANES:(2 * m + 1) * LANES].astype(BF16).astype(F32)
        hi = x[:, (2 * m + 1) * LANES:(2 * m + 2) * LANES].astype(BF16).astype(F32)
        word = ((lax.bitcast_convert_type(hi, U32) & jnp.uint32(0xFFFF0000))
                | (lax.bitcast_convert_type(lo, U32) >> 16))
        ref2d[pl.ds(m, n, stride=ROW_SUB), :] = word


def _load_token_rows(ref2d, first_token, n_tokens):
    cols = []
    for m in range(ROW_SUB):
        word = ref2d[pl.ds(first_token * ROW_SUB + m, n_tokens, stride=ROW_SUB), :]
        cols.append(lax.bitcast_convert_type(word << 16, F32))
        cols.append(lax.bitcast_convert_type(word & jnp.uint32(0xFFFF0000), F32))
    return jnp.concatenate(cols, axis=1)


def _row_gather(table_hbm, idx_ref, buf, sem, n_rows):
    def copy(r, src_row):
        return pltpu.make_async_copy(table_hbm.at[pl.ds(pl.multiple_of(src_row * ROW_SUB, ROW_SUB), ROW_SUB)],
                                     buf.at[pl.ds(pl.multiple_of(r * ROW_SUB, ROW_SUB), ROW_SUB)], sem)

    def start():
        def body(q, carry):
            copy(2 * q, idx_ref[0, 0, 2 * q]).start(priority=0)
            copy(2 * q + 1, idx_ref[0, 0, 2 * q + 1]).start(priority=1)
            return carry
        lax.fori_loop(0, n_rows // 2, body, 0, unroll=4)

    def wait():
        def body(r, carry):
            copy(r, 0).wait()
            return carry
        lax.fori_loop(0, n_rows, body, 0, unroll=8)

    return start, wait


def _dispatch_kernel(cnt_ref, ps_ref, pe_ref, dst_ref, h_ref, xg_hbm, zrow, sem, *, top_k):
    i = pl.program_id(0)
    tm = h_ref.shape[0] // ROW_SUB
    n_exp = cnt_ref.shape[0]

    def row_copy(t, dst_row):
        return pltpu.make_async_copy(h_ref.at[pl.ds(pl.multiple_of(t * ROW_SUB, ROW_SUB), ROW_SUB)],
                                     xg_hbm.at[pl.ds(pl.multiple_of(dst_row * ROW_SUB, ROW_SUB), ROW_SUB)], sem.at[0])

    for k in range(top_k):
        def start(q, carry, k=k):
            row_copy(2 * q, dst_ref[0, 0, k * tm + 2 * q]).start(priority=0)
            row_copy(2 * q + 1, dst_ref[0, 0, k * tm + 2 * q + 1]).start(priority=1)
            return carry
        lax.fori_loop(0, tm // 2, start, 0, unroll=4)

    def wait(r, carry):
        row_copy(0, 0).wait()
        return carry

    lax.fori_loop(0, top_k * tm, wait, 0, unroll=8)

    @pl.when(i == pl.num_programs(0) - 1)
    def _():
        zrow[...] = jnp.zeros_like(zrow)

        def pad_copy(s):
            return pltpu.make_async_copy(zrow, xg_hbm.at[pl.ds(pl.multiple_of(s * ROW_SUB, ROW_SUB), ROW_SUB)],
                                         sem.at[1])

        def zero_fill(lo, hi):
            lax.fori_loop(lo, hi, lambda s, c: (pad_copy(s).start(), c)[1], 0)
            lax.fori_loop(lo, hi, lambda s, c: (pad_copy(s).wait(), c)[1], 0)

        def per_expert(e, carry):
            zero_fill(ps_ref[e] + cnt_ref[e], pe_ref[e])
            return carry

        lax.fori_loop(0, n_exp, per_expert, 0)
        zero_fill(pe_ref[n_exp - 1], xg_hbm.shape[0] // ROW_SUB)


def dispatch_rows(h_packed, dest, counts, pad_start, pad_end, *, n_slots, rows, tm=128):
    top_k = dest.shape[0]
    tm = _row_tile(rows, tm)
    n_tiles = rows // tm
    dst3 = dest.reshape(top_k, n_tiles, tm).transpose(1, 0, 2).reshape(n_tiles, 1, top_k * tm)
    return pl.pallas_call(
        functools.partial(_dispatch_kernel, top_k=top_k),
        out_shape=jax.ShapeDtypeStruct((n_slots * ROW_SUB, LANES), U32),
        grid_spec=pltpu.PrefetchScalarGridSpec(
            num_scalar_prefetch=3,
            grid=(n_tiles,),
            in_specs=[
                pl.BlockSpec((1, 1, top_k * tm), lambda i, c, s, e: (i, 0, 0), memory_space=pltpu.SMEM),
                pl.BlockSpec((tm * ROW_SUB, LANES), lambda i, c, s, e: (i, 0)),
            ],
            out_specs=pl.BlockSpec(memory_space=pl.ANY),
            scratch_shapes=[pltpu.VMEM((ROW_SUB, LANES), U32), pltpu.SemaphoreType.DMA((2,))],
        ),
        compiler_params=_cparams(1),
        name="dispatch_rows",
    )(counts, pad_start, pad_end, dst3, h_packed)


def _experts_kernel(be_ref, nxt_ref, nv_ref, x_ref, wg_hbm, wu_hbm, wd_hbm, o_ref,
                    wgs, wus, wds, wgb, wub, wdb, wsem):
    i = pl.program_id(0)
    nv = nv_ref[0]
    mb = x_ref.shape[0] // ROW_SUB

    def weight_copies(e):
        return (pltpu.make_async_copy(wg_hbm.at[e], wgs, wsem.at[0]),
                pltpu.make_async_copy(wu_hbm.at[e], wus, wsem.at[1]),
                pltpu.make_async_copy(wd_hbm.at[e], wds, wsem.at[2]))

    @pl.when((i == 0) & (nv > 0))
    def _():
        for cp in weight_copies(be_ref[0]):
            cp.start()

    prev = be_ref[jnp.maximum(i - 1, 0)]

    @pl.when((i < nv) & ((i == 0) | (be_ref[i] != prev)))
    def _():
        for cp in weight_copies(be_ref[i]):
            cp.wait()
        wgb[...] = wgs[...].astype(BF16)
        wub[...] = wus[...].astype(BF16)
        wdb[...] = wds[...].astype(BF16)

        @pl.when(nxt_ref[i] >= 0)
        def _():
            for cp in weight_copies(nxt_ref[i]):
                cp.start()

    @pl.when(i < nv)
    def _():
        x = _load_token_rows(x_ref, 0, mb).astype(BF16)
        g = jnp.dot(x, wgb[...], preferred_element_type=F32)
        u = jnp.dot(x, wub[...], preferred_element_type=F32)
        a = (_silu(g) * u).astype(BF16)
        _store_token_rows(o_ref, jnp.dot(a, wdb[...], preferred_element_type=F32))

    @pl.when(i >= nv)
    def _():
        o_ref[...] = jnp.zeros_like(o_ref)


def routed_experts(xg, block_e, next_e, n_valid, w_gate, w_up, w_down):
    d = w_gate.shape[-2]
    f = w_gate.shape[-1]
    mb = MOE_BLOCK
    n_blocks = xg.shape[0] // (mb * ROW_SUB)
    lw = LANES
    w_gate, w_up = w_gate.reshape(-1, d, f), w_up.reshape(-1, d, f)
    w_down = w_down.reshape(-1, f, d)
    return pl.pallas_call(
        _experts_kernel,
        out_shape=jax.ShapeDtypeStruct((n_blocks * mb * ROW_SUB, lw), U32),
        grid_spec=pltpu.PrefetchScalarGridSpec(
            num_scalar_prefetch=3,
            grid=(n_blocks,),
            in_specs=[
                pl.BlockSpec((mb * ROW_SUB, lw), lambda i, be, nx, nv: (jnp.minimum(i, jnp.maximum(nv[0] - 1, 0)), 0)),
                pl.BlockSpec(memory_space=pl.ANY),
                pl.BlockSpec(memory_space=pl.ANY),
                pl.BlockSpec(memory_space=pl.ANY),
            ],
            out_specs=pl.BlockSpec((mb * ROW_SUB, lw), lambda i, be, nx, nv: (i, 0)),
            scratch_shapes=[pltpu.VMEM((d, f), F32), pltpu.VMEM((d, f), F32), pltpu.VMEM((f, d), F32),
                            pltpu.VMEM((d, f), BF16), pltpu.VMEM((d, f), BF16), pltpu.VMEM((f, d), BF16),
                            pltpu.SemaphoreType.DMA((3,))],
        ),
        compiler_params=_cparams(1),
        name="routed_experts",
    )(block_e, next_e, n_valid, xg, w_gate, w_up, w_down)


def _swiglu_kernel(x_ref, wg_ref, wu_ref, wd_ref, o_ref, wgb, wub, wdb):
    @pl.when(pl.program_id(0) == 0)
    def _():
        wgb[...] = wg_ref[...].astype(BF16)
        wub[...] = wu_ref[...].astype(BF16)
        wdb[...] = wd_ref[...].astype(BF16)

    x = x_ref[...].astype(BF16)
    g = jnp.dot(x, wgb[...], preferred_element_type=F32)
    u = jnp.dot(x, wub[...], preferred_element_type=F32)
    a = (_silu(g) * u).astype(BF16)
    o_ref[...] = jnp.dot(a, wdb[...], preferred_element_type=F32).astype(o_ref.dtype)


def shared_expert(x, wg, wu, wd, *, layer, rows, tm=None):
    d = x.shape[1]
    f = wg.shape[-1]
    tm = _row_tile(rows, tm)
    const = lambda i: (0, 0)
    return pl.pallas_call(
        _swiglu_kernel,
        out_shape=jax.ShapeDtypeStruct((rows, d), F32),
        grid=(rows // tm,),
        in_specs=[
            pl.BlockSpec((tm, d), lambda i: (i, 0)),
            _wspec(wg, layer, (d, f), const, pipeline_mode=pl.Buffered(1)),
            _wspec(wu, layer, (d, f), const, pipeline_mode=pl.Buffered(1)),
            _wspec(wd, layer, (f, d), const, pipeline_mode=pl.Buffered(1)),
        ],
        out_specs=pl.BlockSpec((tm, d), lambda i: (i, 0)),
        scratch_shapes=[pltpu.VMEM((d, f), BF16), pltpu.VMEM((d, f), BF16), pltpu.VMEM((f, d), BF16)],
        compiler_params=_cparams(1),
        name="shared_expert",
    )(x, wg, wu, wd)


def _ffn_res_kernel(dst_ref, dstn_ref, y_hbm, w_ref, s_ref, x_ref, nw_ref, g_ref, o_ref, buf, sem, *, top_k):
    i = pl.program_id(0)
    tm = x_ref.shape[0]
    nr = top_k * tm
    slot = i % 2
    start_first, _ = _row_gather(y_hbm, dst_ref, buf.at[0], sem.at[0], nr)
    start_next, _ = _row_gather(y_hbm, dstn_ref, buf.at[1 - slot], sem.at[1 - slot], nr)
    _, wait_cur = _row_gather(y_hbm, dst_ref, buf.at[slot], sem.at[slot], nr)

    @pl.when(i == 0)
    def _():
        start_first()

    @pl.when(i + 1 < pl.num_programs(0))
    def _():
        start_next()

    wait_cur()
    w = w_ref[...]
    f = s_ref[...]
    for k in range(top_k):
        f = f + _load_token_rows(buf.at[slot], k * tm, tm) * w[:, k:k + 1]
    fn = f * lax.rsqrt(jnp.mean(f * f, axis=-1, keepdims=True) + EPS) * nw_ref[...]
    o_ref[...] = x_ref[...] + g_ref[0] * fn


def ffn_residual(y, dest, wts, shared, x, nw, gate, set_of_tile, *, rows, tm=128):
    d = x.shape[1]
    lw = LANES
    top_k = wts.shape[1]
    tm = _row_tile(rows, tm)
    n_tiles = rows // tm
    dst3 = dest.reshape(top_k, n_tiles, tm).transpose(1, 0, 2).reshape(n_tiles, 1, top_k * tm)
    return pl.pallas_call(
        functools.partial(_ffn_res_kernel, top_k=top_k),
        out_shape=jax.ShapeDtypeStruct((rows, d), F32),
        grid=(n_tiles,),
        in_specs=[
            pl.BlockSpec((1, 1, top_k * tm), lambda i: (i, 0, 0), memory_space=pltpu.SMEM),
            pl.BlockSpec((1, 1, top_k * tm), lambda i: (jnp.minimum(i + 1, n_tiles - 1), 0, 0),
                         memory_space=pltpu.SMEM),
            pl.BlockSpec(memory_space=pl.ANY),
            pl.BlockSpec((tm, top_k), lambda i: (i, 0)),
            pl.BlockSpec((tm, d), lambda i: (i, 0)),
            pl.BlockSpec((tm, d), lambda i: (i, 0)),
            pl.BlockSpec((1, d), lambda i: (0, 0)),
            pl.BlockSpec((1, 1, d), lambda i: (set_of_tile(i, tm), 0, 0)),
        ],
        out_specs=pl.BlockSpec((tm, d), lambda i: (i, 0)),
        scratch_shapes=[pltpu.VMEM((2, top_k * tm * ROW_SUB, lw), U32), pltpu.SemaphoreType.DMA((2,))],
        compiler_params=_cparams(1),
        name="ffn_residual",
    )(dst3, dst3, y, wts, shared, x, nw.reshape(1, d), gate)


def _route_kernel(h_ref, wr_ref, b_ref, eidx_ref, wts_ref, rank_ref, cnt_ref, wrb, tri, cnt_s, *,
                  n_groups, topk_groups, top_k):
    n_exp, tm = wr_ref.shape[0], h_ref.shape[0]
    gs = n_exp // n_groups
    ninf = -jnp.inf

    @pl.when(pl.program_id(0) == 0)
    def _():
        wrb[...] = wr_ref[...].astype(BF16)
        tri[...] = (lax.broadcasted_iota(jnp.int32, (tm, tm), 0)
                    < lax.broadcasted_iota(jnp.int32, (tm, tm), 1)).astype(BF16)
        cnt_s[...] = jnp.zeros_like(cnt_s)

    logits = lax.dot_general(wrb[...], h_ref[...].astype(BF16), (((1,), (1,)), ((), ())),
                             preferred_element_type=F32)
    scores = jax.nn.sigmoid(logits)
    sel = scores + b_ref[:, 0:1]
    s3 = sel.reshape(n_groups, gs, tm)
    io3 = lax.broadcasted_iota(jnp.int32, (n_groups, gs, tm), 1)
    m1 = jnp.max(s3, axis=1, keepdims=True)
    i1 = jnp.min(jnp.where(s3 == m1, io3, gs), axis=1, keepdims=True)
    m2 = jnp.max(jnp.where(io3 == i1, ninf, s3), axis=1, keepdims=True)
    gscore = (m1 + m2).reshape(n_groups, tm)
    iog = lax.broadcasted_iota(jnp.int32, (n_groups, tm), 0)
    gsel = jnp.zeros((n_groups, tm), jnp.bool_)
    for _ in range(topk_groups):
        gm = jnp.max(gscore, axis=0, keepdims=True)
        gi = jnp.min(jnp.where(gscore == gm, iog, n_groups), axis=0, keepdims=True)
        hit = iog == gi
        gsel = gsel | hit
        gscore = jnp.where(hit, ninf, gscore)
    x = jnp.where(gsel.reshape(n_groups, 1, tm), s3, ninf).reshape(n_exp, tm)
    ioe = lax.broadcasted_iota(jnp.int32, (n_exp, tm), 0)
    hits = []
    chosen = jnp.zeros((n_exp, tm), jnp.bool_)
    for k in range(top_k):
        m = jnp.max(x, axis=0, keepdims=True)
        idx = jnp.min(jnp.where(x == m, ioe, n_exp), axis=0, keepdims=True)
        hit = ioe == idx
        x = jnp.where(hit, ninf, x)
        chosen = chosen | hit
        hits.append(hit)
        eidx_ref[k:k + 1, :] = idx
    wsel = jnp.where(chosen, scores, 0.0)
    wd = wsel / jnp.sum(wsel, axis=0, keepdims=True) * ROUTE_SCALE
    cm = jnp.where(chosen, 1.0, 0.0)
    rank = jnp.dot(cm.astype(BF16), tri[...], preferred_element_type=F32) + cnt_s[:, 0:1]
    for k in range(top_k):
        wts_ref[k:k + 1, :] = jnp.sum(jnp.where(hits[k], wd, 0.0), axis=0, keepdims=True)
        rank_ref[k:k + 1, :] = jnp.sum(jnp.where(hits[k], rank, 0.0), axis=0, keepdims=True).astype(jnp.int32)
    cnt_s[...] = cnt_s[...] + jnp.sum(cm, axis=1, keepdims=True)
    cnt_ref[...] = cnt_s[...].astype(jnp.int32)


def route(h, w_router, e_bias, *, rows, tm=None):
    d = h.shape[1]
    n_exp = w_router.shape[1]
    tm = _row_tile(rows, tm)
    bias = jnp.broadcast_to(e_bias.astype(F32).reshape(n_exp, 1), (n_exp, LANES))
    kt = lambda i: (0, i)
    eidx, wts, rank, cnt = pl.pallas_call(
        functools.partial(_route_kernel, n_groups=N_GROUPS, topk_groups=TOPK_GROUPS, top_k=TOP_K),
        out_shape=(jax.ShapeDtypeStruct((TOP_K, rows), jnp.int32), jax.ShapeDtypeStruct((TOP_K, rows), F32),
                   jax.ShapeDtypeStruct((TOP_K, rows), jnp.int32), jax.ShapeDtypeStruct((n_exp, LANES), jnp.int32)),
        grid=(rows // tm,),
        in_specs=[
            pl.BlockSpec((tm, d), lambda i: (i, 0)),
            pl.BlockSpec((n_exp, d), lambda i: (0, 0)),
            pl.BlockSpec((n_exp, LANES), lambda i: (0, 0)),
        ],
        out_specs=(pl.BlockSpec((TOP_K, tm), kt), pl.BlockSpec((TOP_K, tm), kt), pl.BlockSpec((TOP_K, tm), kt),
                   pl.BlockSpec((n_exp, LANES), lambda i: (0, 0))),
        scratch_shapes=[pltpu.VMEM((n_exp, d), BF16), pltpu.VMEM((tm, tm), BF16), pltpu.VMEM((n_exp, LANES), F32)],
        compiler_params=_cparams(1),
        name="route",
    )(h, w_router.T, bias)
    return eidx, wts, rank, cnt[:, 0]


def moe_ffn(h, h_packed, w_router, e_bias, w_gate, w_up, w_down, ws_gate, ws_up, ws_down, *, layer, rows):
    n_exp = w_router.shape[1]
    eidx, wts, rank, counts = route(h, w_router, e_bias, rows=rows)
    mb = MOE_BLOCK
    padded = (counts + mb - 1) // mb * mb
    pad_end = jnp.cumsum(padded)
    pad_start = pad_end - padded
    start_of = jnp.sum(jnp.where(eidx[:, :, None] == jnp.arange(n_exp)[None, None, :], pad_start, 0), axis=-1)
    dest = start_of + rank
    n_blocks = (rows * TOP_K + n_exp * (mb - 1) + mb - 1) // mb
    xg = dispatch_rows(h_packed, dest, counts, pad_start, pad_end, n_slots=n_blocks * mb, rows=rows)
    n_valid = (pad_end[-1] // mb).astype(jnp.int32)
    starts = jnp.arange(n_blocks, dtype=jnp.int32) * mb
    block_e = jnp.minimum(jnp.sum(starts[:, None] >= pad_end[None, :], axis=1), n_exp - 1).astype(jnp.int32)
    last_e = block_e[jnp.maximum(n_valid - 1, 0)]
    block_e = jnp.where(jnp.arange(n_blocks) < n_valid, block_e, last_e)
    seg_end = pad_end[block_e] // mb
    next_e = jnp.where(seg_end < n_valid, block_e[jnp.minimum(seg_end, n_blocks - 1)] + layer * n_exp, -1)
    y = routed_experts(xg, block_e + layer * n_exp, next_e.astype(jnp.int32), n_valid.reshape(1),
                       w_gate, w_up, w_down)
    shared = shared_expert(h, ws_gate, ws_up, ws_down, layer=layer, rows=rows)
    return y, dest, wts.T, shared


def kernel(x, c, ctx, c_ctx, w_mod, b_mod, norm_mix_pre, norm_mix_post, norm_ffn_pre, norm_ffn_post,
           w_in, s5_lam_re, s5_lam_im, s5_log_step, s5_b_re, s5_b_im, s5_c_re, s5_c_im, s5_d, s5_w_glu,
           dn_conv, dn_a_log, dn_dt_bias, dn_norm, w_br_s5, w_br_dn, w_out,
           moe_router, moe_bias, moe_w_gate, moe_w_up, moe_w_down, sh_w_gate, sh_w_up, sh_w_down):
    bsz, seq, d = x.shape
    n_ctx = ctx.shape[1]
    depth = w_mod.shape[0]
    lat_rows, ctx_rows = bsz * seq, bsz * n_ctx
    all_rows = lat_rows + ctx_rows
    s5_width = s5_d.shape[1]
    dn_width = w_br_dn.shape[1]
    n_heads = dn_a_log.shape[-1]
    main_cols = s5_width + 4 * dn_width
    ba_cols = 4 * n_heads
    lanes = 128
    nc = (seq + n_ctx) // S5_CHUNK
    n_levels = max(1, (nc - 1).bit_length())

    def set_of_tile(i, tm):
        return jnp.minimum((i * tm) // seq, bsz)

    xs = jnp.concatenate([x.reshape(lat_rows, d), ctx.reshape(ctx_rows, d)], axis=0)
    n_sets = 8
    cin = jnp.zeros((n_sets, d), F32).at[:bsz].set(_silu(c)).at[bsz].set(_silu(c_ctx))
    for i in range(depth):
        last = i == depth - 1
        rows = lat_rows if last else all_rows
        mods = matmul(cin, w_mod, b_mod[i].reshape(1, -1), layer=i, tm=n_sets, name="mods").reshape(n_sets, 6, 1, d)
        mod = [mods[:, k] for k in range(6)]
        hmix = prenorm(xs, norm_mix_pre[i], mod[0], mod[1], set_of_tile)
        p_main = matmul(hmix, w_in, layer=i, n_cols=main_cols, out_dtype=BF16, name="in_proj")
        p_ba = matmul(hmix, w_in, layer=i, col0=main_cols, n_cols=lanes, tn=lanes, name="in_proj_ba")
        gates = matmul_unaligned(hmix, w_in, layer=i, col0=main_cols + ba_cols, n_cols=2 * d, rows=rows,
                                 name="in_proj_gates")
        ops = s5_operators(s5_lam_re[i], s5_lam_im[i], s5_log_step[i], s5_b_re[i], s5_b_im[i],
                           s5_c_re[i], s5_c_im[i], n_levels)
        y_s5 = s5_mix(p_main[:, :s5_width], ops, s5_d[i], bsz=bsz, seq=seq, ctx=n_ctx)
        y_s5 = s5_glu(y_s5, s5_w_glu, layer=i, rows=rows)
        y_dn = deltanet_mix(p_main, p_ba, dn_conv[i], dn_a_log[i], dn_dt_bias[i], dn_norm[i],
                            bsz=bsz, seq=seq, ctx=n_ctx, u_width=s5_width)
        m = branch_merge(y_s5, y_dn, gates, w_br_s5, w_br_dn, layer=i, rows=rows)
        xs = outproj_residual(m, w_out, xs, norm_mix_post[i], mod[2], set_of_tile, layer=i, rows=rows)
        hffn, hpacked = prenorm(xs, norm_ffn_pre[i], mod[3], mod[4], set_of_tile, rows=rows, packed=True)
        y, dest, wts, shared = moe_ffn(hffn, hpacked, moe_router[i], moe_bias[i], moe_w_gate, moe_w_up, moe_w_down,
                                       sh_w_gate, sh_w_up, sh_w_down, layer=i, rows=rows)
        xs = ffn_residual(y, dest, wts, shared, xs, norm_ffn_post[i], mod[5], set_of_tile, rows=rows)
    return xs[:lat_rows].reshape(bsz, seq, d)
```
